```python
import math
import jax
import jax.numpy as jnp
from jax import lax
import numpy as np

D_MODEL = 1024
BATCH = 8
SEQ = 2048
DEPTH = 2
DEC_BATCH = 16
DEC_SEQ = 64
PAST_LEN = 1024

CHUNK = 64
H_A = 4
DK_A = 128
DV_A = 128
CONV_W = 4
H_B = 4
DH_B = 64
BAND_CHUNKS = 8
REL_CLIP = 128
H_C = 4
DK_C = 64
DV_C = 64
ROPE_BASE = 10000.0
D_FF = 2816
EPS = 1e-6
NEG_INF = -1e30

MIX_A = H_A * DV_A
MIX_B = H_B * DH_B
MIX_C = H_C * DV_C
MIX = MIX_A + MIX_B + MIX_C
CONV_DIM = 2 * H_A * DK_A + H_A * DV_A
BAND = BAND_CHUNKS * CHUNK
IN_SIZES = (CONV_DIM, MIX_A, H_A, H_A, MIX_B, MIX_B, MIX_B, H_C * DK_C, H_C * DK_C, MIX_C, MIX_C)
IN_COLS = sum(IN_SIZES)

kernel_name = "hybrid_streaming_encoder_step"


def rmsnorm(x, g):
    xf = x.astype(jnp.float32)
    y = xf * lax.rsqrt(jnp.mean(xf * xf, axis=-1, keepdims=True) + EPS)
    return (y * g.astype(jnp.float32)).astype(x.dtype)


def head_layernorm(x):
    xc = x - jnp.mean(x, axis=-1, keepdims=True)
    return xc * lax.rsqrt(jnp.mean(xc * xc, axis=-1, keepdims=True) + EPS)


def l2norm(x):
    return x * lax.rsqrt(jnp.sum(x * x, axis=-1, keepdims=True) + EPS)


def swiglu(x, w_gate, w_up, w_down):
    return (jax.nn.silu(x @ w_gate) * (x @ w_up)) @ w_down


def split_cols(u, sizes):
    out, o = [], 0
    for s in sizes:
        out.append(u[..., o:o + s])
        o += s
    return out


def rope(x, pos):
    half = x.shape[-1] // 2
    inv_freq = jnp.exp(-math.log(ROPE_BASE) * jnp.arange(half, dtype=jnp.float32) / half)
    ang = pos.astype(jnp.float32)[:, None] * inv_freq[None, :]
    cos, sin = jnp.cos(ang)[:, None, :], jnp.sin(ang)[:, None, :]
    x1, x2 = x[..., :half], x[..., half:]
    return jnp.concatenate([x1 * cos - x2 * sin, x1 * sin + x2 * cos], axis=-1)


def causal_conv(u, buf, w):
    L = u.shape[1]
    up = jnp.concatenate([buf.astype(u.dtype), u], axis=1)
    y = up[:, 0:L] * w[0]
    for j in range(1, CONV_W):
        y = y + up[:, j:j + L] * w[j]
    return jax.nn.silu(y), up[:, -(CONV_W - 1):]


def chunk_scan(step, state, xs):
    L = xs[0].shape[2]
    cl = min(CHUNK, L)
    nc = L // cl

    def to_chunks(a):
        a = a.reshape(a.shape[:2] + (nc, cl) + a.shape[3:])
        return jnp.moveaxis(a, 2, 0)

    state, out = lax.scan(lambda s, c: step(s, *c), state, tuple(to_chunks(a) for a in xs))
    out = jnp.moveaxis(out, 0, 2)
    return state, out.reshape(out.shape[:2] + (L,) + out.shape[4:])


def gdn_chunk(S, q, k, v, g, beta):
    L = q.shape[2]
    decay = jnp.cumsum(g, axis=-1)
    tril = jnp.tril(jnp.ones((L, L), bool))
    strict = jnp.tril(jnp.ones((L, L), bool), -1)
    diff = decay[..., :, None] - decay[..., None, :]
    lmask = jnp.where(tril, jnp.exp(jnp.where(tril, diff, 0.0)), 0.0)
    kb = k * beta[..., None]
    a_low = jnp.where(strict, jnp.einsum('bhik,bhjk->bhij', kb, k) * lmask, 0.0)
    m = a_low + jnp.eye(L, dtype=jnp.float32)
    rhs = jnp.concatenate([v * beta[..., None], kb * jnp.exp(decay)[..., None]], axis=-1)
    sol = lax.linalg.triangular_solve(m, rhs, left_side=True, lower=True, unit_diagonal=True)
    val, kcd = sol[..., :DV_A], sol[..., DV_A:]
    v_new = val - jnp.einsum('bhlk,bhkv->bhlv', kcd, S)
    attn = jnp.einsum('bhik,bhjk->bhij', q, k) * lmask
    o = (jnp.einsum('bhlk,bhkv->bhlv', q * jnp.exp(decay)[..., None], S)
         + jnp.einsum('bhij,bhjv->bhiv', attn, v_new))
    dl = decay[..., -1:]
    S_new = (S * jnp.exp(dl)[..., None]
             + jnp.einsum('bhlk,bhlv->bhkv', k * jnp.exp(dl - decay)[..., None], v_new))
    return S_new, o


def retention_log_decay():
    return jnp.log1p(-jnp.exp2(-5.0 - jnp.arange(H_C, dtype=jnp.float32)))


def ret_chunk(R, q, k, v, lg):
    L = q.shape[2]
    n = jnp.arange(L, dtype=jnp.float32)
    diff = n[:, None] - n[None, :]
    causal = diff >= 0
    dmask = jnp.where(causal, jnp.exp(lg[:, None, None] * jnp.where(causal, diff, 0.0)), 0.0)
    inner = jnp.einsum('bhij,bhjv->bhiv', jnp.einsum('bhik,bhjk->bhij', q, k) * dmask, v)
    cross = jnp.einsum('bhlk,bhkv->bhlv', q * jnp.exp(lg[:, None] * (n + 1.0))[..., None], R)
    R_new = (R * jnp.exp(lg * L)[:, None, None]
             + jnp.einsum('bhlk,bhlv->bhkv', k * jnp.exp(lg[:, None] * (L - 1.0 - n))[..., None], v))
    return R_new, inner + cross


def band_attn_prompt(q, k, v, rel_bias):
    B_, T = q.shape[:2]
    nc = T // CHUNK
    f32 = jnp.float32

    def chunks(a):
        return a.astype(f32).reshape(B_, nc, CHUNK, H_B, DH_B)

    def band(a):
        ap = jnp.pad(chunks(a), ((0, 0), (BAND_CHUNKS, 0), (0, 0), (0, 0), (0, 0)))
        return jnp.concatenate([ap[:, o:o + nc] for o in range(BAND_CHUNKS + 1)], axis=2)

    kband, vband = band(k), band(v)
    s = jnp.einsum('bnqhd,bnkhd->bnhqk', chunks(q), kband) * DH_B ** -0.5
    j = jnp.arange((BAND_CHUNKS + 1) * CHUNK)
    i = jnp.arange(CHUNK)
    rel = j[None, :] - BAND - i[:, None]
    s = s + rel_bias.astype(f32)[:, jnp.clip(rel, -REL_CLIP, REL_CLIP) + REL_CLIP]
    valid = (jnp.arange(nc)[:, None] - BAND_CHUNKS + j[None, :] // CHUNK) >= 0
    s = jnp.where(valid[None, :, None, None, :], s, NEG_INF)
    p = jax.nn.softmax(s, axis=-1)
    o = jnp.einsum('bnhqk,bnkhd->bnqhd', p, vband)
    return o.reshape(B_, T, MIX_B)


def band_attn_sample(q, k, v, ck, cv, rel_bias):
    B_, L = q.shape[:2]
    P = ck.shape[1]
    f32 = jnp.float32
    kk = jnp.concatenate([ck.astype(f32), k.astype(f32)], axis=1)
    vv = jnp.concatenate([cv.astype(f32), v.astype(f32)], axis=1)
    s = jnp.einsum('bqhd,bkhd->bhqk', q.astype(f32), kk) * DH_B ** -0.5
    rel = (jnp.arange(P + L) - P)[None, :] - jnp.arange(L)[:, None]
    s = s + rel_bias.astype(f32)[:, jnp.clip(rel, -REL_CLIP, REL_CLIP) + REL_CLIP]
    p = jax.nn.softmax(s, axis=-1)
    o = jnp.einsum('bhqk,bkhd->bqhd', p, vv)
    return o.reshape(B_, L, MIX_B)


def token_mix(h, w_in, conv_w, a_log, dt_bias, gdn_norm, rel_bias, w_out,
              conv_buf, s_gdn, s_ret, band_k, band_v, pos0):
    B_, L, _ = h.shape
    f32 = jnp.float32
    (u_qkv, z_a, b_a, a_a, q_b, k_b, v_b, q_c, k_c, v_c, g_c) = split_cols(h @ w_in, IN_SIZES)
    hs = lambda a: jnp.swapaxes(a, 1, 2)

    u_conv, conv_new = causal_conv(u_qkv, conv_buf, conv_w)
    qa, ka, va = split_cols(u_conv.astype(f32), (H_A * DK_A, H_A * DK_A, H_A * DV_A))
    qa = l2norm(qa.reshape(B_, L, H_A, DK_A)) * DK_A ** -0.5
    ka = l2norm(ka.reshape(B_, L, H_A, DK_A))
    va = va.reshape(B_, L, H_A, DV_A)
    g = -jnp.exp(a_log.astype(f32)) * jax.nn.softplus(a_a.astype(f32) + dt_bias.astype(f32))
    beta = jax.nn.sigmoid(b_a.astype(f32))
    s_gdn_new, oa = chunk_scan(gdn_chunk, s_gdn.astype(f32), (hs(qa), hs(ka), hs(va), hs(g), hs(beta)))
    oa = rmsnorm(hs(oa), gdn_norm) * jax.nn.silu(z_a.astype(f32).reshape(B_, L, H_A, DV_A))

    qb = q_b.reshape(B_, L, H_B, DH_B)
    kb = k_b.reshape(B_, L, H_B, DH_B)
    vb = v_b.reshape(B_, L, H_B, DH_B)
    if band_k is None:
        ob = band_attn_prompt(qb, kb, vb, rel_bias)
        keep = min(BAND, L)
        kb_new, vb_new = kb[:, -keep:], vb[:, -keep:]
    else:
        ob = band_attn_sample(qb, kb, vb, band_k.astype(kb.dtype), band_v.astype(vb.dtype), rel_bias)
        kb_new, vb_new = kb, vb

    pos = pos0 + jnp.arange(L)
    qc = rope(q_c.astype(f32).reshape(B_, L, H_C, DK_C), pos)
    kc = rope(k_c.astype(f32).reshape(B_, L, H_C, DK_C), pos) * DK_C ** -0.5
    vc = v_c.astype(f32).reshape(B_, L, H_C, DV_C)
    lg = retention_log_decay()
    s_ret_new, oc = chunk_scan(lambda s, q, k, v: ret_chunk(s, q, k, v, lg), s_ret.astype(f32),
                               (hs(qc), hs(kc), hs(vc)))
    oc = head_layernorm(hs(oc)) * jax.nn.silu(g_c.astype(f32).reshape(B_, L, H_C, DV_C))

    o = jnp.concatenate([oa.reshape(B_, L, MIX_A), ob.reshape(B_, L, MIX_B), oc.reshape(B_, L, MIX_C)], axis=-1)
    out = o.astype(h.dtype) @ w_out
    return out, (s_gdn_new, conv_new, kb_new, vb_new, s_ret_new)


def setup_inputs(seed: int = 0) -> dict:
    key = jax.random.key(seed)
    ks = jax.random.split(key, 32)
    f32 = jnp.float32
    nrm = lambda k, shape, scale: jax.random.normal(k, shape, f32) * scale
    band_len = min(BAND, PAST_LEN)
    dt = jnp.exp(jax.random.uniform(ks[20], (DEPTH, H_A), f32, math.log(1e-3), math.log(1e-1)))
    return {
        "x_prompt": nrm(ks[0], (BATCH, SEQ, D_MODEL), 1.0),
        "x_sample": nrm(ks[1], (DEC_BATCH, DEC_SEQ, D_MODEL), 1.0),
        "state_gdn": nrm(ks[2], (DEPTH, DEC_BATCH, H_A, DK_A, DV_A), DK_A ** -0.5),
        "state_conv": nrm(ks[3], (DEPTH, DEC_BATCH, CONV_W - 1, CONV_DIM), 1.0),
        "cache_band_k": nrm(ks[4], (DEPTH, DEC_BATCH, band_len, H_B, DH_B), 1.0),
        "cache_band_v": nrm(ks[5], (DEPTH, DEC_BATCH, band_len, H_B, DH_B), 1.0),
        "state_ret": nrm(ks[6], (DEPTH, DEC_BATCH, H_C, DK_C, DV_C), 0.5),
        "norm_ffn1": 1.0 + nrm(ks[7], (DEPTH, D_MODEL), 0.02),
        "w_ffn1_gate": nrm(ks[8], (DEPTH, D_MODEL, D_FF), D_MODEL ** -0.5),
        "w_ffn1_up": nrm(ks[9], (DEPTH, D_MODEL, D_FF), D_MODEL ** -0.5),
        "w_ffn1_down": nrm(ks[10], (DEPTH, D_FF, D_MODEL), D_FF ** -0.5),
        "norm_mix": 1.0 + nrm(ks[11], (DEPTH, D_MODEL), 0.02),
        "w_in": nrm(ks[12], (DEPTH, D_MODEL, IN_COLS), D_MODEL ** -0.5),
        "conv_w": nrm(ks[13], (DEPTH, CONV_W, CONV_DIM), CONV_W ** -0.5),
        "a_log": jnp.log(jax.random.uniform(ks[14], (DEPTH, H_A), f32, 1.0, 16.0)),
        "dt_bias": dt + jnp.log(-jnp.expm1(-dt)),
        "gdn_norm": 1.0 + nrm(ks[15], (DEPTH, DV_A), 0.02),
        "rel_bias": nrm(ks[16], (DEPTH, H_B, 2 * REL_CLIP + 1), 0.1),
        "w_out": nrm(ks[17], (DEPTH, MIX, D_MODEL), MIX ** -0.5),
        "norm_ffn2": 1.0 + nrm(ks[18], (DEPTH, D_MODEL), 0.02),
        "w_ffn2_gate": nrm(ks[19], (DEPTH, D_MODEL, D_FF), D_MODEL ** -0.5),
        "w_ffn2_up": nrm(ks[21], (DEPTH, D_MODEL, D_FF), D_MODEL ** -0.5),
        "w_ffn2_down": nrm(ks[22], (DEPTH, D_FF, D_MODEL), D_FF ** -0.5),
        "norm_final": 1.0 + nrm(ks[23], (D_MODEL,), 0.02),
    }


def reference(x_prompt, x_sample, state_gdn, state_conv, cache_band_k, cache_band_v, state_ret,
              norm_ffn1, w_ffn1_gate, w_ffn1_up, w_ffn1_down, norm_mix, w_in, conv_w, a_log, dt_bias,
              gdn_norm, rel_bias, w_out, norm_ffn2, w_ffn2_gate, w_ffn2_up, w_ffn2_down, norm_final):

    def run(x, pos0, use_cache):
        B_ = x.shape[0]
        new = []
        for l in range(DEPTH):
            if use_cache:
                c = (state_conv[l], state_gdn[l], state_ret[l], cache_band_k[l], cache_band_v[l])
            else:
                c = (jnp.zeros((B_, CONV_W - 1, CONV_DIM), x.dtype),
                     jnp.zeros((B_, H_A, DK_A, DV_A), jnp.float32),
                     jnp.zeros((B_, H_C, DK_C, DV_C), jnp.float32), None, None)
            x = x + 0.5 * swiglu(rmsnorm(x, norm_ffn1[l]), w_ffn1_gate[l], w_ffn1_up[l], w_ffn1_down[l])
            m, st = token_mix(rmsnorm(x, norm_mix[l]), w_in[l], conv_w[l], a_log[l], dt_bias[l], gdn_norm[l],
                              rel_bias[l], w_out[l], c[0], c[1], c[2], c[3], c[4], pos0)
            x = x + m
            x = x + 0.5 * swiglu(rmsnorm(x, norm_ffn2[l]), w_ffn2_gate[l], w_ffn2_up[l], w_ffn2_down[l])
            new.append(st)
        y = rmsnorm(x, norm_final)
        g_, c_, k_, v_, r_ = zip(*new)
        return y, jnp.stack(g_), jnp.stack(c_), jnp.stack(k_), jnp.stack(v_), jnp.stack(r_)

    y_prompt, p_gdn, p_conv, p_band_k, p_band_v, p_ret = run(x_prompt, 0, False)
    y_sample, s_gdn, s_conv, s_band_k, s_band_v, s_ret = run(x_sample, PAST_LEN, True)
    return (y_prompt, y_sample, p_gdn, p_conv, p_band_k, p_band_v, p_ret,
            s_gdn, s_conv, s_band_k, s_band_v, s_ret)
```

```python
import functools
import math

import jax
import jax.numpy as jnp
from jax import lax
from jax.experimental import pallas as pl
from jax.experimental.pallas import tpu as pltpu

F32 = jnp.float32
BF16 = jnp.bfloat16

D_MODEL = 1024
DEPTH = 2
CHUNK = 64
H_A, DK_A, DV_A, CONV_W = 4, 128, 128, 4
H_B, DH_B, BAND_CHUNKS, REL_CLIP = 4, 64, 8, 128
H_C, DK_C, DV_C = 4, 64, 64
ROPE_BASE = 10000.0
D_FF = 2816
EPS = 1e-6
NEG_INF = -1e30
BAND = BAND_CHUNKS * CHUNK
QKV_A = 3 * H_A * DK_A
MIX_A = H_A * DV_A
MIX_B = H_B * DH_B
MIX_C = H_C * DV_C
LANES = 128
SUBLANES = 8
U_COLS = QKV_A + MIX_A + 3 * MIX_B + 4 * MIX_C + LANES
COL_Z = QKV_A // MIX_A
COL_B = (QKV_A + MIX_A) // MIX_B
COL_BA = (U_COLS - LANES) // LANES
VMEM_LIMIT = 56 * 1024 * 1024

FFN_TM = 1024
FFN_TF = 256
PROJ_TM = 256
MIX_TB = 256


def _params(sem):
    return pltpu.CompilerParams(dimension_semantics=sem, vmem_limit_bytes=VMEM_LIMIT)


def _rms(x, g):
    return (x * lax.rsqrt(jnp.mean(x * x, axis=-1, keepdims=True) + EPS)) * g


def _silu(x):
    return x * jax.nn.sigmoid(x)


def _dot(a, b):
    return jnp.dot(a.astype(BF16), b.astype(BF16), preferred_element_type=F32)


def _dot_nt(a, b):
    return lax.dot_general(a.astype(BF16), b.astype(BF16), (((1,), (1,)), ((), ())),
                           preferred_element_type=F32)


def _dot_tn(a, b):
    return lax.dot_general(a.astype(BF16), b.astype(BF16), (((0,), (0,)), ((), ())),
                           preferred_element_type=F32)


def _dot_exact_rhs(a, b):
    hi = a.astype(BF16)
    r1 = a - hi.astype(F32)
    mid = r1.astype(BF16)
    lo = (r1 - mid.astype(F32)).astype(BF16)
    bb = b.astype(BF16)
    out = jnp.dot(hi, bb, preferred_element_type=F32)
    out = out + jnp.dot(mid, bb, preferred_element_type=F32)
    return out + jnp.dot(lo, bb, preferred_element_type=F32)


def _ffn_steps(h_scr, acc_scr, wg_ref, wu_ref, wd_ref):
    h = h_scr[...]
    gate = jnp.dot(h, wg_ref[...], preferred_element_type=F32)
    up = jnp.dot(h, wu_ref[...], preferred_element_type=F32)
    act = (_silu(gate) * up).astype(BF16)
    acc_scr[...] += jnp.dot(act, wd_ref[...], preferred_element_type=F32)


def _ffn_kernel(x_ref, g_ref, wg_ref, wu_ref, wd_ref, o_ref, h_scr, acc_scr):
    j = pl.program_id(1)

    @pl.when(j == 0)
    def _():
        h_scr[...] = _rms(x_ref[...], g_ref[...]).astype(BF16)
        acc_scr[...] = jnp.zeros_like(acc_scr)

    _ffn_steps(h_scr, acc_scr, wg_ref, wu_ref, wd_ref)

    @pl.when(j == pl.num_programs(1) - 1)
    def _():
        o_ref[...] = x_ref[...] + 0.5 * acc_scr[...]


def _mix_ffn_kernel(x_ref, oa_ref, ob_ref, oc_ref, woa_ref, wob_ref, woc_ref, g_ref, wg_ref, wu_ref,
                    wd_ref, gf_ref, o_ref, x_scr, h_scr, acc_scr, *, final_norm):
    j = pl.program_id(1)

    @pl.when(j == 0)
    def _():
        mix = jnp.dot(oa_ref[...], woa_ref[...], preferred_element_type=F32)
        mix = mix + jnp.dot(ob_ref[...], wob_ref[...], preferred_element_type=F32)
        mix = mix + jnp.dot(oc_ref[...], woc_ref[...], preferred_element_type=F32)
        x2 = x_ref[...] + mix
        x_scr[...] = x2
        h_scr[...] = _rms(x2, g_ref[...]).astype(BF16)
        acc_scr[...] = jnp.zeros_like(acc_scr)

    _ffn_steps(h_scr, acc_scr, wg_ref, wu_ref, wd_ref)

    @pl.when(j == pl.num_programs(1) - 1)
    def _():
        y = x_scr[...] + 0.5 * acc_scr[...]
        if final_norm:
            y = _rms(y, gf_ref[...])
        o_ref[...] = y


def _ffn_weight_specs(layer, tf):
    return [
        pl.BlockSpec((None, D_MODEL, tf), lambda i, j: (layer, 0, j)),
        pl.BlockSpec((None, D_MODEL, tf), lambda i, j: (layer, 0, j)),
        pl.BlockSpec((None, tf, D_MODEL), lambda i, j: (layer, j, 0)),
    ]


def _ffn(x, g, wg, wu, wd, layer):
    n = x.shape[0]
    tm = min(FFN_TM, n)
    assert n % tm == 0, (n, tm)
    tok = pl.BlockSpec((tm, D_MODEL), lambda i, j: (i, 0))
    return pl.pallas_call(
        _ffn_kernel,
        grid=(n // tm, D_FF // FFN_TF),
        in_specs=[tok, pl.BlockSpec((None, 1, D_MODEL), lambda i, j: (layer, 0, 0))]
        + _ffn_weight_specs(layer, FFN_TF),
        out_specs=tok,
        out_shape=jax.ShapeDtypeStruct((n, D_MODEL), F32),
        scratch_shapes=[pltpu.VMEM((tm, D_MODEL), BF16), pltpu.VMEM((tm, D_MODEL), F32)],
        compiler_params=_params(("parallel", "arbitrary")),
        name="ffn",
    )(x, g, wg, wu, wd)


def _mix_ffn(x, oa, ob, oc, w_out, g, wg, wu, wd, g_final, layer, final_norm):
    n = x.shape[0]
    tm = min(FFN_TM, n)
    assert n % tm == 0, (n, tm)
    tok = pl.BlockSpec((tm, D_MODEL), lambda i, j: (i, 0))
    in_specs = [
        tok,
        pl.BlockSpec((tm, MIX_A), lambda i, j: (i, 0)),
        pl.BlockSpec((tm, MIX_B), lambda i, j: (i, 0)),
        pl.BlockSpec((tm, MIX_C), lambda i, j: (i, 0)),
        pl.BlockSpec((None, MIX_A, D_MODEL), lambda i, j: (layer, 0, 0)),
        pl.BlockSpec((None, MIX_B, D_MODEL), lambda i, j: (layer, MIX_A // MIX_B, 0)),
        pl.BlockSpec((None, MIX_C, D_MODEL), lambda i, j: (layer, (MIX_A + MIX_B) // MIX_C, 0)),
        pl.BlockSpec((None, 1, D_MODEL), lambda i, j: (layer, 0, 0)),
    ] + _ffn_weight_specs(layer, FFN_TF) + [pl.BlockSpec((1, D_MODEL), lambda i, j: (0, 0))]
    return pl.pallas_call(
        functools.partial(_mix_ffn_kernel, final_norm=final_norm),
        grid=(n // tm, D_FF // FFN_TF),
        in_specs=in_specs,
        out_specs=tok,
        out_shape=jax.ShapeDtypeStruct((n, D_MODEL), F32),
        scratch_shapes=[pltpu.VMEM((tm, D_MODEL), F32), pltpu.VMEM((tm, D_MODEL), BF16),
                        pltpu.VMEM((tm, D_MODEL), F32)],
        compiler_params=_params(("parallel", "arbitrary")),
        name="mix_ffn",
    )(x, oa, ob, oc, w_out, w_out, w_out, g, wg, wu, wd, g_final)


def _inproj_kernel(x_ref, g_ref, w_ref, o_ref):
    h = _rms(x_ref[...], g_ref[...]).astype(BF16)
    o_ref[...] = jnp.dot(h, w_ref[...], preferred_element_type=F32)


def _inproj(x, g, w, layer):
    n = x.shape[0]
    tm = min(PROJ_TM, n)
    assert n % tm == 0, (n, tm)
    return pl.pallas_call(
        _inproj_kernel,
        grid=(n // tm,),
        in_specs=[pl.BlockSpec((tm, D_MODEL), lambda i: (i, 0)),
                  pl.BlockSpec((None, 1, D_MODEL), lambda i: (layer, 0, 0)),
                  pl.BlockSpec((None, D_MODEL, U_COLS), lambda i: (layer, 0, 0))],
        out_specs=pl.BlockSpec((tm, U_COLS), lambda i: (i, 0)),
        out_shape=jax.ShapeDtypeStruct((n, U_COLS), F32),
        compiler_params=_params(("parallel",)),
        name="inproj",
    )(x, g, w)


def _gdn_kernel(u_ref, z_ref, ba_ref, convw_ref, prm_ref, s0_ref, cb0_ref, oa_ref, sout_ref,
                xbuf, ybuf, dbuf, bbuf, s_scr, *, tb):
    t = pl.program_id(1)
    nchunks = tb // CHUNK

    @pl.when(t == 0)
    def _():
        xbuf[0:SUBLANES, :] = cb0_ref[0]
        s_scr[...] = s0_ref[0]

    xbuf[SUBLANES:SUBLANES + tb, :] = u_ref[0]
    y = xbuf[5:5 + tb, :] * convw_ref[0:1, :]
    for j in range(1, CONV_W):
        y = y + xbuf[5 + j:5 + j + tb, :] * convw_ref[j:j + 1, :]
    ybuf[...] = _silu(y)
    xbuf[0:SUBLANES, :] = xbuf[tb:tb + SUBLANES, :]

    ba = ba_ref[0]
    x = ba + prm_ref[1:2, :]
    softplus = jnp.maximum(x, 0.0) + jnp.log1p(jnp.exp(-jnp.abs(x)))
    g = prm_ref[0:1, :] * softplus
    bbuf[...] = jax.nn.sigmoid(ba)
    row = lax.broadcasted_iota(jnp.int32, (tb, tb), 0)
    col = lax.broadcasted_iota(jnp.int32, (tb, tb), 1)
    shift = CHUNK.bit_length() - 1
    same_chunk = jnp.right_shift(row, shift) == jnp.right_shift(col, shift)
    tri_bd = jnp.where(same_chunk & (col <= row), 1.0, 0.0)
    dbuf[...] = _cumsum_rows(g, tri_bd)

    r_i = lax.broadcasted_iota(jnp.int32, (CHUNK, CHUNK), 0)
    c_i = lax.broadcasted_iota(jnp.int32, (CHUNK, CHUNK), 1)
    tril = c_i <= r_i
    strict = c_i < r_i
    eye = c_i == r_i
    eye_f = jnp.where(eye, 1.0, 0.0)
    gnorm = prm_ref[2:3, :]

    def chunk_body(c, carry):
        r0 = pl.multiple_of(c * CHUNK, CHUNK)
        rows = pl.ds(r0, CHUNK)
        dch = dbuf[rows, :]
        beta = bbuf[rows, :]
        dlast = dch[CHUNK - 1:CHUNK, :]
        e_d = jnp.exp(dch)
        e_dec = jnp.exp(dlast - dch)
        e_last = jnp.exp(dlast)
        for h in range(H_A):
            q = ybuf[rows, h * DK_A:(h + 1) * DK_A]
            k = ybuf[rows, MIX_A + h * DK_A:MIX_A + (h + 1) * DK_A]
            v = ybuf[rows, 2 * MIX_A + h * DV_A:2 * MIX_A + (h + 1) * DV_A]
            q = (q * lax.rsqrt(jnp.sum(q * q, axis=-1, keepdims=True) + EPS)) * DK_A ** -0.5
            k = k * lax.rsqrt(jnp.sum(k * k, axis=-1, keepdims=True) + EPS)
            dcol = dch[:, 4 + h:5 + h]
            bcol = beta[:, h:h + 1]
            edcol = e_d[:, 4 + h:5 + h]
            drow = jnp.sum(jnp.where(eye, dcol, 0.0), axis=0, keepdims=True)
            diff = dcol - drow
            lmask = jnp.where(tril, jnp.exp(jnp.where(tril, diff, 0.0)), 0.0)
            kb = k * bcol
            qk = _dot_nt(jnp.concatenate([q, kb], axis=0), k)
            attn = qk[0:CHUNK] * lmask
            a_low = jnp.where(strict, qk[CHUNK:2 * CHUNK] * lmask, 0.0)
            pw = a_low
            inv = eye_f - a_low
            for _ in range(5):
                pw = _dot(pw, pw)
                inv = inv + _dot(inv, pw)
            rhs = jnp.concatenate([v * bcol, kb * edcol], axis=1)
            sol = _dot(inv, rhs)
            val, kcd = sol[:, :DV_A], sol[:, DV_A:]
            s_h = s_scr[h]
            v_new = val - _dot(kcd, s_h)
            o = _dot(q * edcol, s_h) + _dot(attn, v_new)
            s_scr[h] = s_h * e_last[:, 4 + h:5 + h] + _dot_tn(k * e_dec[:, 4 + h:5 + h], v_new)
            zz = z_ref[0, rows, h * DV_A:(h + 1) * DV_A]
            oa_ref[0, rows, h * DV_A:(h + 1) * DV_A] = (_rms(o, gnorm) * _silu(zz)).astype(BF16)
        return carry

    lax.fori_loop(0, nchunks, chunk_body, 0)

    @pl.when(t == pl.num_programs(1) - 1)
    def _():
        sout_ref[0] = s_scr[...]


def _cumsum_rows(g, tri_bd):
    hi = g.astype(BF16)
    r1 = g - hi.astype(F32)
    mid = r1.astype(BF16)
    lo = (r1 - mid.astype(F32)).astype(BF16)
    tb = tri_bd.astype(BF16)
    out = jnp.dot(tb, hi, preferred_element_type=F32)
    out = out + jnp.dot(tb, mid, preferred_element_type=F32)
    return out + jnp.dot(tb, lo, preferred_element_type=F32)


def _gdn(u, conv_w, prm, s0, cb0, tb):
    b, t, _ = u.shape
    return pl.pallas_call(
        functools.partial(_gdn_kernel, tb=tb),
        grid=(b, t // tb),
        in_specs=[pl.BlockSpec((1, tb, QKV_A), lambda i, j: (i, j, 0)),
                  pl.BlockSpec((1, tb, MIX_A), lambda i, j: (i, j, COL_Z)),
                  pl.BlockSpec((1, tb, LANES), lambda i, j: (i, j, COL_BA)),
                  pl.BlockSpec((CONV_W, QKV_A), lambda i, j: (0, 0)),
                  pl.BlockSpec((SUBLANES, LANES), lambda i, j: (0, 0)),
                  pl.BlockSpec((1, H_A, DK_A, DV_A), lambda i, j: (i, 0, 0, 0)),
                  pl.BlockSpec((1, SUBLANES, QKV_A), lambda i, j: (i, 0, 0))],
        out_specs=[pl.BlockSpec((1, tb, MIX_A), lambda i, j: (i, j, 0)),
                   pl.BlockSpec((1, H_A, DK_A, DV_A), lambda i, j: (i, 0, 0, 0))],
        out_shape=[jax.ShapeDtypeStruct((b, t, MIX_A), BF16),
                   jax.ShapeDtypeStruct((b, H_A, DK_A, DV_A), F32)],
        scratch_shapes=[pltpu.VMEM((tb + SUBLANES, QKV_A), F32), pltpu.VMEM((tb, QKV_A), F32),
                        pltpu.VMEM((tb, LANES), F32), pltpu.VMEM((tb, LANES), F32),
                        pltpu.VMEM((H_A, DK_A, DV_A), F32)],
        compiler_params=_params(("parallel", "arbitrary")),
        name="gdn",
    )(u, u, u, conv_w, prm, s0, cb0)


def _band_kernel(q_ref, k_ref, v_ref, bias_ref, ck_ref, cv_ref, o_ref, kbuf, vbuf, *, qb, t_len, n_invalid):
    t = pl.program_id(1)

    @pl.when(t == 0)
    def _():
        kbuf[0:BAND, :] = ck_ref[0].astype(BF16)
        vbuf[0:BAND, :] = cv_ref[0].astype(BF16)
        kbuf[BAND:BAND + t_len, :] = k_ref[0].astype(BF16)
        vbuf[BAND:BAND + t_len, :] = v_ref[0].astype(BF16)

    start = pl.multiple_of(t * qb, qb)
    kw = kbuf[pl.ds(start, qb + BAND), :]
    vw = vbuf[pl.ds(start, qb + BAND), :]
    q = q_ref[0] * DH_B ** -0.5
    lane = lax.broadcasted_iota(jnp.int32, (1, MIX_B), 1)
    if n_invalid:
        kpos = lax.broadcasted_iota(jnp.int32, (qb, qb + BAND), 1) + start
        valid = kpos >= n_invalid
    acc = jnp.zeros((qb, MIX_B), F32)
    for h in range(H_B):
        mh = (lane >= h * DH_B) & (lane < (h + 1) * DH_B)
        s = _dot_nt(jnp.where(mh, q, 0.0), kw) + bias_ref[h]
        if n_invalid:
            s = jnp.where(valid, s, NEG_INF)
        p = jnp.exp(s - jnp.max(s, axis=-1, keepdims=True))
        o = _dot(p, vw) / jnp.sum(p, axis=-1, keepdims=True)
        acc = acc + jnp.where(mh, o, 0.0)
    o_ref[0] = acc.astype(BF16)


def _band(u, bias, ck, cv, qb, n_invalid):
    b, t, _ = u.shape
    cache_map = (lambda i, j: (i, 0, 0)) if ck.shape[0] == b else (lambda i, j: (0, 0, 0))
    return pl.pallas_call(
        functools.partial(_band_kernel, qb=qb, t_len=t, n_invalid=n_invalid),
        grid=(b, t // qb),
        in_specs=[pl.BlockSpec((1, qb, MIX_B), lambda i, j: (i, j, COL_B)),
                  pl.BlockSpec((1, t, MIX_B), lambda i, j: (i, 0, COL_B + 1)),
                  pl.BlockSpec((1, t, MIX_B), lambda i, j: (i, 0, COL_B + 2)),
                  pl.BlockSpec((H_B, qb, qb + BAND), lambda i, j: (0, 0, 0)),
                  pl.BlockSpec((1, BAND, MIX_B), cache_map),
                  pl.BlockSpec((1, BAND, MIX_B), cache_map)],
        out_specs=pl.BlockSpec((1, qb, MIX_B), lambda i, j: (i, j, 0)),
        out_shape=jax.ShapeDtypeStruct((b, t, MIX_B), BF16),
        scratch_shapes=[pltpu.VMEM((BAND + t, MIX_B), BF16), pltpu.VMEM((BAND + t, MIX_B), BF16)],
        compiler_params=_params(("parallel", "arbitrary")),
        name="band",
    )(u, u, u, bias, ck, cv)


def _band_bias_table(rel_bias, qb):
    i = jnp.arange(qb)[:, None]
    j = jnp.arange(qb + BAND)[None, :]
    rel = j - BAND - i
    tab = rel_bias.astype(F32)[:, jnp.clip(rel, -REL_CLIP, REL_CLIP) + REL_CLIP]
    ci, cj = i // CHUNK, j // CHUNK
    return jnp.where((cj >= ci) & (cj <= ci + BAND_CHUNKS), tab, NEG_INF)


def _ret_kernel(q_ref, k_ref, v_ref, g_ref, rope_ref, pow_ref, dmask_ref, sq_ref, r0_ref, o_ref, rout_ref,
                r_scr, *, tb):
    t = pl.program_id(1)

    @pl.when(t == 0)
    def _():
        r_scr[...] = r0_ref[0]

    def rope(x):
        return (x * rope_ref[0] + pltpu.roll(x, MIX_C - DK_C // 2, 1) * rope_ref[1]
                + pltpu.roll(x, DK_C // 2, 1) * rope_ref[2])

    q = rope(q_ref[0])
    k = rope(k_ref[0]) * DK_C ** -0.5
    v = v_ref[0]
    r = r_scr[...]
    acc = _dot(q * pow_ref[0], r)
    lane = lax.broadcasted_iota(jnp.int32, (1, MIX_C), 1)
    for h in range(H_C):
        mh = (lane >= h * DK_C) & (lane < (h + 1) * DK_C)
        s = _dot_nt(jnp.where(mh, q, 0.0), k) * dmask_ref[h]
        acc = acc + jnp.where(mh, _dot(s, v), 0.0)
    r_scr[...] = r * sq_ref[0] + sq_ref[1] * _dot_tn(k * pow_ref[1], v)
    avg = sq_ref[2]
    xc = acc - _dot_exact_rhs(acc, avg)
    var = _dot_exact_rhs(xc * xc, avg)
    o_ref[0] = ((xc * lax.rsqrt(var + EPS)) * _silu(g_ref[0])).astype(BF16)

    @pl.when(t == pl.num_programs(1) - 1)
    def _():
        rout_ref[0] = r_scr[...]


def _ret(u, rope_tab, pow_tab, dmask, sq_tab, r0, tb):
    b, t, _ = u.shape
    ublk = lambda c: pl.BlockSpec((1, tb, MIX_C), lambda i, j: (i, j, COL_B + c))
    return pl.pallas_call(
        functools.partial(_ret_kernel, tb=tb),
        grid=(b, t // tb),
        in_specs=[ublk(3), ublk(4), ublk(5), ublk(6),
                  pl.BlockSpec((3, tb, MIX_C), lambda i, j: (0, j, 0)),
                  pl.BlockSpec((2, tb, MIX_C), lambda i, j: (0, 0, 0)),
                  pl.BlockSpec((H_C, tb, tb), lambda i, j: (0, 0, 0)),
                  pl.BlockSpec((3, MIX_C, MIX_C), lambda i, j: (0, 0, 0)),
                  pl.BlockSpec((1, MIX_C, MIX_C), lambda i, j: (i, 0, 0))],
        out_specs=[pl.BlockSpec((1, tb, MIX_C), lambda i, j: (i, j, 0)),
                   pl.BlockSpec((1, MIX_C, MIX_C), lambda i, j: (i, 0, 0))],
        out_shape=[jax.ShapeDtypeStruct((b, t, MIX_C), BF16),
                   jax.ShapeDtypeStruct((b, MIX_C, MIX_C), F32)],
        scratch_shapes=[pltpu.VMEM((MIX_C, MIX_C), F32)],
        compiler_params=_params(("parallel", "arbitrary")),
        name="ret",
    )(u, u, u, u, rope_tab, pow_tab, dmask, sq_tab, r0)


def _ret_tables(t_len, pos0, tb):
    half = DK_C // 2
    inv_freq = jnp.exp(-math.log(ROPE_BASE) * jnp.arange(half, dtype=F32) / half)
    ang = (pos0 + jnp.arange(t_len)).astype(F32)[:, None] * inv_freq[None, :]
    cos, sin = jnp.cos(ang), jnp.sin(ang)
    zero = jnp.zeros_like(sin)
    per_head = lambda a, b_: jnp.tile(jnp.concatenate([a, b_], axis=1), (1, H_C))
    rope_tab = jnp.stack([per_head(cos, cos), per_head(-sin, zero), per_head(zero, sin)])
    lg = jnp.log1p(-jnp.exp2(-5.0 - jnp.arange(H_C, dtype=F32)))
    lg_lane = jnp.repeat(lg, DK_C)
    n = jnp.arange(tb, dtype=F32)
    pow_tab = jnp.stack([jnp.exp(lg_lane[None, :] * (n + 1.0)[:, None]),
                         jnp.exp(lg_lane[None, :] * (tb - 1.0 - n)[:, None])])
    diff = n[:, None] - n[None, :]
    causal = diff >= 0
    dmask = jnp.where(causal, jnp.exp(lg[:, None, None] * jnp.where(causal, diff, 0.0)), 0.0)
    head = jnp.arange(MIX_C) // DK_C
    same = (head[:, None] == head[None, :]).astype(F32)
    carry = jnp.broadcast_to(jnp.exp(lg_lane * tb)[:, None], (MIX_C, MIX_C))
    sq_tab = jnp.stack([carry, same, same / DV_C])
    return rope_tab, pow_tab, dmask, sq_tab


def _relayout_w_in(w_in):
    n_small = 2 * H_A
    a_end = QKV_A + MIX_A
    pad = jnp.zeros(w_in.shape[:2] + (LANES - n_small,), w_in.dtype)
    return jnp.concatenate([w_in[..., :a_end], w_in[..., a_end + n_small:],
                            w_in[..., a_end:a_end + n_small], pad], axis=-1).astype(BF16)


def _block_diag(r):
    b = r.shape[0]
    eye = jnp.eye(H_C, dtype=r.dtype)
    return jnp.einsum('bhkv,hg->bhkgv', r, eye).reshape(b, H_C * DK_C, H_C * DV_C)


def _diag_blocks(r):
    b = r.shape[0]
    r = r.reshape(b, H_C, DK_C, H_C, DV_C)
    return jnp.stack([r[:, h, :, h, :] for h in range(H_C)], axis=1)


def _run(x, pos0, caches, weights):
    (norm_ffn1, wg1, wu1, wd1, norm_mix, w_in, conv_w, a_log, dt_bias, gdn_norm, rel_bias, w_out,
     norm_ffn2, wg2, wu2, wd2, norm_final) = weights
    b, t, _ = x.shape
    n = b * t
    tb = min(MIX_TB, t)
    assert t % tb == 0 and tb % CHUNK == 0, (t, tb)
    rope_tab, pow_tab, dmask, sq_tab = _ret_tables(t, pos0, tb)
    gf = norm_final.reshape(1, D_MODEL)
    norm_ffn1, norm_mix, norm_ffn2 = (g.reshape(DEPTH, 1, D_MODEL) for g in (norm_ffn1, norm_mix, norm_ffn2))
    xf = x.reshape(n, D_MODEL)
    new = []
    for l in range(DEPTH):
        if caches is None:
            s_gdn0 = jnp.zeros((b, H_A, DK_A, DV_A), F32)
            cb0 = jnp.zeros((b, SUBLANES, QKV_A), F32)
            r0 = jnp.zeros((b, MIX_C, MIX_C), F32)
            ck = cv = jnp.zeros((1, BAND, MIX_B), F32)
            n_invalid = BAND
        else:
            state_gdn, state_conv, cache_k, cache_v, state_ret = caches
            s_gdn0 = state_gdn[l]
            cb0 = jnp.pad(state_conv[l], ((0, 0), (SUBLANES - (CONV_W - 1), 0), (0, 0)))
            r0 = _block_diag(state_ret[l])
            ck = cache_k[l].reshape(b, BAND, MIX_B)
            cv = cache_v[l].reshape(b, BAND, MIX_B)
            n_invalid = 0
        xf = _ffn(xf, norm_ffn1, wg1, wu1, wd1, l)
        u = _inproj(xf, norm_mix, w_in, l).reshape(b, t, U_COLS)
        lanes = jnp.arange(LANES)
        in_decay = (lanes >= H_A) & (lanes < 2 * H_A)
        hidx = jnp.clip(lanes - H_A, 0, H_A - 1)
        prm = jnp.zeros((SUBLANES, LANES), F32)
        prm = prm.at[0].set(jnp.where(in_decay, -jnp.exp(a_log[l].astype(F32))[hidx], 0.0))
        prm = prm.at[1].set(jnp.where(in_decay, dt_bias[l].astype(F32)[hidx], 0.0))
        prm = prm.at[2].set(gdn_norm[l].astype(F32))
        oa, s_gdn = _gdn(u, conv_w[l], prm, s_gdn0, cb0, tb)
        ob = _band(u, _band_bias_table(rel_bias[l], tb), ck, cv, tb, n_invalid)
        oc, r_new = _ret(u, rope_tab, pow_tab, dmask, sq_tab, r0, tb)
        xf = _mix_ffn(xf, oa.reshape(n, MIX_A), ob.reshape(n, MIX_B), oc.reshape(n, MIX_C), w_out,
                      norm_ffn2, wg2, wu2, wd2, gf, l, l == DEPTH - 1)
        keep = min(BAND, t)
        conv_new = u[:, t - (CONV_W - 1):, :QKV_A]
        kb_new = u[:, t - keep:, (COL_B + 1) * MIX_B:(COL_B + 2) * MIX_B].reshape(b, keep, H_B, DH_B)
        vb_new = u[:, t - keep:, (COL_B + 2) * MIX_B:(COL_B + 3) * MIX_B].reshape(b, keep, H_B, DH_B)
        new.append((s_gdn, conv_new, kb_new, vb_new, _diag_blocks(r_new)))
    g_, c_, k_, v_, r_ = zip(*new)
    return (xf.reshape(b, t, D_MODEL), jnp.stack(g_), jnp.stack(c_), jnp.stack(k_), jnp.stack(v_),
            jnp.stack(r_))


def kernel(x_prompt, x_sample, state_gdn, state_conv, cache_band_k, cache_band_v, state_ret, norm_ffn1, w_ffn1_gate, w_ffn1_up, w_ffn1_down, norm_mix, w_in, conv_w, a_log, dt_bias, gdn_norm, rel_bias, w_out, norm_ffn2, w_ffn2_gate, w_ffn2_up, w_ffn2_down, norm_final):
    past_len = 1024
    weights = (norm_ffn1, w_ffn1_gate.astype(BF16), w_ffn1_up.astype(BF16), w_ffn1_down.astype(BF16),
               norm_mix, _relayout_w_in(w_in), conv_w, a_log, dt_bias, gdn_norm, rel_bias,
               w_out.astype(BF16), norm_ffn2, w_ffn2_gate.astype(BF16), w_ffn2_up.astype(BF16),
               w_ffn2_down.astype(BF16), norm_final)
    p = _run(x_prompt, 0, None, weights)
    s = _run(x_sample, past_len, (state_gdn, state_conv, cache_band_k, cache_band_v, state_ret), weights)
    return (p[0], s[0]) + p[1:] + s[1:]
```

```python
import functools
import math

import jax
import jax.numpy as jnp
from jax import lax
from jax.experimental import pallas as pl
from jax.experimental.pallas import tpu as pltpu

F32 = jnp.float32
BF16 = jnp.bfloat16

D_MODEL = 1024
DEPTH = 2
CHUNK = 64
H_A, DK_A, DV_A, CONV_W = 4, 128, 128, 4
H_B, DH_B, BAND_CHUNKS, REL_CLIP = 4, 64, 8, 128
H_C, DK_C, DV_C = 4, 64, 64
ROPE_BASE = 10000.0
D_FF = 2816
EPS = 1e-6
NEG_INF = -1e30
BAND = BAND_CHUNKS * CHUNK
QKV_A = 3 * H_A * DK_A
MIX_A = H_A * DV_A
MIX_B = H_B * DH_B
MIX_C = H_C * DV_C
LANES = 128
SUBLANES = 8
U_COLS = QKV_A + MIX_A + 3 * MIX_B + 4 * MIX_C + LANES
COL_Z = QKV_A // MIX_A
COL_B = (QKV_A + MIX_A) // MIX_B
COL_BA = (U_COLS - LANES) // LANES
VMEM_LIMIT = 56 * 1024 * 1024

FFN_TM = 1024
FFN_TF = 256
PROJ_TM = 256
MIX_TB = 256


def _params(sem):
    return pltpu.CompilerParams(dimension_semantics=sem, vmem_limit_bytes=VMEM_LIMIT)


def _rms(x, g):
    return (x * lax.rsqrt(jnp.mean(x * x, axis=-1, keepdims=True) + EPS)) * g


def _silu(x):
    return x * jax.nn.sigmoid(x)


def _dot(a, b):
    return jnp.dot(a.astype(BF16), b.astype(BF16), preferred_element_type=F32)


def _dot_nt(a, b):
    return lax.dot_general(a.astype(BF16), b.astype(BF16), (((1,), (1,)), ((), ())),
                           preferred_element_type=F32)


def _dot_tn(a, b):
    return lax.dot_general(a.astype(BF16), b.astype(BF16), (((0,), (0,)), ((), ())),
                           preferred_element_type=F32)


def _dot_exact_rhs(a, b):
    hi = a.astype(BF16)
    r1 = a - hi.astype(F32)
    mid = r1.astype(BF16)
    lo = (r1 - mid.astype(F32)).astype(BF16)
    bb = b.astype(BF16)
    out = jnp.dot(hi, bb, preferred_element_type=F32)
    out = out + jnp.dot(mid, bb, preferred_element_type=F32)
    return out + jnp.dot(lo, bb, preferred_element_type=F32)


def _ffn_steps(h_scr, acc_scr, wg_ref, wu_ref, wd_ref):
    h = h_scr[...]
    gate = jnp.dot(h, wg_ref[...], preferred_element_type=F32)
    up = jnp.dot(h, wu_ref[...], preferred_element_type=F32)
    act = (_silu(gate) * up).astype(BF16)
    acc_scr[...] += jnp.dot(act, wd_ref[...], preferred_element_type=F32)


def _ffn_kernel(x_ref, g_ref, wg_ref, wu_ref, wd_ref, o_ref, h_scr, acc_scr):
    j = pl.program_id(1)

    @pl.when(j == 0)
    def _():
        h_scr[...] = _rms(x_ref[...], g_ref[...]).astype(BF16)
        acc_scr[...] = jnp.zeros_like(acc_scr)

    _ffn_steps(h_scr, acc_scr, wg_ref, wu_ref, wd_ref)

    @pl.when(j == pl.num_programs(1) - 1)
    def _():
        o_ref[...] = x_ref[...] + 0.5 * acc_scr[...]


def _mix_ffn_kernel(x_ref, oa_ref, ob_ref, oc_ref, woa_ref, wob_ref, woc_ref, g_ref, wg_ref, wu_ref,
                    wd_ref, gf_ref, o_ref, x_scr, h_scr, acc_scr, *, final_norm):
    j = pl.program_id(1)

    @pl.when(j == 0)
    def _():
        mix = jnp.dot(oa_ref[...], woa_ref[...], preferred_element_type=F32)
        mix = mix + jnp.dot(ob_ref[...], wob_ref[...], preferred_element_type=F32)
        mix = mix + jnp.dot(oc_ref[...], woc_ref[...], preferred_element_type=F32)
        x2 = x_ref[...] + mix
        x_scr[...] = x2
        h_scr[...] = _rms(x2, g_ref[...]).astype(BF16)
        acc_scr[...] = jnp.zeros_like(acc_scr)

    _ffn_steps(h_scr, acc_scr, wg_ref, wu_ref, wd_ref)

    @pl.when(j == pl.num_programs(1) - 1)
    def _():
        y = x_scr[...] + 0.5 * acc_scr[...]
        if final_norm:
            y = _rms(y, gf_ref[...])
        o_ref[...] = y


def _ffn_weight_specs(layer, tf):
    return [
        pl.BlockSpec((None, D_MODEL, tf), lambda i, j: (layer, 0, j)),
        pl.BlockSpec((None, D_MODEL, tf), lambda i, j: (layer, 0, j)),
        pl.BlockSpec((None, tf, D_MODEL), lambda i, j: (layer, j, 0)),
    ]


def _ffn(x, g, wg, wu, wd, layer):
    n = x.shape[0]
    tm = min(FFN_TM, n)
    assert n % tm == 0, (n, tm)
    tok = pl.BlockSpec((tm, D_MODEL), lambda i, j: (i, 0))
    return pl.pallas_call(
        _ffn_kernel,
        grid=(n // tm, D_FF // FFN_TF),
        in_specs=[tok, pl.BlockSpec((None, 1, D_MODEL), lambda i, j: (layer, 0, 0))]
        + _ffn_weight_specs(layer, FFN_TF),
        out_specs=tok,
        out_shape=jax.ShapeDtypeStruct((n, D_MODEL), F32),
        scratch_shapes=[pltpu.VMEM((tm, D_MODEL), BF16), pltpu.VMEM((tm, D_MODEL), F32)],
        compiler_params=_params(("parallel", "arbitrary")),
        name="ffn",
    )(x, g, wg, wu, wd)


def _mix_ffn(x, oa, ob, oc, w_out, g, wg, wu, wd, g_final, layer, final_norm):
    n = x.shape[0]
    tm = min(FFN_TM, n)
    assert n % tm == 0, (n, tm)
    tok = pl.BlockSpec((tm, D_MODEL), lambda i, j: (i, 0))
    in_specs = [
        tok,
        pl.BlockSpec((tm, MIX_A), lambda i, j: (i, 0)),
        pl.BlockSpec((tm, MIX_B), lambda i, j: (i, 0)),
        pl.BlockSpec((tm, MIX_C), lambda i, j: (i, 0)),
        pl.BlockSpec((None, MIX_A, D_MODEL), lambda i, j: (layer, 0, 0)),
        pl.BlockSpec((None, MIX_B, D_MODEL), lambda i, j: (layer, MIX_A // MIX_B, 0)),
        pl.BlockSpec((None, MIX_C, D_MODEL), lambda i, j: (layer, (MIX_A + MIX_B) // MIX_C, 0)),
        pl.BlockSpec((None, 1, D_MODEL), lambda i, j: (layer, 0, 0)),
    ] + _ffn_weight_specs(layer, FFN_TF) + [pl.BlockSpec((1, D_MODEL), lambda i, j: (0, 0))]
    return pl.pallas_call(
        functools.partial(_mix_ffn_kernel, final_norm=final_norm),
        grid=(n // tm, D_FF // FFN_TF),
        in_specs=in_specs,
        out_specs=tok,
        out_shape=jax.ShapeDtypeStruct((n, D_MODEL), F32),
        scratch_shapes=[pltpu.VMEM((tm, D_MODEL), F32), pltpu.VMEM((tm, D_MODEL), BF16),
                        pltpu.VMEM((tm, D_MODEL), F32)],
        compiler_params=_params(("parallel", "arbitrary")),
        name="mix_ffn",
    )(x, oa, ob, oc, w_out, w_out, w_out, g, wg, wu, wd, g_final)


def _inproj_kernel(x_ref, g_ref, w_ref, o_ref):
    h = _rms(x_ref[...], g_ref[...]).astype(BF16)
    o_ref[...] = jnp.dot(h, w_ref[...], preferred_element_type=F32)


def _inproj(x, g, w, layer):
    n = x.shape[0]
    tm = min(PROJ_TM, n)
    assert n % tm == 0, (n, tm)
    return pl.pallas_call(
        _inproj_kernel,
        grid=(n // tm,),
        in_specs=[pl.BlockSpec((tm, D_MODEL), lambda i: (i, 0)),
                  pl.BlockSpec((None, 1, D_MODEL), lambda i: (layer, 0, 0)),
                  pl.BlockSpec((None, D_MODEL, U_COLS), lambda i: (layer, 0, 0))],
        out_specs=pl.BlockSpec((tm, U_COLS), lambda i: (i, 0)),
        out_shape=jax.ShapeDtypeStruct((n, U_COLS), F32),
        compiler_params=_params(("parallel",)),
        name="inproj",
    )(x, g, w)


def _gdn_kernel(u_ref, z_ref, ba_ref, convw_ref, prm_ref, s0_ref, cb0_ref, oa_ref, sout_ref,
                xbuf, dbuf, bbuf, val_scr, qk_scr, kdec_scr, attn_scr, s_scr, *, tb):
    t = pl.program_id(1)
    nchunks = tb // CHUNK

    @pl.when(t == 0)
    def _():
        xbuf[0:SUBLANES, :] = cb0_ref[0]
        s_scr[...] = s0_ref[0]

    xbuf[SUBLANES:SUBLANES + tb, :] = u_ref[0]

    ba = ba_ref[0]
    x = ba + prm_ref[1:2, :]
    softplus = jnp.maximum(x, 0.0) + jnp.log1p(jnp.exp(-jnp.abs(x)))
    g = prm_ref[0:1, :] * softplus
    bbuf[...] = jax.nn.sigmoid(ba)
    row = lax.broadcasted_iota(jnp.int32, (tb, tb), 0)
    col = lax.broadcasted_iota(jnp.int32, (tb, tb), 1)
    shift = CHUNK.bit_length() - 1
    same_chunk = jnp.right_shift(row, shift) == jnp.right_shift(col, shift)
    tri_bd = jnp.where(same_chunk & (col <= row), 1.0, 0.0)
    dbuf[...] = _cumsum_rows(g, tri_bd)

    r_i = lax.broadcasted_iota(jnp.int32, (CHUNK, CHUNK), 0)
    c_i = lax.broadcasted_iota(jnp.int32, (CHUNK, CHUNK), 1)
    tril = c_i <= r_i
    strict = c_i < r_i
    eye = c_i == r_i
    eye_f = jnp.where(eye, 1.0, 0.0)
    gnorm = prm_ref[2:3, :]

    def conv(c, col0):
        r0 = c * CHUNK + SUBLANES - (CONV_W - 1)
        cols = slice(col0, col0 + LANES)
        y = xbuf[r0:r0 + CHUNK, cols] * convw_ref[0:1, cols]
        for j in range(1, CONV_W):
            y = y + xbuf[r0 + j:r0 + j + CHUNK, cols] * convw_ref[j:j + 1, cols]
        return _silu(y)

    problems = [(c, h) for c in range(nchunks) for h in range(H_A)]
    lmask, qk, a_low = {}, {}, {}
    for c in range(nchunks):
        rows = slice(c * CHUNK, (c + 1) * CHUNK)
        dch = dbuf[rows, :]
        beta = bbuf[rows, :]
        e_d = jnp.exp(dch)
        e_dec = jnp.exp(dch[CHUNK - 1:CHUNK, :] - dch)
        for h in range(H_A):
            p = c * H_A + h
            hcols = slice(h * DV_A, (h + 1) * DV_A)
            q = conv(c, h * DK_A)
            k = conv(c, MIX_A + h * DK_A)
            v = conv(c, 2 * MIX_A + h * DV_A)
            q = (q * lax.rsqrt(jnp.sum(q * q, axis=-1, keepdims=True) + EPS)) * DK_A ** -0.5
            k = k * lax.rsqrt(jnp.sum(k * k, axis=-1, keepdims=True) + EPS)
            dcol = dch[:, 4 + h:5 + h]
            bcol = beta[:, h:h + 1]
            edcol = e_d[:, 4 + h:5 + h]
            drow = jnp.sum(jnp.where(eye, dcol, 0.0), axis=0, keepdims=True)
            diff = dcol - drow
            lmask[c, h] = jnp.where(tril, jnp.exp(jnp.where(tril, diff, 0.0)), 0.0)
            kb = k * bcol
            qk[c, h] = _dot_nt(jnp.concatenate([q, kb], axis=0), k)
            val_scr[rows, hcols] = v * bcol
            qk_scr[p, 0:CHUNK, :] = (q * edcol).astype(BF16)
            qk_scr[p, CHUNK:2 * CHUNK, :] = (kb * edcol).astype(BF16)
            kdec_scr[rows, hcols] = (k * e_dec[:, 4 + h:5 + h]).astype(BF16)
    for c, h in problems:
        attn_scr[c * H_A + h] = (qk[c, h][0:CHUNK] * lmask[c, h]).astype(BF16)
        a_low[c, h] = jnp.where(strict, qk[c, h][CHUNK:2 * CHUNK] * lmask[c, h], 0.0)
    pw = a_low
    inv = {ch: eye_f - a_low[ch] for ch in problems}
    for _ in range(5):
        pw = {ch: _dot(pw[ch], pw[ch]) for ch in problems}
        inv = {ch: inv[ch] + _dot(inv[ch], pw[ch]) for ch in problems}
    for c, h in problems:
        p = c * H_A + h
        rows = slice(c * CHUNK, (c + 1) * CHUNK)
        hcols = slice(h * DV_A, (h + 1) * DV_A)
        rhs = jnp.concatenate([val_scr[rows, hcols].astype(BF16), qk_scr[p, CHUNK:2 * CHUNK, :]], axis=1)
        sol = _dot(inv[c, h], rhs)
        val_scr[rows, hcols] = sol[:, :DV_A]
        qk_scr[p, CHUNK:2 * CHUNK, :] = sol[:, DV_A:].astype(BF16)

    state = [s_scr[h] for h in range(H_A)]
    for c in range(nchunks):
        rows = slice(c * CHUNK, (c + 1) * CHUNK)
        e_last = jnp.exp(dbuf[(c + 1) * CHUNK - 1:(c + 1) * CHUNK, :])
        qs = [_dot(qk_scr[c * H_A + h], state[h]) for h in range(H_A)]
        v_new = [val_scr[rows, h * DV_A:(h + 1) * DV_A] - qs[h][CHUNK:2 * CHUNK] for h in range(H_A)]
        o_in = [_dot(attn_scr[c * H_A + h], v_new[h]) for h in range(H_A)]
        ds = [_dot_tn(kdec_scr[rows, h * DK_A:(h + 1) * DK_A], v_new[h]) for h in range(H_A)]
        for h in range(H_A):
            hcols = slice(h * DV_A, (h + 1) * DV_A)
            o = qs[h][0:CHUNK] + o_in[h]
            state[h] = state[h] * e_last[:, 4 + h:5 + h] + ds[h]
            oa_ref[0, rows, hcols] = (_rms(o, gnorm) * _silu(z_ref[0, rows, hcols])).astype(BF16)
    for h in range(H_A):
        s_scr[h] = state[h]
    xbuf[0:SUBLANES, :] = xbuf[tb:tb + SUBLANES, :]

    @pl.when(t == pl.num_programs(1) - 1)
    def _():
        sout_ref[0] = s_scr[...]


def _cumsum_rows(g, tri_bd):
    hi = g.astype(BF16)
    r1 = g - hi.astype(F32)
    mid = r1.astype(BF16)
    lo = (r1 - mid.astype(F32)).astype(BF16)
    tb = tri_bd.astype(BF16)
    out = jnp.dot(tb, hi, preferred_element_type=F32)
    out = out + jnp.dot(tb, mid, preferred_element_type=F32)
    return out + jnp.dot(tb, lo, preferred_element_type=F32)


def _gdn(u, conv_w, prm, s0, cb0, tb):
    b, t, _ = u.shape
    return pl.pallas_call(
        functools.partial(_gdn_kernel, tb=tb),
        grid=(b, t // tb),
        in_specs=[pl.BlockSpec((1, tb, QKV_A), lambda i, j: (i, j, 0)),
                  pl.BlockSpec((1, tb, MIX_A), lambda i, j: (i, j, COL_Z)),
                  pl.BlockSpec((1, tb, LANES), lambda i, j: (i, j, COL_BA)),
                  pl.BlockSpec((CONV_W, QKV_A), lambda i, j: (0, 0)),
                  pl.BlockSpec((SUBLANES, LANES), lambda i, j: (0, 0)),
                  pl.BlockSpec((1, H_A, DK_A, DV_A), lambda i, j: (i, 0, 0, 0)),
                  pl.BlockSpec((1, SUBLANES, QKV_A), lambda i, j: (i, 0, 0))],
        out_specs=[pl.BlockSpec((1, tb, MIX_A), lambda i, j: (i, j, 0)),
                   pl.BlockSpec((1, H_A, DK_A, DV_A), lambda i, j: (i, 0, 0, 0))],
        out_shape=[jax.ShapeDtypeStruct((b, t, MIX_A), BF16),
                   jax.ShapeDtypeStruct((b, H_A, DK_A, DV_A), F32)],
        scratch_shapes=[pltpu.VMEM((tb + SUBLANES, QKV_A), F32),
                        pltpu.VMEM((tb, LANES), F32), pltpu.VMEM((tb, LANES), F32),
                        pltpu.VMEM((tb, MIX_A), F32),
                        pltpu.VMEM((tb // CHUNK * H_A, 2 * CHUNK, DK_A), BF16),
                        pltpu.VMEM((tb, H_A * DK_A), BF16),
                        pltpu.VMEM((tb // CHUNK * H_A, CHUNK, CHUNK), BF16),
                        pltpu.VMEM((H_A, DK_A, DV_A), F32)],
        compiler_params=_params(("parallel", "arbitrary")),
        name="gdn",
    )(u, u, u, conv_w, prm, s0, cb0)


def _band_kernel(q_ref, k_ref, v_ref, bias_ref, ck_ref, cv_ref, o_ref, kbuf, vbuf, *, qb, t_len, n_invalid):
    t = pl.program_id(1)

    @pl.when(t == 0)
    def _():
        kbuf[0:BAND, :] = ck_ref[0].astype(BF16)
        vbuf[0:BAND, :] = cv_ref[0].astype(BF16)
        kbuf[BAND:BAND + t_len, :] = k_ref[0].astype(BF16)
        vbuf[BAND:BAND + t_len, :] = v_ref[0].astype(BF16)

    start = pl.multiple_of(t * qb, qb)
    kw = kbuf[pl.ds(start, qb + BAND), :]
    vw = vbuf[pl.ds(start, qb + BAND), :]
    q = q_ref[0] * DH_B ** -0.5
    lane = lax.broadcasted_iota(jnp.int32, (1, MIX_B), 1)
    if n_invalid:
        kpos = lax.broadcasted_iota(jnp.int32, (qb, qb + BAND), 1) + start
        valid = kpos >= n_invalid
    acc = jnp.zeros((qb, MIX_B), F32)
    for h in range(H_B):
        mh = (lane >= h * DH_B) & (lane < (h + 1) * DH_B)
        s = _dot_nt(jnp.where(mh, q, 0.0), kw) + bias_ref[h]
        if n_invalid:
            s = jnp.where(valid, s, NEG_INF)
        p = jnp.exp(s - jnp.max(s, axis=-1, keepdims=True))
        o = _dot(p, vw) / jnp.sum(p, axis=-1, keepdims=True)
        acc = acc + jnp.where(mh, o, 0.0)
    o_ref[0] = acc.astype(BF16)


def _band(u, bias, ck, cv, qb, n_invalid):
    b, t, _ = u.shape
    cache_map = (lambda i, j: (i, 0, 0)) if ck.shape[0] == b else (lambda i, j: (0, 0, 0))
    return pl.pallas_call(
        functools.partial(_band_kernel, qb=qb, t_len=t, n_invalid=n_invalid),
        grid=(b, t // qb),
        in_specs=[pl.BlockSpec((1, qb, MIX_B), lambda i, j: (i, j, COL_B)),
                  pl.BlockSpec((1, t, MIX_B), lambda i, j: (i, 0, COL_B + 1)),
                  pl.BlockSpec((1, t, MIX_B), lambda i, j: (i, 0, COL_B + 2)),
                  pl.BlockSpec((H_B, qb, qb + BAND), lambda i, j: (0, 0, 0)),
                  pl.BlockSpec((1, BAND, MIX_B), cache_map),
                  pl.BlockSpec((1, BAND, MIX_B), cache_map)],
        out_specs=pl.BlockSpec((1, qb, MIX_B), lambda i, j: (i, j, 0)),
        out_shape=jax.ShapeDtypeStruct((b, t, MIX_B), BF16),
        scratch_shapes=[pltpu.VMEM((BAND + t, MIX_B), BF16), pltpu.VMEM((BAND + t, MIX_B), BF16)],
        compiler_params=_params(("parallel", "arbitrary")),
        name="band",
    )(u, u, u, bias, ck, cv)


def _band_bias_table(rel_bias, qb):
    rb = rel_bias.astype(F32)
    nh = rb.shape[0]
    w = qb + BAND
    p = w + qb - 1
    rel_lo, rel_hi = -(qb - 1) - BAND, qb - 1
    mid_hi = min(rel_hi, REL_CLIP)
    f = jnp.concatenate([jnp.broadcast_to(rb[:, :1], (nh, -REL_CLIP - rel_lo)),
                         rb[:, :mid_hi + REL_CLIP + 1],
                         jnp.broadcast_to(rb[:, -1:], (nh, rel_hi - mid_hi))], axis=1)
    rolled = jnp.tile(jnp.pad(f, ((0, 0), (0, 1))), (1, qb))[:, :qb * p].reshape(nh, qb, p)
    tab = rolled[:, :, qb - 1:qb - 1 + w]
    ci = jnp.arange(qb)[:, None] // CHUNK
    cj = jnp.arange(w)[None, :] // CHUNK
    return jnp.where((cj >= ci) & (cj <= ci + BAND_CHUNKS), tab, NEG_INF)


def _ret_kernel(q_ref, k_ref, v_ref, g_ref, rope_ref, pow_ref, dmask_ref, sq_ref, r0_ref, o_ref, rout_ref,
                r_scr, *, tb):
    t = pl.program_id(1)

    @pl.when(t == 0)
    def _():
        r_scr[...] = r0_ref[0]

    def rope(x):
        return (x * rope_ref[0] + pltpu.roll(x, MIX_C - DK_C // 2, 1) * rope_ref[1]
                + pltpu.roll(x, DK_C // 2, 1) * rope_ref[2])

    q = rope(q_ref[0])
    k = rope(k_ref[0]) * DK_C ** -0.5
    v = v_ref[0]
    r = r_scr[...]
    acc = _dot(q * pow_ref[0], r)
    lane = lax.broadcasted_iota(jnp.int32, (1, MIX_C), 1)
    for h in range(H_C):
        mh = (lane >= h * DK_C) & (lane < (h + 1) * DK_C)
        s = _dot_nt(jnp.where(mh, q, 0.0), k) * dmask_ref[h]
        acc = acc + jnp.where(mh, _dot(s, v), 0.0)
    r_scr[...] = r * sq_ref[0] + sq_ref[1] * _dot_tn(k * pow_ref[1], v)
    avg = sq_ref[2]
    xc = acc - _dot_exact_rhs(acc, avg)
    var = _dot_exact_rhs(xc * xc, avg)
    o_ref[0] = ((xc * lax.rsqrt(var + EPS)) * _silu(g_ref[0])).astype(BF16)

    @pl.when(t == pl.num_programs(1) - 1)
    def _():
        rout_ref[0] = r_scr[...]


def _ret(u, rope_tab, pow_tab, dmask, sq_tab, r0, tb):
    b, t, _ = u.shape
    ublk = lambda c: pl.BlockSpec((1, tb, MIX_C), lambda i, j: (i, j, COL_B + c))
    return pl.pallas_call(
        functools.partial(_ret_kernel, tb=tb),
        grid=(b, t // tb),
        in_specs=[ublk(3), ublk(4), ublk(5), ublk(6),
                  pl.BlockSpec((3, tb, MIX_C), lambda i, j: (0, j, 0)),
                  pl.BlockSpec((2, tb, MIX_C), lambda i, j: (0, 0, 0)),
                  pl.BlockSpec((H_C, tb, tb), lambda i, j: (0, 0, 0)),
                  pl.BlockSpec((3, MIX_C, MIX_C), lambda i, j: (0, 0, 0)),
                  pl.BlockSpec((1, MIX_C, MIX_C), lambda i, j: (i, 0, 0))],
        out_specs=[pl.BlockSpec((1, tb, MIX_C), lambda i, j: (i, j, 0)),
                   pl.BlockSpec((1, MIX_C, MIX_C), lambda i, j: (i, 0, 0))],
        out_shape=[jax.ShapeDtypeStruct((b, t, MIX_C), BF16),
                   jax.ShapeDtypeStruct((b, MIX_C, MIX_C), F32)],
        scratch_shapes=[pltpu.VMEM((MIX_C, MIX_C), F32)],
        compiler_params=_params(("parallel", "arbitrary")),
        name="ret",
    )(u, u, u, u, rope_tab, pow_tab, dmask, sq_tab, r0)


def _ret_tables(t_len, pos0, tb):
    half = DK_C // 2
    inv_freq = jnp.exp(-math.log(ROPE_BASE) * jnp.arange(half, dtype=F32) / half)
    ang = (pos0 + jnp.arange(t_len)).astype(F32)[:, None] * inv_freq[None, :]
    cos, sin = jnp.cos(ang), jnp.sin(ang)
    zero = jnp.zeros_like(sin)
    per_head = lambda a, b_: jnp.tile(jnp.concatenate([a, b_], axis=1), (1, H_C))
    rope_tab = jnp.stack([per_head(cos, cos), per_head(-sin, zero), per_head(zero, sin)])
    lg = jnp.log1p(-jnp.exp2(-5.0 - jnp.arange(H_C, dtype=F32)))
    lg_lane = jnp.repeat(lg, DK_C)
    n = jnp.arange(tb, dtype=F32)
    pow_tab = jnp.stack([jnp.exp(lg_lane[None, :] * (n + 1.0)[:, None]),
                         jnp.exp(lg_lane[None, :] * (tb - 1.0 - n)[:, None])])
    diff = n[:, None] - n[None, :]
    causal = diff >= 0
    dmask = jnp.where(causal, jnp.exp(lg[:, None, None] * jnp.where(causal, diff, 0.0)), 0.0)
    head = jnp.arange(MIX_C) // DK_C
    same = (head[:, None] == head[None, :]).astype(F32)
    carry = jnp.broadcast_to(jnp.exp(lg_lane * tb)[:, None], (MIX_C, MIX_C))
    sq_tab = jnp.stack([carry, same, same / DV_C])
    return rope_tab, pow_tab, dmask, sq_tab


def _relayout_w_in(w_in):
    n_small = 2 * H_A
    a_end = QKV_A + MIX_A
    pad = jnp.zeros(w_in.shape[:2] + (LANES - n_small,), w_in.dtype)
    return jnp.concatenate([w_in[..., :a_end], w_in[..., a_end + n_small:],
                            w_in[..., a_end:a_end + n_small], pad], axis=-1).astype(BF16)


def _block_diag(r):
    b = r.shape[0]
    eye = jnp.eye(H_C, dtype=r.dtype)
    return jnp.einsum('bhkv,hg->bhkgv', r, eye).reshape(b, H_C * DK_C, H_C * DV_C)


def _diag_blocks(r):
    b = r.shape[0]
    r = r.reshape(b, H_C, DK_C, H_C, DV_C)
    return jnp.stack([r[:, h, :, h, :] for h in range(H_C)], axis=1)


def _run(x, pos0, caches, weights):
    (norm_ffn1, wg1, wu1, wd1, norm_mix, w_in, conv_w, a_log, dt_bias, gdn_norm, rel_bias, w_out,
     norm_ffn2, wg2, wu2, wd2, norm_final) = weights
    b, t, _ = x.shape
    n = b * t
    tb = min(MIX_TB, t)
    assert t % tb == 0 and tb % CHUNK == 0, (t, tb)
    rope_tab, pow_tab, dmask, sq_tab = _ret_tables(t, pos0, tb)
    gf = norm_final.reshape(1, D_MODEL)
    norm_ffn1, norm_mix, norm_ffn2 = (g.reshape(DEPTH, 1, D_MODEL) for g in (norm_ffn1, norm_mix, norm_ffn2))
    xf = x.reshape(n, D_MODEL)
    new = []
    for l in range(DEPTH):
        if caches is None:
            s_gdn0 = jnp.zeros((b, H_A, DK_A, DV_A), F32)
            cb0 = jnp.zeros((b, SUBLANES, QKV_A), F32)
            r0 = jnp.zeros((b, MIX_C, MIX_C), F32)
            ck = cv = jnp.zeros((1, BAND, MIX_B), F32)
            n_invalid = BAND
        else:
            state_gdn, state_conv, cache_k, cache_v, state_ret = caches
            s_gdn0 = state_gdn[l]
            cb0 = jnp.pad(state_conv[l], ((0, 0), (SUBLANES - (CONV_W - 1), 0), (0, 0)))
            r0 = _block_diag(state_ret[l])
            ck = cache_k[l].reshape(b, BAND, MIX_B)
            cv = cache_v[l].reshape(b, BAND, MIX_B)
            n_invalid = 0
        xf = _ffn(xf, norm_ffn1, wg1, wu1, wd1, l)
        u = _inproj(xf, norm_mix, w_in, l).reshape(b, t, U_COLS)
        lanes = jnp.arange(LANES)
        in_decay = (lanes >= H_A) & (lanes < 2 * H_A)
        hidx = jnp.clip(lanes - H_A, 0, H_A - 1)
        prm = jnp.zeros((SUBLANES, LANES), F32)
        prm = prm.at[0].set(jnp.where(in_decay, -jnp.exp(a_log[l].astype(F32))[hidx], 0.0))
        prm = prm.at[1].set(jnp.where(in_decay, dt_bias[l].astype(F32)[hidx], 0.0))
        prm = prm.at[2].set(gdn_norm[l].astype(F32))
        oa, s_gdn = _gdn(u, conv_w[l], prm, s_gdn0, cb0, tb)
        ob = _band(u, _band_bias_table(rel_bias[l], tb), ck, cv, tb, n_invalid)
        oc, r_new = _ret(u, rope_tab, pow_tab, dmask, sq_tab, r0, tb)
        xf = _mix_ffn(xf, oa.reshape(n, MIX_A), ob.reshape(n, MIX_B), oc.reshape(n, MIX_C), w_out,
                      norm_ffn2, wg2, wu2, wd2, gf, l, l == DEPTH - 1)
        keep = min(BAND, t)
        conv_new = u[:, t - (CONV_W - 1):, :QKV_A]
        kb_new = u[:, t - keep:, (COL_B + 1) * MIX_B:(COL_B + 2) * MIX_B].reshape(b, keep, H_B, DH_B)
        vb_new = u[:, t - keep:, (COL_B + 2) * MIX_B:(COL_B + 3) * MIX_B].reshape(b, keep, H_B, DH_B)
        new.append((s_gdn, conv_new, kb_new, vb_new, _diag_blocks(r_new)))
    g_, c_, k_, v_, r_ = zip(*new)
    return (xf.reshape(b, t, D_MODEL), jnp.stack(g_), jnp.stack(c_), jnp.stack(k_), jnp.stack(v_),
            jnp.stack(r_))


def kernel(x_prompt, x_sample, state_gdn, state_conv, cache_band_k, cache_band_v, state_ret, norm_ffn1, w_ffn1_gate, w_ffn1_up, w_ffn1_down, norm_mix, w_in, conv_w, a_log, dt_bias, gdn_norm, rel_bias, w_out, norm_ffn2, w_ffn2_gate, w_ffn2_up, w_ffn2_down, norm_final):
    past_len = 1024
    weights = (norm_ffn1, w_ffn1_gate.astype(BF16), w_ffn1_up.astype(BF16), w_ffn1_down.astype(BF16),
               norm_mix, _relayout_w_in(w_in), conv_w, a_log, dt_bias, gdn_norm, rel_bias,
               w_out.astype(BF16), norm_ffn2, w_ffn2_gate.astype(BF16), w_ffn2_up.astype(BF16),
               w_ffn2_down.astype(BF16), norm_final)
    p = _run(x_prompt, 0, None, weights)
    s = _run(x_sample, past_len, (state_gdn, state_conv, cache_band_k, cache_band_v, state_ret), weights)
    return (p[0], s[0]) + p[1:] + s[1:]
```

```python
import functools
import math

import jax
import jax.numpy as jnp
import numpy as np
from jax import lax
from jax.experimental import pallas as pl
from jax.experimental.pallas import tpu as pltpu

F32 = jnp.float32
BF16 = jnp.bfloat16

D_MODEL = 1024
DEPTH = 2
CHUNK = 64
H_A, DK_A, DV_A, CONV_W = 4, 128, 128, 4
H_B, DH_B, BAND_CHUNKS, REL_CLIP = 4, 64, 8, 128
H_C, DK_C, DV_C = 4, 64, 64
ROPE_BASE = 10000.0
D_FF = 2816
EPS = 1e-6
NEG_INF = -1e30
BAND = BAND_CHUNKS * CHUNK
QKV_A = 3 * H_A * DK_A
MIX_A = H_A * DV_A
MIX_B = H_B * DH_B
MIX_C = H_C * DV_C
LANES = 128
SUBLANES = 8
U_COLS = QKV_A + MIX_A + 3 * MIX_B + 4 * MIX_C + LANES
COL_Z = QKV_A // MIX_A
COL_B = (QKV_A + MIX_A) // MIX_B
COL_BA = (U_COLS - LANES) // LANES
VMEM_LIMIT = 56 * 1024 * 1024

FFN_TM = 1024
FFN_SUB = 512
FFN_TF = 256
PROJ_TM = 512
PROJ_SUB = 256
MIX_TB = 256


def _params(sem):
    return pltpu.CompilerParams(dimension_semantics=sem, vmem_limit_bytes=VMEM_LIMIT)


def _rms(x, g):
    return (x * lax.rsqrt(jnp.mean(x * x, axis=-1, keepdims=True) + EPS)) * g


def _silu(x):
    return x * jax.nn.sigmoid(x)


def _dot(a, b):
    return jnp.dot(a.astype(BF16), b.astype(BF16), preferred_element_type=F32)


def _dot_nt(a, b):
    return lax.dot_general(a.astype(BF16), b.astype(BF16), (((1,), (1,)), ((), ())),
                           preferred_element_type=F32)


def _dot_tn(a, b):
    return lax.dot_general(a.astype(BF16), b.astype(BF16), (((0,), (0,)), ((), ())),
                           preferred_element_type=F32)


def _dot_exact_rhs(a, b):
    hi = a.astype(BF16)
    r1 = a - hi.astype(F32)
    mid = r1.astype(BF16)
    lo = (r1 - mid.astype(F32)).astype(BF16)
    bb = b.astype(BF16)
    out = jnp.dot(hi, bb, preferred_element_type=F32)
    out = out + jnp.dot(mid, bb, preferred_element_type=F32)
    return out + jnp.dot(lo, bb, preferred_element_type=F32)


def _half_swiglu(x, g, wg_ref, wu_ref, wd_ref):
    h = _rms(x, g).astype(BF16)
    acc = None
    for s in range(D_FF // FFN_TF):
        cols = slice(s * FFN_TF, (s + 1) * FFN_TF)
        gate = jnp.dot(h, wg_ref[:, cols], preferred_element_type=F32)
        up = jnp.dot(h, wu_ref[:, cols], preferred_element_type=F32)
        act = (_silu(gate) * up).astype(BF16)
        part = jnp.dot(act, wd_ref[cols, :], preferred_element_type=F32)
        acc = part if acc is None else acc + part
    return 0.5 * acc


def _ffn_kernel(x_ref, g_ref, wg_ref, wu_ref, wd_ref, o_ref, *, sub):
    for r in range(x_ref.shape[0] // sub):
        rows = slice(r * sub, (r + 1) * sub)
        x = x_ref[rows, :]
        o_ref[rows, :] = x + _half_swiglu(x, g_ref[...], wg_ref, wu_ref, wd_ref)


def _mix_ffn_kernel(x_ref, oa_ref, ob_ref, oc_ref, woa_ref, wob_ref, woc_ref, g_ref, wg_ref, wu_ref,
                    wd_ref, gf_ref, o_ref, *, sub, final_norm):
    for r in range(x_ref.shape[0] // sub):
        rows = slice(r * sub, (r + 1) * sub)
        mix = jnp.dot(oa_ref[rows, :], woa_ref[...], preferred_element_type=F32)
        mix = mix + jnp.dot(ob_ref[rows, :], wob_ref[...], preferred_element_type=F32)
        mix = mix + jnp.dot(oc_ref[rows, :], woc_ref[...], preferred_element_type=F32)
        x2 = x_ref[rows, :] + mix
        y = x2 + _half_swiglu(x2, g_ref[...], wg_ref, wu_ref, wd_ref)
        if final_norm:
            y = _rms(y, gf_ref[...])
        o_ref[rows, :] = y


def _resident(shape, index_map):
    return pl.BlockSpec(shape, index_map, pipeline_mode=pl.Buffered(1))


def _ffn_weight_specs(layer):
    return [
        _resident((None, D_MODEL, D_FF), lambda i: (layer, 0, 0)),
        _resident((None, D_MODEL, D_FF), lambda i: (layer, 0, 0)),
        _resident((None, D_FF, D_MODEL), lambda i: (layer, 0, 0)),
    ]


def _ffn(x, g, wg, wu, wd, layer):
    n = x.shape[0]
    tm = min(FFN_TM, n)
    sub = min(FFN_SUB, tm)
    assert n % tm == 0 and tm % sub == 0, (n, tm)
    tok = pl.BlockSpec((tm, D_MODEL), lambda i: (i, 0))
    return pl.pallas_call(
        functools.partial(_ffn_kernel, sub=sub),
        grid=(n // tm,),
        in_specs=[tok, pl.BlockSpec((None, 1, D_MODEL), lambda i: (layer, 0, 0))] + _ffn_weight_specs(layer),
        out_specs=tok,
        out_shape=jax.ShapeDtypeStruct((n, D_MODEL), F32),
        compiler_params=_params(("parallel",)),
        name="ffn",
    )(x, g, wg, wu, wd)


def _mix_ffn(x, oa, ob, oc, w_out, g, wg, wu, wd, g_final, layer, final_norm):
    n = x.shape[0]
    tm = min(FFN_TM, n)
    sub = min(FFN_SUB, tm)
    assert n % tm == 0 and tm % sub == 0, (n, tm)
    tok = pl.BlockSpec((tm, D_MODEL), lambda i: (i, 0))
    in_specs = [
        tok,
        pl.BlockSpec((tm, MIX_A), lambda i: (i, 0)),
        pl.BlockSpec((tm, MIX_B), lambda i: (i, 0)),
        pl.BlockSpec((tm, MIX_C), lambda i: (i, 0)),
        _resident((None, MIX_A, D_MODEL), lambda i: (layer, 0, 0)),
        _resident((None, MIX_B, D_MODEL), lambda i: (layer, MIX_A // MIX_B, 0)),
        _resident((None, MIX_C, D_MODEL), lambda i: (layer, (MIX_A + MIX_B) // MIX_C, 0)),
        pl.BlockSpec((None, 1, D_MODEL), lambda i: (layer, 0, 0)),
    ] + _ffn_weight_specs(layer) + [pl.BlockSpec((1, D_MODEL), lambda i: (0, 0))]
    return pl.pallas_call(
        functools.partial(_mix_ffn_kernel, sub=sub, final_norm=final_norm),
        grid=(n // tm,),
        in_specs=in_specs,
        out_specs=tok,
        out_shape=jax.ShapeDtypeStruct((n, D_MODEL), F32),
        compiler_params=_params(("parallel",)),
        name="mix_ffn",
    )(x, oa, ob, oc, w_out, w_out, w_out, g, wg, wu, wd, g_final)


def _inproj_kernel(x_ref, g_ref, w_ref, o_ref, *, sub):
    for r in range(x_ref.shape[0] // sub):
        rows = slice(r * sub, (r + 1) * sub)
        h = _rms(x_ref[rows, :], g_ref[...]).astype(BF16)
        o_ref[rows, :] = jnp.dot(h, w_ref[...], preferred_element_type=F32)


def _inproj(x, g, w, layer):
    n = x.shape[0]
    tm = min(PROJ_TM, n)
    sub = min(PROJ_SUB, tm)
    assert n % tm == 0 and tm % sub == 0, (n, tm)
    return pl.pallas_call(
        functools.partial(_inproj_kernel, sub=sub),
        grid=(n // tm,),
        in_specs=[pl.BlockSpec((tm, D_MODEL), lambda i: (i, 0)),
                  pl.BlockSpec((None, 1, D_MODEL), lambda i: (layer, 0, 0)),
                  _resident((None, D_MODEL, U_COLS), lambda i: (layer, 0, 0))],
        out_specs=pl.BlockSpec((tm, U_COLS), lambda i: (i, 0)),
        out_shape=jax.ShapeDtypeStruct((n, U_COLS), F32),
        compiler_params=_params(("parallel",)),
        name="inproj",
    )(x, g, w)


def _gdn_kernel(u_ref, z_ref, ba_ref, convw_ref, prm_ref, s0_ref, cb0_ref, oa_ref, sout_ref,
                xbuf, dbuf, bbuf, val_scr, qk_scr, kdec_scr, attn_scr, s_scr, *, tb):
    t = pl.program_id(1)
    nchunks = tb // CHUNK

    @pl.when(t == 0)
    def _():
        xbuf[0:SUBLANES, :] = cb0_ref[0]
        s_scr[...] = s0_ref[0]

    xbuf[SUBLANES:SUBLANES + tb, :] = u_ref[0]

    ba = ba_ref[0]
    x = ba + prm_ref[1:2, :]
    softplus = jnp.maximum(x, 0.0) + jnp.log1p(jnp.exp(-jnp.abs(x)))
    g = prm_ref[0:1, :] * softplus
    bbuf[...] = jax.nn.sigmoid(ba)
    row = lax.broadcasted_iota(jnp.int32, (tb, tb), 0)
    col = lax.broadcasted_iota(jnp.int32, (tb, tb), 1)
    shift = CHUNK.bit_length() - 1
    same_chunk = jnp.right_shift(row, shift) == jnp.right_shift(col, shift)
    tri_bd = jnp.where(same_chunk & (col <= row), 1.0, 0.0)
    dbuf[...] = _cumsum_rows(g, tri_bd)

    r_i = lax.broadcasted_iota(jnp.int32, (CHUNK, CHUNK), 0)
    c_i = lax.broadcasted_iota(jnp.int32, (CHUNK, CHUNK), 1)
    tril = c_i <= r_i
    strict = c_i < r_i
    eye = c_i == r_i
    eye_f = jnp.where(eye, 1.0, 0.0)
    gnorm = prm_ref[2:3, :]

    def conv(c, col0):
        r0 = c * CHUNK + SUBLANES - (CONV_W - 1)
        cols = slice(col0, col0 + LANES)
        y = xbuf[r0:r0 + CHUNK, cols] * convw_ref[0:1, cols]
        for j in range(1, CONV_W):
            y = y + xbuf[r0 + j:r0 + j + CHUNK, cols] * convw_ref[j:j + 1, cols]
        return _silu(y)

    problems = [(c, h) for c in range(nchunks) for h in range(H_A)]
    lmask, qk, a_low = {}, {}, {}
    for c in range(nchunks):
        rows = slice(c * CHUNK, (c + 1) * CHUNK)
        dch = dbuf[rows, :]
        beta = bbuf[rows, :]
        e_d = jnp.exp(dch)
        e_dec = jnp.exp(dch[CHUNK - 1:CHUNK, :] - dch)
        for h in range(H_A):
            p = c * H_A + h
            hcols = slice(h * DV_A, (h + 1) * DV_A)
            q = conv(c, h * DK_A)
            k = conv(c, MIX_A + h * DK_A)
            v = conv(c, 2 * MIX_A + h * DV_A)
            q = (q * lax.rsqrt(jnp.sum(q * q, axis=-1, keepdims=True) + EPS)) * DK_A ** -0.5
            k = k * lax.rsqrt(jnp.sum(k * k, axis=-1, keepdims=True) + EPS)
            dcol = dch[:, 4 + h:5 + h]
            bcol = beta[:, h:h + 1]
            edcol = e_d[:, 4 + h:5 + h]
            drow = jnp.sum(jnp.where(eye, dcol, 0.0), axis=0, keepdims=True)
            diff = dcol - drow
            lmask[c, h] = jnp.where(tril, jnp.exp(jnp.where(tril, diff, 0.0)), 0.0)
            kb = k * bcol
            qk[c, h] = _dot_nt(jnp.concatenate([q, kb], axis=0), k)
            val_scr[rows, hcols] = v * bcol
            qk_scr[p, 0:CHUNK, :] = (q * edcol).astype(BF16)
            qk_scr[p, CHUNK:2 * CHUNK, :] = (kb * edcol).astype(BF16)
            kdec_scr[rows, hcols] = (k * e_dec[:, 4 + h:5 + h]).astype(BF16)
    for c, h in problems:
        attn_scr[c * H_A + h] = (qk[c, h][0:CHUNK] * lmask[c, h]).astype(BF16)
        a_low[c, h] = jnp.where(strict, qk[c, h][CHUNK:2 * CHUNK] * lmask[c, h], 0.0)
    pw = a_low
    inv = {ch: eye_f - a_low[ch] for ch in problems}
    for _ in range(5):
        pw = {ch: _dot(pw[ch], pw[ch]) for ch in problems}
        inv = {ch: inv[ch] + _dot(inv[ch], pw[ch]) for ch in problems}
    for c, h in problems:
        p = c * H_A + h
        rows = slice(c * CHUNK, (c + 1) * CHUNK)
        hcols = slice(h * DV_A, (h + 1) * DV_A)
        rhs = jnp.concatenate([val_scr[rows, hcols].astype(BF16), qk_scr[p, CHUNK:2 * CHUNK, :]], axis=1)
        sol = _dot(inv[c, h], rhs)
        val_scr[rows, hcols] = sol[:, :DV_A]
        qk_scr[p, CHUNK:2 * CHUNK, :] = sol[:, DV_A:].astype(BF16)

    state = [s_scr[h] for h in range(H_A)]
    for c in range(nchunks):
        rows = slice(c * CHUNK, (c + 1) * CHUNK)
        e_last = jnp.exp(dbuf[(c + 1) * CHUNK - 1:(c + 1) * CHUNK, :])
        qs = [_dot(qk_scr[c * H_A + h], state[h]) for h in range(H_A)]
        v_new = [val_scr[rows, h * DV_A:(h + 1) * DV_A] - qs[h][CHUNK:2 * CHUNK] for h in range(H_A)]
        o_in = [_dot(attn_scr[c * H_A + h], v_new[h]) for h in range(H_A)]
        ds = [_dot_tn(kdec_scr[rows, h * DK_A:(h + 1) * DK_A], v_new[h]) for h in range(H_A)]
        for h in range(H_A):
            hcols = slice(h * DV_A, (h + 1) * DV_A)
            o = qs[h][0:CHUNK] + o_in[h]
            state[h] = state[h] * e_last[:, 4 + h:5 + h] + ds[h]
            oa_ref[0, rows, hcols] = (_rms(o, gnorm) * _silu(z_ref[0, rows, hcols])).astype(BF16)
    for h in range(H_A):
        s_scr[h] = state[h]
    xbuf[0:SUBLANES, :] = xbuf[tb:tb + SUBLANES, :]

    @pl.when(t == pl.num_programs(1) - 1)
    def _():
        sout_ref[0] = s_scr[...]


def _cumsum_rows(g, tri_bd):
    hi = g.astype(BF16)
    r1 = g - hi.astype(F32)
    mid = r1.astype(BF16)
    lo = (r1 - mid.astype(F32)).astype(BF16)
    tb = tri_bd.astype(BF16)
    out = jnp.dot(tb, hi, preferred_element_type=F32)
    out = out + jnp.dot(tb, mid, preferred_element_type=F32)
    return out + jnp.dot(tb, lo, preferred_element_type=F32)


def _gdn(u, conv_w, prm, s0, cb0, tb):
    b, t, _ = u.shape
    return pl.pallas_call(
        functools.partial(_gdn_kernel, tb=tb),
        grid=(b, t // tb),
        in_specs=[pl.BlockSpec((1, tb, QKV_A), lambda i, j: (i, j, 0)),
                  pl.BlockSpec((1, tb, MIX_A), lambda i, j: (i, j, COL_Z)),
                  pl.BlockSpec((1, tb, LANES), lambda i, j: (i, j, COL_BA)),
                  pl.BlockSpec((CONV_W, QKV_A), lambda i, j: (0, 0)),
                  pl.BlockSpec((SUBLANES, LANES), lambda i, j: (0, 0)),
                  pl.BlockSpec((1, H_A, DK_A, DV_A), lambda i, j: (i, 0, 0, 0)),
                  pl.BlockSpec((1, SUBLANES, QKV_A), lambda i, j: (i, 0, 0))],
        out_specs=[pl.BlockSpec((1, tb, MIX_A), lambda i, j: (i, j, 0)),
                   pl.BlockSpec((1, H_A, DK_A, DV_A), lambda i, j: (i, 0, 0, 0))],
        out_shape=[jax.ShapeDtypeStruct((b, t, MIX_A), BF16),
                   jax.ShapeDtypeStruct((b, H_A, DK_A, DV_A), F32)],
        scratch_shapes=[pltpu.VMEM((tb + SUBLANES, QKV_A), F32),
                        pltpu.VMEM((tb, LANES), F32), pltpu.VMEM((tb, LANES), F32),
                        pltpu.VMEM((tb, MIX_A), F32),
                        pltpu.VMEM((tb // CHUNK * H_A, 2 * CHUNK, DK_A), BF16),
                        pltpu.VMEM((tb, H_A * DK_A), BF16),
                        pltpu.VMEM((tb // CHUNK * H_A, CHUNK, CHUNK), BF16),
                        pltpu.VMEM((H_A, DK_A, DV_A), F32)],
        compiler_params=_params(("parallel", "arbitrary")),
        name="gdn",
    )(u, u, u, conv_w, prm, s0, cb0)


def _band_kernel(q_ref, k_ref, v_ref, bias_ref, ck_ref, cv_ref, o_ref, kbuf, vbuf, *, qb, t_len, n_invalid):
    t = pl.program_id(1)

    @pl.when(t == 0)
    def _():
        kbuf[0:BAND, :] = ck_ref[0].astype(BF16)
        vbuf[0:BAND, :] = cv_ref[0].astype(BF16)
        kbuf[BAND:BAND + t_len, :] = k_ref[0].astype(BF16)
        vbuf[BAND:BAND + t_len, :] = v_ref[0].astype(BF16)

    start = pl.multiple_of(t * qb, qb)
    kw = kbuf[pl.ds(start, qb + BAND), :]
    vw = vbuf[pl.ds(start, qb + BAND), :]
    q = q_ref[0] * DH_B ** -0.5
    lane = lax.broadcasted_iota(jnp.int32, (1, MIX_B), 1)
    if n_invalid:
        kpos = lax.broadcasted_iota(jnp.int32, (qb, qb + BAND), 1) + start
        valid = kpos >= n_invalid
    acc = jnp.zeros((qb, MIX_B), F32)
    for h in range(H_B):
        mh = (lane >= h * DH_B) & (lane < (h + 1) * DH_B)
        s = _dot_nt(jnp.where(mh, q, 0.0), kw) + bias_ref[h]
        if n_invalid:
            s = jnp.where(valid, s, NEG_INF)
        p = jnp.exp(s - jnp.max(s, axis=-1, keepdims=True))
        o = _dot(p, vw) / jnp.sum(p, axis=-1, keepdims=True)
        acc = acc + jnp.where(mh, o, 0.0)
    o_ref[0] = acc.astype(BF16)


def _band(u, bias, ck, cv, qb, n_invalid):
    b, t, _ = u.shape
    cache_map = (lambda i, j: (i, 0, 0)) if ck.shape[0] == b else (lambda i, j: (0, 0, 0))
    return pl.pallas_call(
        functools.partial(_band_kernel, qb=qb, t_len=t, n_invalid=n_invalid),
        grid=(b, t // qb),
        in_specs=[pl.BlockSpec((1, qb, MIX_B), lambda i, j: (i, j, COL_B)),
                  pl.BlockSpec((1, t, MIX_B), lambda i, j: (i, 0, COL_B + 1)),
                  pl.BlockSpec((1, t, MIX_B), lambda i, j: (i, 0, COL_B + 2)),
                  pl.BlockSpec((H_B, qb, qb + BAND), lambda i, j: (0, 0, 0)),
                  pl.BlockSpec((1, BAND, MIX_B), cache_map),
                  pl.BlockSpec((1, BAND, MIX_B), cache_map)],
        out_specs=pl.BlockSpec((1, qb, MIX_B), lambda i, j: (i, j, 0)),
        out_shape=jax.ShapeDtypeStruct((b, t, MIX_B), BF16),
        scratch_shapes=[pltpu.VMEM((BAND + t, MIX_B), BF16), pltpu.VMEM((BAND + t, MIX_B), BF16)],
        compiler_params=_params(("parallel", "arbitrary")),
        name="band",
    )(u, u, u, bias, ck, cv)


def _band_bias_table(rel_bias, qb):
    rb = rel_bias.astype(F32)
    nh = rb.shape[0]
    w = qb + BAND
    p = w + qb - 1
    rel_lo, rel_hi = -(qb - 1) - BAND, qb - 1
    mid_hi = min(rel_hi, REL_CLIP)
    f = jnp.concatenate([jnp.broadcast_to(rb[:, :1], (nh, -REL_CLIP - rel_lo)),
                         rb[:, :mid_hi + REL_CLIP + 1],
                         jnp.broadcast_to(rb[:, -1:], (nh, rel_hi - mid_hi))], axis=1)
    rolled = jnp.tile(jnp.pad(f, ((0, 0), (0, 1))), (1, qb))[:, :qb * p].reshape(nh, qb, p)
    tab = rolled[:, :, qb - 1:qb - 1 + w]
    ci = np.arange(qb)[:, None] // CHUNK
    cj = np.arange(w)[None, :] // CHUNK
    return jnp.where((cj >= ci) & (cj <= ci + BAND_CHUNKS), tab, NEG_INF)


def _ret_kernel(q_ref, k_ref, v_ref, g_ref, rope_ref, pow_ref, dmask_ref, sq_ref, r0_ref, o_ref, rout_ref,
                r_scr, *, tb):
    t = pl.program_id(1)

    @pl.when(t == 0)
    def _():
        r_scr[...] = r0_ref[0]

    def rope(x):
        return (x * rope_ref[0] + pltpu.roll(x, MIX_C - DK_C // 2, 1) * rope_ref[1]
                + pltpu.roll(x, DK_C // 2, 1) * rope_ref[2])

    q = rope(q_ref[0])
    k = rope(k_ref[0]) * DK_C ** -0.5
    v = v_ref[0]
    r = r_scr[...]
    acc = _dot(q * pow_ref[0], r)
    lane = lax.broadcasted_iota(jnp.int32, (1, MIX_C), 1)
    for h in range(H_C):
        mh = (lane >= h * DK_C) & (lane < (h + 1) * DK_C)
        s = _dot_nt(jnp.where(mh, q, 0.0), k) * dmask_ref[h]
        acc = acc + jnp.where(mh, _dot(s, v), 0.0)
    r_scr[...] = r * sq_ref[0] + sq_ref[1] * _dot_tn(k * pow_ref[1], v)
    avg = sq_ref[2]
    xc = acc - _dot_exact_rhs(acc, avg)
    var = _dot_exact_rhs(xc * xc, avg)
    o_ref[0] = ((xc * lax.rsqrt(var + EPS)) * _silu(g_ref[0])).astype(BF16)

    @pl.when(t == pl.num_programs(1) - 1)
    def _():
        rout_ref[0] = r_scr[...]


def _ret(u, rope_tab, pow_tab, dmask, sq_tab, r0, tb):
    b, t, _ = u.shape
    ublk = lambda c: pl.BlockSpec((1, tb, MIX_C), lambda i, j: (i, j, COL_B + c))
    return pl.pallas_call(
        functools.partial(_ret_kernel, tb=tb),
        grid=(b, t // tb),
        in_specs=[ublk(3), ublk(4), ublk(5), ublk(6),
                  pl.BlockSpec((3, tb, MIX_C), lambda i, j: (0, j, 0)),
                  pl.BlockSpec((2, tb, MIX_C), lambda i, j: (0, 0, 0)),
                  pl.BlockSpec((H_C, tb, tb), lambda i, j: (0, 0, 0)),
                  pl.BlockSpec((3, MIX_C, MIX_C), lambda i, j: (0, 0, 0)),
                  pl.BlockSpec((1, MIX_C, MIX_C), lambda i, j: (i, 0, 0))],
        out_specs=[pl.BlockSpec((1, tb, MIX_C), lambda i, j: (i, j, 0)),
                   pl.BlockSpec((1, MIX_C, MIX_C), lambda i, j: (i, 0, 0))],
        out_shape=[jax.ShapeDtypeStruct((b, t, MIX_C), BF16),
                   jax.ShapeDtypeStruct((b, MIX_C, MIX_C), F32)],
        scratch_shapes=[pltpu.VMEM((MIX_C, MIX_C), F32)],
        compiler_params=_params(("parallel", "arbitrary")),
        name="ret",
    )(u, u, u, u, rope_tab, pow_tab, dmask, sq_tab, r0)


def _ret_tables(t_len, pos0, tb):
    half = DK_C // 2
    inv_freq = np.exp(-math.log(ROPE_BASE) * np.arange(half) / half)
    ang = (pos0 + np.arange(t_len))[:, None] * inv_freq[None, :]
    cos, sin = np.cos(ang), np.sin(ang)
    zero = np.zeros_like(sin)
    per_head = lambda a, b_: np.tile(np.concatenate([a, b_], axis=1), (1, H_C))
    rope_tab = np.stack([per_head(cos, cos), per_head(-sin, zero), per_head(zero, sin)])
    lg = np.log1p(-np.exp2(-5.0 - np.arange(H_C)))
    lg_lane = np.repeat(lg, DK_C)
    n = np.arange(tb, dtype=np.float64)
    pow_tab = np.stack([np.exp(lg_lane[None, :] * (n + 1.0)[:, None]),
                        np.exp(lg_lane[None, :] * (tb - 1.0 - n)[:, None])])
    diff = n[:, None] - n[None, :]
    causal = diff >= 0
    dmask = np.where(causal, np.exp(lg[:, None, None] * np.where(causal, diff, 0.0)), 0.0)
    head = np.arange(MIX_C) // DK_C
    same = (head[:, None] == head[None, :]).astype(np.float64)
    carry = np.broadcast_to(np.exp(lg_lane * tb)[:, None], (MIX_C, MIX_C))
    sq_tab = np.stack([carry, same, same / DV_C])
    return tuple(jnp.asarray(a, F32) for a in (rope_tab, pow_tab, dmask, sq_tab))


def _relayout_w_in(w_in):
    n_small = 2 * H_A
    a_end = QKV_A + MIX_A
    pad = jnp.zeros(w_in.shape[:2] + (LANES - n_small,), w_in.dtype)
    return jnp.concatenate([w_in[..., :a_end], w_in[..., a_end + n_small:],
                            w_in[..., a_end:a_end + n_small], pad], axis=-1).astype(BF16)


def _block_diag(r):
    b = r.shape[0]
    eye = jnp.eye(H_C, dtype=r.dtype)
    return jnp.einsum('bhkv,hg->bhkgv', r, eye).reshape(b, H_C * DK_C, H_C * DV_C)


def _diag_blocks(r):
    b = r.shape[0]
    r = r.reshape(b, H_C, DK_C, H_C, DV_C)
    return jnp.stack([r[:, h, :, h, :] for h in range(H_C)], axis=1)


def _run(x, pos0, caches, weights):
    (norm_ffn1, wg1, wu1, wd1, norm_mix, w_in, conv_w, a_log, dt_bias, gdn_norm, rel_bias, w_out,
     norm_ffn2, wg2, wu2, wd2, norm_final) = weights
    b, t, _ = x.shape
    n = b * t
    tb = min(MIX_TB, t)
    assert t % tb == 0 and tb % CHUNK == 0, (t, tb)
    rope_tab, pow_tab, dmask, sq_tab = _ret_tables(t, pos0, tb)
    gf = norm_final.reshape(1, D_MODEL)
    norm_ffn1, norm_mix, norm_ffn2 = (g.reshape(DEPTH, 1, D_MODEL) for g in (norm_ffn1, norm_mix, norm_ffn2))
    xf = x.reshape(n, D_MODEL)
    new = []
    for l in range(DEPTH):
        if caches is None:
            s_gdn0 = jnp.zeros((b, H_A, DK_A, DV_A), F32)
            cb0 = jnp.zeros((b, SUBLANES, QKV_A), F32)
            r0 = jnp.zeros((b, MIX_C, MIX_C), F32)
            ck = cv = jnp.zeros((1, BAND, MIX_B), F32)
            n_invalid = BAND
        else:
            state_gdn, state_conv, cache_k, cache_v, state_ret = caches
            s_gdn0 = state_gdn[l]
            cb0 = jnp.pad(state_conv[l], ((0, 0), (SUBLANES - (CONV_W - 1), 0), (0, 0)))
            r0 = _block_diag(state_ret[l])
            ck = cache_k[l].reshape(b, BAND, MIX_B)
            cv = cache_v[l].reshape(b, BAND, MIX_B)
            n_invalid = 0
        xf = _ffn(xf, norm_ffn1, wg1, wu1, wd1, l)
        u = _inproj(xf, norm_mix, w_in, l).reshape(b, t, U_COLS)
        decay_lanes = lambda v: jnp.pad(v.astype(F32), (H_A, LANES - 2 * H_A))
        prm = jnp.concatenate([decay_lanes(-jnp.exp(a_log[l].astype(F32)))[None],
                               decay_lanes(dt_bias[l])[None], gdn_norm[l].astype(F32)[None],
                               jnp.zeros((SUBLANES - 3, LANES), F32)], axis=0)
        oa, s_gdn = _gdn(u, conv_w[l], prm, s_gdn0, cb0, tb)
        ob = _band(u, _band_bias_table(rel_bias[l], tb), ck, cv, tb, n_invalid)
        oc, r_new = _ret(u, rope_tab, pow_tab, dmask, sq_tab, r0, tb)
        xf = _mix_ffn(xf, oa.reshape(n, MIX_A), ob.reshape(n, MIX_B), oc.reshape(n, MIX_C), w_out,
                      norm_ffn2, wg2, wu2, wd2, gf, l, l == DEPTH - 1)
        keep = min(BAND, t)
        conv_new = u[:, t - (CONV_W - 1):, :QKV_A]
        kb_new = u[:, t - keep:, (COL_B + 1) * MIX_B:(COL_B + 2) * MIX_B].reshape(b, keep, H_B, DH_B)
        vb_new = u[:, t - keep:, (COL_B + 2) * MIX_B:(COL_B + 3) * MIX_B].reshape(b, keep, H_B, DH_B)
        new.append((s_gdn, conv_new, kb_new, vb_new, _diag_blocks(r_new)))
    g_, c_, k_, v_, r_ = zip(*new)
    return (xf.reshape(b, t, D_MODEL), jnp.stack(g_), jnp.stack(c_), jnp.stack(k_), jnp.stack(v_),
            jnp.stack(r_))


def kernel(x_prompt, x_sample, state_gdn, state_conv, cache_band_k, cache_band_v, state_ret, norm_ffn1, w_ffn1_gate, w_ffn1_up, w_ffn1_down, norm_mix, w_in, conv_w, a_log, dt_bias, gdn_norm, rel_bias, w_out, norm_ffn2, w_ffn2_gate, w_ffn2_up, w_ffn2_down, norm_final):
    past_len = 1024
    weights = (norm_ffn1, w_ffn1_gate.astype(BF16), w_ffn1_up.astype(BF16), w_ffn1_down.astype(BF16),
               norm_mix, _relayout_w_in(w_in), conv_w, a_log, dt_bias, gdn_norm, rel_bias,
               w_out.astype(BF16), norm_ffn2, w_ffn2_gate.astype(BF16), w_ffn2_up.astype(BF16),
               w_ffn2_down.astype(BF16), norm_final)
    p = _run(x_prompt, 0, None, weights)
    s = _run(x_sample, past_len, (state_gdn, state_conv, cache_band_k, cache_band_v, state_ret), weights)
    return (p[0], s[0]) + p[1:] + s[1:]
```

```python
import functools
import math

import jax
import jax.numpy as jnp
import numpy as np
from jax import lax
from jax.experimental import pallas as pl
from jax.experimental.pallas import tpu as pltpu

F32 = jnp.float32
BF16 = jnp.bfloat16

D_MODEL = 1024
DEPTH = 2
CHUNK = 64
H_A, DK_A, DV_A, CONV_W = 4, 128, 128, 4
H_B, DH_B, BAND_CHUNKS, REL_CLIP = 4, 64, 8, 128
H_C, DK_C, DV_C = 4, 64, 64
ROPE_BASE = 10000.0
D_FF = 2816
EPS = 1e-6
NEG_INF = -1e30
BAND = BAND_CHUNKS * CHUNK
QKV_A = 3 * H_A * DK_A
MIX_A = H_A * DV_A
MIX_B = H_B * DH_B
MIX_C = H_C * DV_C
LANES = 128
SUBLANES = 8
U_COLS = QKV_A + MIX_A + 3 * MIX_B + 4 * MIX_C + LANES
COL_Z = QKV_A // MIX_A
COL_B = (QKV_A + MIX_A) // MIX_B
COL_BA = (U_COLS - LANES) // LANES
VMEM_LIMIT = 56 * 1024 * 1024

FFN_TM = 1024
FFN_SUB = 512
FFN_TF = 256
PROJ_TM = 512
PROJ_SUB = 256
MIX_TB = 256
GDN_PROBLEMS = 32
MIX_ROWS = 512


def _params(sem):
    return pltpu.CompilerParams(dimension_semantics=sem, vmem_limit_bytes=VMEM_LIMIT)


def _rms(x, g):
    return (x * lax.rsqrt(jnp.mean(x * x, axis=-1, keepdims=True) + EPS)) * g


def _silu(x):
    return x * jax.nn.sigmoid(x)


def _dot(a, b):
    return jnp.dot(a.astype(BF16), b.astype(BF16), preferred_element_type=F32)


def _dot_nt(a, b):
    return lax.dot_general(a.astype(BF16), b.astype(BF16), (((1,), (1,)), ((), ())),
                           preferred_element_type=F32)


def _dot_tn(a, b):
    return lax.dot_general(a.astype(BF16), b.astype(BF16), (((0,), (0,)), ((), ())),
                           preferred_element_type=F32)


def _dot_exact_rhs(a, b):
    hi = a.astype(BF16)
    r1 = a - hi.astype(F32)
    mid = r1.astype(BF16)
    lo = (r1 - mid.astype(F32)).astype(BF16)
    bb = b.astype(BF16)
    out = jnp.dot(hi, bb, preferred_element_type=F32)
    out = out + jnp.dot(mid, bb, preferred_element_type=F32)
    return out + jnp.dot(lo, bb, preferred_element_type=F32)


def _half_swiglu(x, g, wg_ref, wu_ref, wd_ref):
    h = _rms(x, g).astype(BF16)
    acc = None
    for s in range(D_FF // FFN_TF):
        cols = slice(s * FFN_TF, (s + 1) * FFN_TF)
        gate = jnp.dot(h, wg_ref[:, cols], preferred_element_type=F32)
        up = jnp.dot(h, wu_ref[:, cols], preferred_element_type=F32)
        act = (_silu(gate) * up).astype(BF16)
        part = jnp.dot(act, wd_ref[cols, :], preferred_element_type=F32)
        acc = part if acc is None else acc + part
    return 0.5 * acc


def _ffn_kernel(x_ref, g_ref, wg_ref, wu_ref, wd_ref, o_ref, *, sub):
    for r in range(x_ref.shape[0] // sub):
        rows = slice(r * sub, (r + 1) * sub)
        x = x_ref[rows, :]
        o_ref[rows, :] = x + _half_swiglu(x, g_ref[...], wg_ref, wu_ref, wd_ref)


def _mix_ffn_kernel(x_ref, oa_ref, ob_ref, oc_ref, woa_ref, wob_ref, woc_ref, g_ref, wg_ref, wu_ref,
                    wd_ref, gf_ref, o_ref, *, sub, final_norm):
    for r in range(x_ref.shape[0] // sub):
        rows = slice(r * sub, (r + 1) * sub)
        mix = jnp.dot(oa_ref[rows, :], woa_ref[...], preferred_element_type=F32)
        mix = mix + jnp.dot(ob_ref[rows, :], wob_ref[...], preferred_element_type=F32)
        mix = mix + jnp.dot(oc_ref[rows, :], woc_ref[...], preferred_element_type=F32)
        x2 = x_ref[rows, :] + mix
        y = x2 + _half_swiglu(x2, g_ref[...], wg_ref, wu_ref, wd_ref)
        if final_norm:
            y = _rms(y, gf_ref[...])
        o_ref[rows, :] = y


def _resident(shape, index_map):
    return pl.BlockSpec(shape, index_map, pipeline_mode=pl.Buffered(1))


def _ffn_weight_specs(layer):
    return [
        _resident((None, D_MODEL, D_FF), lambda i: (layer, 0, 0)),
        _resident((None, D_MODEL, D_FF), lambda i: (layer, 0, 0)),
        _resident((None, D_FF, D_MODEL), lambda i: (layer, 0, 0)),
    ]


def _ffn(x, g, wg, wu, wd, layer):
    n = x.shape[0]
    tm = min(FFN_TM, n)
    sub = min(FFN_SUB, tm)
    assert n % tm == 0 and tm % sub == 0, (n, tm)
    tok = pl.BlockSpec((tm, D_MODEL), lambda i: (i, 0))
    return pl.pallas_call(
        functools.partial(_ffn_kernel, sub=sub),
        grid=(n // tm,),
        in_specs=[tok, pl.BlockSpec((None, 1, D_MODEL), lambda i: (layer, 0, 0))] + _ffn_weight_specs(layer),
        out_specs=tok,
        out_shape=jax.ShapeDtypeStruct((n, D_MODEL), F32),
        compiler_params=_params(("parallel",)),
        name="ffn",
    )(x, g, wg, wu, wd)


def _mix_ffn(x, oa, ob, oc, w_out, g, wg, wu, wd, g_final, layer, final_norm):
    n = x.shape[0]
    tm = min(FFN_TM, n)
    sub = min(FFN_SUB, tm)
    assert n % tm == 0 and tm % sub == 0, (n, tm)
    tok = pl.BlockSpec((tm, D_MODEL), lambda i: (i, 0))
    in_specs = [
        tok,
        pl.BlockSpec((tm, MIX_A), lambda i: (i, 0)),
        pl.BlockSpec((tm, MIX_B), lambda i: (i, 0)),
        pl.BlockSpec((tm, MIX_C), lambda i: (i, 0)),
        _resident((None, MIX_A, D_MODEL), lambda i: (layer, 0, 0)),
        _resident((None, MIX_B, D_MODEL), lambda i: (layer, MIX_A // MIX_B, 0)),
        _resident((None, MIX_C, D_MODEL), lambda i: (layer, (MIX_A + MIX_B) // MIX_C, 0)),
        pl.BlockSpec((None, 1, D_MODEL), lambda i: (layer, 0, 0)),
    ] + _ffn_weight_specs(layer) + [pl.BlockSpec((1, D_MODEL), lambda i: (0, 0))]
    return pl.pallas_call(
        functools.partial(_mix_ffn_kernel, sub=sub, final_norm=final_norm),
        grid=(n // tm,),
        in_specs=in_specs,
        out_specs=tok,
        out_shape=jax.ShapeDtypeStruct((n, D_MODEL), F32),
        compiler_params=_params(("parallel",)),
        name="mix_ffn",
    )(x, oa, ob, oc, w_out, w_out, w_out, g, wg, wu, wd, g_final)


def _inproj_kernel(x_ref, g_ref, w_ref, o_ref, *, sub):
    for r in range(x_ref.shape[0] // sub):
        rows = slice(r * sub, (r + 1) * sub)
        h = _rms(x_ref[rows, :], g_ref[...]).astype(BF16)
        o_ref[rows, :] = jnp.dot(h, w_ref[...], preferred_element_type=F32)


def _inproj(x, g, w, layer):
    n = x.shape[0]
    tm = min(PROJ_TM, n)
    sub = min(PROJ_SUB, tm)
    assert n % tm == 0 and tm % sub == 0, (n, tm)
    return pl.pallas_call(
        functools.partial(_inproj_kernel, sub=sub),
        grid=(n // tm,),
        in_specs=[pl.BlockSpec((tm, D_MODEL), lambda i: (i, 0)),
                  pl.BlockSpec((None, 1, D_MODEL), lambda i: (layer, 0, 0)),
                  _resident((None, D_MODEL, U_COLS), lambda i: (layer, 0, 0))],
        out_specs=pl.BlockSpec((tm, U_COLS), lambda i: (i, 0)),
        out_shape=jax.ShapeDtypeStruct((n, U_COLS), F32),
        compiler_params=_params(("parallel",)),
        name="inproj",
    )(x, g, w)


def _gdn_kernel(u_ref, z_ref, ba_ref, convw_ref, prm_ref, s0_ref, cb0_ref, oa_ref, sout_ref,
                xbuf, dbuf, bbuf, val_scr, qk_scr, kdec_scr, attn_scr, s_scr, *, nb, tb):
    t = pl.program_id(1)
    nchunks = tb // CHUNK
    nslab = QKV_A // LANES

    @pl.when(t == 0)
    def _():
        for bi in range(nb):
            for s in range(nslab):
                xbuf[bi * nslab + s, 0:SUBLANES, :] = cb0_ref[bi, :, s * LANES:(s + 1) * LANES]
        s_scr[...] = s0_ref[...].reshape(s_scr.shape)

    for bi in range(nb):
        for s in range(nslab):
            xbuf[bi * nslab + s, SUBLANES:SUBLANES + tb, :] = u_ref[bi, :, s * LANES:(s + 1) * LANES]

    row = lax.broadcasted_iota(jnp.int32, (tb, tb), 0)
    col = lax.broadcasted_iota(jnp.int32, (tb, tb), 1)
    shift = CHUNK.bit_length() - 1
    same_chunk = jnp.right_shift(row, shift) == jnp.right_shift(col, shift)
    tri_bd = jnp.where(same_chunk & (col <= row), 1.0, 0.0)
    for bi in range(nb):
        ba = ba_ref[bi]
        x = ba + prm_ref[1:2, :]
        softplus = jnp.maximum(x, 0.0) + jnp.log1p(jnp.exp(-jnp.abs(x)))
        g = prm_ref[0:1, :] * softplus
        bbuf[bi] = jax.nn.sigmoid(ba)
        dbuf[bi] = _cumsum_rows(g, tri_bd)

    r_i = lax.broadcasted_iota(jnp.int32, (CHUNK, CHUNK), 0)
    c_i = lax.broadcasted_iota(jnp.int32, (CHUNK, CHUNK), 1)
    tril = c_i <= r_i
    strict = c_i < r_i
    eye = c_i == r_i
    eye_f = jnp.where(eye, 1.0, 0.0)
    gnorm = prm_ref[2:3, :]

    def conv(bi, c, col0):
        r0 = c * CHUNK + SUBLANES - (CONV_W - 1)
        cols = slice(col0, col0 + LANES)
        slab = bi * nslab + col0 // LANES
        y = xbuf[slab, r0:r0 + CHUNK, :] * convw_ref[0:1, cols]
        for j in range(1, CONV_W):
            y = y + xbuf[slab, r0 + j:r0 + j + CHUNK, :] * convw_ref[j:j + 1, cols]
        return _silu(y)

    def pidx(bi, c, h):
        return (bi * nchunks + c) * H_A + h

    problems = [(bi, c, h) for bi in range(nb) for c in range(nchunks) for h in range(H_A)]
    lmask, qk, a_low = {}, {}, {}
    for bi in range(nb):
        for c in range(nchunks):
            rows = slice(c * CHUNK, (c + 1) * CHUNK)
            dch = dbuf[bi, rows, :]
            beta = bbuf[bi, rows, :]
            e_d = jnp.exp(dch)
            e_dec = jnp.exp(dch[CHUNK - 1:CHUNK, :] - dch)
            for h in range(H_A):
                p = pidx(bi, c, h)
                hcols = slice(h * DV_A, (h + 1) * DV_A)
                q = conv(bi, c, h * DK_A)
                k = conv(bi, c, MIX_A + h * DK_A)
                v = conv(bi, c, 2 * MIX_A + h * DV_A)
                q = (q * lax.rsqrt(jnp.sum(q * q, axis=-1, keepdims=True) + EPS)) * DK_A ** -0.5
                k = k * lax.rsqrt(jnp.sum(k * k, axis=-1, keepdims=True) + EPS)
                dcol = dch[:, 4 + h:5 + h]
                bcol = beta[:, h:h + 1]
                edcol = e_d[:, 4 + h:5 + h]
                drow = jnp.sum(jnp.where(eye, dcol, 0.0), axis=0, keepdims=True)
                diff = dcol - drow
                lmask[bi, c, h] = jnp.where(tril, jnp.exp(jnp.where(tril, diff, 0.0)), 0.0)
                kb = k * bcol
                qk[bi, c, h] = _dot_nt(jnp.concatenate([q, kb], axis=0), k)
                val_scr[bi, rows, hcols] = v * bcol
                qk_scr[p, 0:CHUNK, :] = (q * edcol).astype(BF16)
                qk_scr[p, CHUNK:2 * CHUNK, :] = (kb * edcol).astype(BF16)
                kdec_scr[bi, rows, hcols] = (k * e_dec[:, 4 + h:5 + h]).astype(BF16)
    for pr in problems:
        attn_scr[pidx(*pr)] = (qk[pr][0:CHUNK] * lmask[pr]).astype(BF16)
        a_low[pr] = jnp.where(strict, qk[pr][CHUNK:2 * CHUNK] * lmask[pr], 0.0)
    pw = a_low
    inv = {pr: eye_f - a_low[pr] for pr in problems}
    for _ in range(5):
        pw = {pr: _dot(pw[pr], pw[pr]) for pr in problems}
        inv = {pr: inv[pr] + _dot(inv[pr], pw[pr]) for pr in problems}
    for bi, c, h in problems:
        p = pidx(bi, c, h)
        rows = slice(c * CHUNK, (c + 1) * CHUNK)
        hcols = slice(h * DV_A, (h + 1) * DV_A)
        rhs = jnp.concatenate([val_scr[bi, rows, hcols].astype(BF16), qk_scr[p, CHUNK:2 * CHUNK, :]], axis=1)
        sol = _dot(inv[bi, c, h], rhs)
        val_scr[bi, rows, hcols] = sol[:, :DV_A]
        qk_scr[p, CHUNK:2 * CHUNK, :] = sol[:, DV_A:].astype(BF16)

    chains = [(bi, h) for bi in range(nb) for h in range(H_A)]
    state = {ch: s_scr[ch[0] * H_A + ch[1]] for ch in chains}
    for c in range(nchunks):
        rows = slice(c * CHUNK, (c + 1) * CHUNK)
        qs = {(bi, h): _dot(qk_scr[pidx(bi, c, h)], state[bi, h]) for bi, h in chains}
        v_new = {(bi, h): val_scr[bi, rows, h * DV_A:(h + 1) * DV_A] - qs[bi, h][CHUNK:2 * CHUNK]
                 for bi, h in chains}
        o_in = {(bi, h): _dot(attn_scr[pidx(bi, c, h)], v_new[bi, h]) for bi, h in chains}
        ds = {(bi, h): _dot_tn(kdec_scr[bi, rows, h * DK_A:(h + 1) * DK_A], v_new[bi, h]) for bi, h in chains}
        for bi, h in chains:
            hcols = slice(h * DV_A, (h + 1) * DV_A)
            e_last = jnp.exp(dbuf[bi, (c + 1) * CHUNK - 1:(c + 1) * CHUNK, 4 + h:5 + h])
            o = qs[bi, h][0:CHUNK] + o_in[bi, h]
            state[bi, h] = state[bi, h] * e_last + ds[bi, h]
            oa_ref[bi, rows, hcols] = (_rms(o, gnorm) * _silu(z_ref[bi, rows, hcols])).astype(BF16)
    for bi, h in chains:
        s_scr[bi * H_A + h] = state[bi, h]
    for s in range(nb * nslab):
        xbuf[s, 0:SUBLANES, :] = xbuf[s, tb:tb + SUBLANES, :]

    @pl.when(t == pl.num_programs(1) - 1)
    def _():
        sout_ref[...] = s_scr[...].reshape(sout_ref.shape)


def _cumsum_rows(g, tri_bd):
    hi = g.astype(BF16)
    r1 = g - hi.astype(F32)
    mid = r1.astype(BF16)
    lo = (r1 - mid.astype(F32)).astype(BF16)
    tb = tri_bd.astype(BF16)
    out = jnp.dot(tb, hi, preferred_element_type=F32)
    out = out + jnp.dot(tb, mid, preferred_element_type=F32)
    return out + jnp.dot(tb, lo, preferred_element_type=F32)


def _gdn(u, conv_w, prm, s0, cb0, tb):
    b, t, _ = u.shape
    nb = max(1, min(b, GDN_PROBLEMS // (tb // CHUNK * H_A)))
    assert b % nb == 0, (b, nb)
    nprob = nb * (tb // CHUNK) * H_A
    return pl.pallas_call(
        functools.partial(_gdn_kernel, nb=nb, tb=tb),
        grid=(b // nb, t // tb),
        in_specs=[pl.BlockSpec((nb, tb, QKV_A), lambda i, j: (i, j, 0)),
                  pl.BlockSpec((nb, tb, MIX_A), lambda i, j: (i, j, COL_Z)),
                  pl.BlockSpec((nb, tb, LANES), lambda i, j: (i, j, COL_BA)),
                  pl.BlockSpec((CONV_W, QKV_A), lambda i, j: (0, 0)),
                  pl.BlockSpec((SUBLANES, LANES), lambda i, j: (0, 0)),
                  pl.BlockSpec((nb, H_A, DK_A, DV_A), lambda i, j: (i, 0, 0, 0)),
                  pl.BlockSpec((nb, SUBLANES, QKV_A), lambda i, j: (i, 0, 0))],
        out_specs=[pl.BlockSpec((nb, tb, MIX_A), lambda i, j: (i, j, 0)),
                   pl.BlockSpec((nb, H_A, DK_A, DV_A), lambda i, j: (i, 0, 0, 0))],
        out_shape=[jax.ShapeDtypeStruct((b, t, MIX_A), BF16),
                   jax.ShapeDtypeStruct((b, H_A, DK_A, DV_A), F32)],
        scratch_shapes=[pltpu.VMEM((nb * QKV_A // LANES, tb + SUBLANES, LANES), F32),
                        pltpu.VMEM((nb, tb, LANES), F32), pltpu.VMEM((nb, tb, LANES), F32),
                        pltpu.VMEM((nb, tb, MIX_A), F32),
                        pltpu.VMEM((nprob, 2 * CHUNK, DK_A), BF16),
                        pltpu.VMEM((nb, tb, H_A * DK_A), BF16),
                        pltpu.VMEM((nprob, CHUNK, CHUNK), BF16),
                        pltpu.VMEM((nb * H_A, DK_A, DV_A), F32)],
        compiler_params=_params(("parallel", "arbitrary")),
        name="gdn",
    )(u, u, u, conv_w, prm, s0, cb0)


def _band_kernel(q_ref, k_ref, v_ref, bias_ref, ck_ref, cv_ref, o_ref, kbuf, vbuf, *, nb, qb, t_len, n_invalid):
    t = pl.program_id(1)

    @pl.when(t == 0)
    def _():
        for bi in range(nb):
            kbuf[bi, 0:BAND, :] = ck_ref[bi].astype(BF16)
            vbuf[bi, 0:BAND, :] = cv_ref[bi].astype(BF16)
            kbuf[bi, BAND:BAND + t_len, :] = k_ref[bi].astype(BF16)
            vbuf[bi, BAND:BAND + t_len, :] = v_ref[bi].astype(BF16)

    start = pl.multiple_of(t * qb, qb)
    lane = lax.broadcasted_iota(jnp.int32, (1, MIX_B), 1)
    if n_invalid:
        kpos = lax.broadcasted_iota(jnp.int32, (qb, qb + BAND), 1) + start
        valid = kpos >= n_invalid
    for bi in range(nb):
        kw = kbuf[bi, pl.ds(start, qb + BAND), :]
        vw = vbuf[bi, pl.ds(start, qb + BAND), :]
        q = q_ref[bi] * DH_B ** -0.5
        acc = jnp.zeros((qb, MIX_B), F32)
        for h in range(H_B):
            mh = (lane >= h * DH_B) & (lane < (h + 1) * DH_B)
            s = _dot_nt(jnp.where(mh, q, 0.0), kw) + bias_ref[h]
            if n_invalid:
                s = jnp.where(valid, s, NEG_INF)
            p = jnp.exp(s - jnp.max(s, axis=-1, keepdims=True))
            o = _dot(p, vw) / jnp.sum(p, axis=-1, keepdims=True)
            acc = acc + jnp.where(mh, o, 0.0)
        o_ref[bi] = acc.astype(BF16)


def _band(u, bias, ck, cv, qb, n_invalid):
    b, t, _ = u.shape
    nb = max(1, min(b, MIX_ROWS // qb))
    assert b % nb == 0 and ck.shape[0] == b, (b, nb)
    return pl.pallas_call(
        functools.partial(_band_kernel, nb=nb, qb=qb, t_len=t, n_invalid=n_invalid),
        grid=(b // nb, t // qb),
        in_specs=[pl.BlockSpec((nb, qb, MIX_B), lambda i, j: (i, j, COL_B)),
                  pl.BlockSpec((nb, t, MIX_B), lambda i, j: (i, 0, COL_B + 1)),
                  pl.BlockSpec((nb, t, MIX_B), lambda i, j: (i, 0, COL_B + 2)),
                  pl.BlockSpec((H_B, qb, qb + BAND), lambda i, j: (0, 0, 0)),
                  pl.BlockSpec((nb, BAND, MIX_B), lambda i, j: (i, 0, 0)),
                  pl.BlockSpec((nb, BAND, MIX_B), lambda i, j: (i, 0, 0))],
        out_specs=pl.BlockSpec((nb, qb, MIX_B), lambda i, j: (i, j, 0)),
        out_shape=jax.ShapeDtypeStruct((b, t, MIX_B), BF16),
        scratch_shapes=[pltpu.VMEM((nb, BAND + t, MIX_B), BF16), pltpu.VMEM((nb, BAND + t, MIX_B), BF16)],
        compiler_params=_params(("parallel", "arbitrary")),
        name="band",
    )(u, u, u, bias, ck, cv)


def _band_bias_table(rel_bias, qb):
    rb = rel_bias.astype(F32)
    nh = rb.shape[0]
    w = qb + BAND
    p = w + qb - 1
    rel_lo, rel_hi = -(qb - 1) - BAND, qb - 1
    mid_hi = min(rel_hi, REL_CLIP)
    f = jnp.concatenate([jnp.broadcast_to(rb[:, :1], (nh, -REL_CLIP - rel_lo)),
                         rb[:, :mid_hi + REL_CLIP + 1],
                         jnp.broadcast_to(rb[:, -1:], (nh, rel_hi - mid_hi))], axis=1)
    rolled = jnp.tile(jnp.pad(f, ((0, 0), (0, 1))), (1, qb))[:, :qb * p].reshape(nh, qb, p)
    tab = rolled[:, :, qb - 1:qb - 1 + w]
    ci = np.arange(qb)[:, None] // CHUNK
    cj = np.arange(w)[None, :] // CHUNK
    return jnp.where((cj >= ci) & (cj <= ci + BAND_CHUNKS), tab, NEG_INF)


def _ret_kernel(q_ref, k_ref, v_ref, g_ref, rope_ref, pow_ref, dmask_ref, sq_ref, r0_ref, o_ref, rout_ref,
                r_scr, *, nb, tb):
    t = pl.program_id(1)

    @pl.when(t == 0)
    def _():
        r_scr[...] = r0_ref[...]

    def rope(x):
        return (x * rope_ref[0] + pltpu.roll(x, MIX_C - DK_C // 2, 1) * rope_ref[1]
                + pltpu.roll(x, DK_C // 2, 1) * rope_ref[2])

    lane = lax.broadcasted_iota(jnp.int32, (1, MIX_C), 1)
    avg = sq_ref[2]
    for bi in range(nb):
        q = rope(q_ref[bi])
        k = rope(k_ref[bi]) * DK_C ** -0.5
        v = v_ref[bi]
        r = r_scr[bi]
        acc = _dot(q * pow_ref[0], r)
        for h in range(H_C):
            mh = (lane >= h * DK_C) & (lane < (h + 1) * DK_C)
            s = _dot_nt(jnp.where(mh, q, 0.0), k) * dmask_ref[h]
            acc = acc + jnp.where(mh, _dot(s, v), 0.0)
        r_scr[bi] = r * sq_ref[0] + sq_ref[1] * _dot_tn(k * pow_ref[1], v)
        xc = acc - _dot_exact_rhs(acc, avg)
        var = _dot_exact_rhs(xc * xc, avg)
        o_ref[bi] = ((xc * lax.rsqrt(var + EPS)) * _silu(g_ref[bi])).astype(BF16)

    @pl.when(t == pl.num_programs(1) - 1)
    def _():
        rout_ref[...] = r_scr[...]


def _ret(u, rope_tab, pow_tab, dmask, sq_tab, r0, tb):
    b, t, _ = u.shape
    nb = max(1, min(b, MIX_ROWS // tb))
    assert b % nb == 0, (b, nb)
    ublk = lambda c: pl.BlockSpec((nb, tb, MIX_C), lambda i, j: (i, j, COL_B + c))
    return pl.pallas_call(
        functools.partial(_ret_kernel, nb=nb, tb=tb),
        grid=(b // nb, t // tb),
        in_specs=[ublk(3), ublk(4), ublk(5), ublk(6),
                  pl.BlockSpec((3, tb, MIX_C), lambda i, j: (0, j, 0)),
                  pl.BlockSpec((2, tb, MIX_C), lambda i, j: (0, 0, 0)),
                  pl.BlockSpec((H_C, tb, tb), lambda i, j: (0, 0, 0)),
                  pl.BlockSpec((3, MIX_C, MIX_C), lambda i, j: (0, 0, 0)),
                  pl.BlockSpec((nb, MIX_C, MIX_C), lambda i, j: (i, 0, 0))],
        out_specs=[pl.BlockSpec((nb, tb, MIX_C), lambda i, j: (i, j, 0)),
                   pl.BlockSpec((nb, MIX_C, MIX_C), lambda i, j: (i, 0, 0))],
        out_shape=[jax.ShapeDtypeStruct((b, t, MIX_C), BF16),
                   jax.ShapeDtypeStruct((b, MIX_C, MIX_C), F32)],
        scratch_shapes=[pltpu.VMEM((nb, MIX_C, MIX_C), F32)],
        compiler_params=_params(("parallel", "arbitrary")),
        name="ret",
    )(u, u, u, u, rope_tab, pow_tab, dmask, sq_tab, r0)


def _ret_tables(t_len, pos0, tb):
    half = DK_C // 2
    inv_freq = np.exp(-math.log(ROPE_BASE) * np.arange(half) / half)
    ang = (pos0 + np.arange(t_len))[:, None] * inv_freq[None, :]
    cos, sin = np.cos(ang), np.sin(ang)
    zero = np.zeros_like(sin)
    per_head = lambda a, b_: np.tile(np.concatenate([a, b_], axis=1), (1, H_C))
    rope_tab = np.stack([per_head(cos, cos), per_head(-sin, zero), per_head(zero, sin)])
    lg = np.log1p(-np.exp2(-5.0 - np.arange(H_C)))
    lg_lane = np.repeat(lg, DK_C)
    n = np.arange(tb, dtype=np.float64)
    pow_tab = np.stack([np.exp(lg_lane[None, :] * (n + 1.0)[:, None]),
                        np.exp(lg_lane[None, :] * (tb - 1.0 - n)[:, None])])
    diff = n[:, None] - n[None, :]
    causal = diff >= 0
    dmask = np.where(causal, np.exp(lg[:, None, None] * np.where(causal, diff, 0.0)), 0.0)
    head = np.arange(MIX_C) // DK_C
    same = (head[:, None] == head[None, :]).astype(np.float64)
    carry = np.broadcast_to(np.exp(lg_lane * tb)[:, None], (MIX_C, MIX_C))
    sq_tab = np.stack([carry, same, same / DV_C])
    return tuple(jnp.asarray(a, F32) for a in (rope_tab, pow_tab, dmask, sq_tab))


def _relayout_w_in(w_in):
    n_small = 2 * H_A
    a_end = QKV_A + MIX_A
    pad = jnp.zeros(w_in.shape[:2] + (LANES - n_small,), w_in.dtype)
    return jnp.concatenate([w_in[..., :a_end], w_in[..., a_end + n_small:],
                            w_in[..., a_end:a_end + n_small], pad], axis=-1).astype(BF16)


def _block_diag(r):
    b = r.shape[0]
    eye = jnp.eye(H_C, dtype=r.dtype)
    return jnp.einsum('bhkv,hg->bhkgv', r, eye).reshape(b, H_C * DK_C, H_C * DV_C)


def _diag_blocks(r):
    b = r.shape[0]
    r = r.reshape(b, H_C, DK_C, H_C, DV_C)
    return jnp.stack([r[:, h, :, h, :] for h in range(H_C)], axis=1)


def _run(x, pos0, caches, weights):
    (norm_ffn1, wg1, wu1, wd1, norm_mix, w_in, conv_w, a_log, dt_bias, gdn_norm, rel_bias, w_out,
     norm_ffn2, wg2, wu2, wd2, norm_final) = weights
    b, t, _ = x.shape
    n = b * t
    tb = min(MIX_TB, t)
    assert t % tb == 0 and tb % CHUNK == 0, (t, tb)
    rope_tab, pow_tab, dmask, sq_tab = _ret_tables(t, pos0, tb)
    gf = norm_final.reshape(1, D_MODEL)
    norm_ffn1, norm_mix, norm_ffn2 = (g.reshape(DEPTH, 1, D_MODEL) for g in (norm_ffn1, norm_mix, norm_ffn2))
    xf = x.reshape(n, D_MODEL)
    new = []
    for l in range(DEPTH):
        if caches is None:
            s_gdn0 = jnp.zeros((b, H_A, DK_A, DV_A), F32)
            cb0 = jnp.zeros((b, SUBLANES, QKV_A), F32)
            r0 = jnp.zeros((b, MIX_C, MIX_C), F32)
            ck = cv = jnp.zeros((b, BAND, MIX_B), F32)
            n_invalid = BAND
        else:
            state_gdn, state_conv, cache_k, cache_v, state_ret = caches
            s_gdn0 = state_gdn[l]
            cb0 = jnp.pad(state_conv[l], ((0, 0), (SUBLANES - (CONV_W - 1), 0), (0, 0)))
            r0 = _block_diag(state_ret[l])
            ck = cache_k[l].reshape(b, BAND, MIX_B)
            cv = cache_v[l].reshape(b, BAND, MIX_B)
            n_invalid = 0
        xf = _ffn(xf, norm_ffn1, wg1, wu1, wd1, l)
        u = _inproj(xf, norm_mix, w_in, l).reshape(b, t, U_COLS)
        decay_lanes = lambda v: jnp.pad(v.astype(F32), (H_A, LANES - 2 * H_A))
        prm = jnp.concatenate([decay_lanes(-jnp.exp(a_log[l].astype(F32)))[None],
                               decay_lanes(dt_bias[l])[None], gdn_norm[l].astype(F32)[None],
                               jnp.zeros((SUBLANES - 3, LANES), F32)], axis=0)
        oa, s_gdn = _gdn(u, conv_w[l], prm, s_gdn0, cb0, tb)
        ob = _band(u, _band_bias_table(rel_bias[l], tb), ck, cv, tb, n_invalid)
        oc, r_new = _ret(u, rope_tab, pow_tab, dmask, sq_tab, r0, tb)
        xf = _mix_ffn(xf, oa.reshape(n, MIX_A), ob.reshape(n, MIX_B), oc.reshape(n, MIX_C), w_out,
                      norm_ffn2, wg2, wu2, wd2, gf, l, l == DEPTH - 1)
        keep = min(BAND, t)
        conv_new = u[:, t - (CONV_W - 1):, :QKV_A]
        kb_new = u[:, t - keep:, (COL_B + 1) * MIX_B:(COL_B + 2) * MIX_B].reshape(b, keep, H_B, DH_B)
        vb_new = u[:, t - keep:, (COL_B + 2) * MIX_B:(COL_B + 3) * MIX_B].reshape(b, keep, H_B, DH_B)
        new.append((s_gdn, conv_new, kb_new, vb_new, _diag_blocks(r_new)))
    g_, c_, k_, v_, r_ = zip(*new)
    return (xf.reshape(b, t, D_MODEL), jnp.stack(g_), jnp.stack(c_), jnp.stack(k_), jnp.stack(v_),
            jnp.stack(r_))


def kernel(x_prompt, x_sample, state_gdn, state_conv, cache_band_k, cache_band_v, state_ret, norm_ffn1, w_ffn1_gate, w_ffn1_up, w_ffn1_down, norm_mix, w_in, conv_w, a_log, dt_bias, gdn_norm, rel_bias, w_out, norm_ffn2, w_ffn2_gate, w_ffn2_up, w_ffn2_down, norm_final):
    past_len = 1024
    weights = (norm_ffn1, w_ffn1_gate.astype(BF16), w_ffn1_up.astype(BF16), w_ffn1_down.astype(BF16),
               norm_mix, _relayout_w_in(w_in), conv_w, a_log, dt_bias, gdn_norm, rel_bias,
               w_out.astype(BF16), norm_ffn2, w_ffn2_gate.astype(BF16), w_ffn2_up.astype(BF16),
               w_ffn2_down.astype(BF16), norm_final)
    p = _run(x_prompt, 0, None, weights)
    s = _run(x_sample, past_len, (state_gdn, state_conv, cache_band_k, cache_band_v, state_ret), weights)
    return (p[0], s[0]) + p[1:] + s[1:]
```

```python
import functools
import math

import jax
import jax.numpy as jnp
import numpy as np
from jax import lax
from jax.experimental import pallas as pl
from jax.experimental.pallas import tpu as pltpu

F32 = jnp.float32
BF16 = jnp.bfloat16

D_MODEL = 1024
DEPTH = 2
CHUNK = 64
H_A, DK_A, DV_A, CONV_W = 4, 128, 128, 4
H_B, DH_B, BAND_CHUNKS, REL_CLIP = 4, 64, 8, 128
H_C, DK_C, DV_C = 4, 64, 64
ROPE_BASE = 10000.0
D_FF = 2816
EPS = 1e-6
NEG_INF = -1e30
BAND = BAND_CHUNKS * CHUNK
QKV_A = 3 * H_A * DK_A
MIX_A = H_A * DV_A
MIX_B = H_B * DH_B
MIX_C = H_C * DV_C
LANES = 128
SUBLANES = 8
U_COLS = QKV_A + MIX_A + 3 * MIX_B + 4 * MIX_C + LANES
COL_Z = QKV_A // MIX_A
COL_B = (QKV_A + MIX_A) // MIX_B
COL_BA = (U_COLS - LANES) // LANES
VMEM_LIMIT = 56 * 1024 * 1024

FFN_TM = 1024
FFN_SUB = 512
FFN_TF = 256
PROJ_TM = 512
PROJ_SUB = 256
MIX_TB = 256
GDN_PROBLEMS = 32
MIX_ROWS = 512


def _params(sem):
    return pltpu.CompilerParams(dimension_semantics=sem, vmem_limit_bytes=VMEM_LIMIT)


def _rms(x, g):
    return (x * lax.rsqrt(jnp.mean(x * x, axis=-1, keepdims=True) + EPS)) * g


def _silu(x):
    return x * jax.nn.sigmoid(x)


def _dot(a, b):
    return jnp.dot(a.astype(BF16), b.astype(BF16), preferred_element_type=F32)


def _dot_nt(a, b):
    return lax.dot_general(a.astype(BF16), b.astype(BF16), (((1,), (1,)), ((), ())),
                           preferred_element_type=F32)


def _dot_tn(a, b):
    return lax.dot_general(a.astype(BF16), b.astype(BF16), (((0,), (0,)), ((), ())),
                           preferred_element_type=F32)


def _dot_exact_rhs(a, b):
    hi = a.astype(BF16)
    r1 = a - hi.astype(F32)
    mid = r1.astype(BF16)
    lo = (r1 - mid.astype(F32)).astype(BF16)
    bb = b.astype(BF16)
    out = jnp.dot(hi, bb, preferred_element_type=F32)
    out = out + jnp.dot(mid, bb, preferred_element_type=F32)
    return out + jnp.dot(lo, bb, preferred_element_type=F32)


def _half_swiglu(x, g, wg_ref, wu_ref, wd_ref):
    h = _rms(x, g).astype(BF16)
    acc = None
    for s in range(D_FF // FFN_TF):
        cols = slice(s * FFN_TF, (s + 1) * FFN_TF)
        gate = jnp.dot(h, wg_ref[:, cols], preferred_element_type=F32)
        up = jnp.dot(h, wu_ref[:, cols], preferred_element_type=F32)
        act = (_silu(gate) * up).astype(BF16)
        part = jnp.dot(act, wd_ref[cols, :], preferred_element_type=F32)
        acc = part if acc is None else acc + part
    return 0.5 * acc


def _ffn_kernel(x_ref, g_ref, wg_ref, wu_ref, wd_ref, o_ref, *, sub):
    for r in range(x_ref.shape[0] // sub):
        rows = slice(r * sub, (r + 1) * sub)
        x = x_ref[rows, :]
        o_ref[rows, :] = x + _half_swiglu(x, g_ref[...], wg_ref, wu_ref, wd_ref)


def _mix_ffn_kernel(x_ref, oa_ref, ob_ref, oc_ref, woa_ref, wob_ref, woc_ref, g_ref, wg_ref, wu_ref,
                    wd_ref, gf_ref, o_ref, *, sub, final_norm):
    for r in range(x_ref.shape[0] // sub):
        rows = slice(r * sub, (r + 1) * sub)
        mix = jnp.dot(oa_ref[rows, :], woa_ref[...], preferred_element_type=F32)
        mix = mix + jnp.dot(ob_ref[rows, :], wob_ref[...], preferred_element_type=F32)
        mix = mix + jnp.dot(oc_ref[rows, :], woc_ref[...], preferred_element_type=F32)
        x2 = x_ref[rows, :] + mix
        y = x2 + _half_swiglu(x2, g_ref[...], wg_ref, wu_ref, wd_ref)
        if final_norm:
            y = _rms(y, gf_ref[...])
        o_ref[rows, :] = y


def _resident(shape, index_map):
    return pl.BlockSpec(shape, index_map, pipeline_mode=pl.Buffered(1))


def _ffn_weight_specs(layer):
    return [
        _resident((None, D_MODEL, D_FF), lambda i: (layer, 0, 0)),
        _resident((None, D_MODEL, D_FF), lambda i: (layer, 0, 0)),
        _resident((None, D_FF, D_MODEL), lambda i: (layer, 0, 0)),
    ]


def _ffn(x, g, wg, wu, wd, layer):
    n = x.shape[0]
    tm = min(FFN_TM, n)
    sub = min(FFN_SUB, tm)
    assert n % tm == 0 and tm % sub == 0, (n, tm)
    tok = pl.BlockSpec((tm, D_MODEL), lambda i: (i, 0))
    return pl.pallas_call(
        functools.partial(_ffn_kernel, sub=sub),
        grid=(n // tm,),
        in_specs=[tok, pl.BlockSpec((None, 1, D_MODEL), lambda i: (layer, 0, 0))] + _ffn_weight_specs(layer),
        out_specs=tok,
        out_shape=jax.ShapeDtypeStruct((n, D_MODEL), F32),
        compiler_params=_params(("parallel",)),
        name="ffn",
    )(x, g, wg, wu, wd)


def _mix_ffn(x, oa, ob, oc, w_out, g, wg, wu, wd, g_final, layer, final_norm):
    n = x.shape[0]
    tm = min(FFN_TM, n)
    sub = min(FFN_SUB, tm)
    assert n % tm == 0 and tm % sub == 0, (n, tm)
    tok = pl.BlockSpec((tm, D_MODEL), lambda i: (i, 0))
    in_specs = [
        tok,
        pl.BlockSpec((tm, MIX_A), lambda i: (i, 0)),
        pl.BlockSpec((tm, MIX_B), lambda i: (i, 0)),
        pl.BlockSpec((tm, MIX_C), lambda i: (i, 0)),
        _resident((None, MIX_A, D_MODEL), lambda i: (layer, 0, 0)),
        _resident((None, MIX_B, D_MODEL), lambda i: (layer, MIX_A // MIX_B, 0)),
        _resident((None, MIX_C, D_MODEL), lambda i: (layer, (MIX_A + MIX_B) // MIX_C, 0)),
        pl.BlockSpec((None, 1, D_MODEL), lambda i: (layer, 0, 0)),
    ] + _ffn_weight_specs(layer) + [pl.BlockSpec((1, D_MODEL), lambda i: (0, 0))]
    return pl.pallas_call(
        functools.partial(_mix_ffn_kernel, sub=sub, final_norm=final_norm),
        grid=(n // tm,),
        in_specs=in_specs,
        out_specs=tok,
        out_shape=jax.ShapeDtypeStruct((n, D_MODEL), F32),
        compiler_params=_params(("parallel",)),
        name="mix_ffn",
    )(x, oa, ob, oc, w_out, w_out, w_out, g, wg, wu, wd, g_final)


def _inproj_kernel(x_ref, g_ref, w_ref, convw_ref, cb0_ref, o_ref, tail_ref, xbuf, *, nseq, tl, sub):
    j = pl.program_id(1)
    nslab = QKV_A // LANES

    @pl.when(j == 0)
    def _():
        for si in range(nseq):
            for sl in range(nslab):
                xbuf[si * nslab + sl, 0:SUBLANES, :] = cb0_ref[si, :, sl * LANES:(sl + 1) * LANES]

    if sub <= tl:
        pieces = [[(si, r0, sub)] for si in range(nseq) for r0 in range(0, tl, sub)]
    else:
        per = sub // tl
        pieces = [[(si, 0, tl) for si in range(g0, g0 + per)] for g0 in range(0, nseq, per)]
    for piece in pieces:
        xs = [x_ref[si, r0:r0 + ln, :] for si, r0, ln in piece]
        h = _rms(xs[0] if len(xs) == 1 else jnp.concatenate(xs, axis=0), g_ref[...]).astype(BF16)
        u = jnp.dot(h, w_ref[...], preferred_element_type=F32)
        lo = 0
        for si, r0, ln in piece:
            o_ref[si, r0:r0 + ln, QKV_A:] = u[lo:lo + ln, QKV_A:]
            for sl in range(nslab):
                xbuf[si * nslab + sl, SUBLANES + r0:SUBLANES + r0 + ln, :] = u[lo:lo + ln, sl * LANES:(sl + 1) * LANES]
            for c0 in range(r0, r0 + ln, CHUNK):
                w0 = c0 + SUBLANES - (CONV_W - 1)
                for sl in range(nslab):
                    cols = slice(sl * LANES, (sl + 1) * LANES)
                    idx = si * nslab + sl
                    y = xbuf[idx, w0:w0 + CHUNK, :] * convw_ref[0:1, cols]
                    for tap in range(1, CONV_W):
                        y = y + xbuf[idx, w0 + tap:w0 + tap + CHUNK, :] * convw_ref[tap:tap + 1, cols]
                    y = _silu(y)
                    if sl < 2 * H_A:
                        y = y * lax.rsqrt(jnp.sum(y * y, axis=-1, keepdims=True) + EPS)
                        if sl < H_A:
                            y = y * DK_A ** -0.5
                    o_ref[si, c0:c0 + CHUNK, cols] = y
            lo += ln
    for idx in range(nseq * nslab):
        si, sl = divmod(idx, nslab)
        tail_ref[si, :, sl * LANES:(sl + 1) * LANES] = xbuf[idx, tl:tl + SUBLANES, :]
        xbuf[idx, 0:SUBLANES, :] = xbuf[idx, tl:tl + SUBLANES, :]


def _inproj(x, g, w, conv_w, cb0, layer):
    b, t, _ = x.shape
    tl = min(PROJ_TM, t)
    nseq = max(1, min(b, PROJ_TM // tl))
    sub = min(PROJ_SUB, nseq * tl)
    assert b % nseq == 0 and t % tl == 0 and (tl % sub == 0 or sub % tl == 0) and tl % CHUNK == 0, (b, t)
    return pl.pallas_call(
        functools.partial(_inproj_kernel, nseq=nseq, tl=tl, sub=sub),
        grid=(b // nseq, t // tl),
        in_specs=[pl.BlockSpec((nseq, tl, D_MODEL), lambda i, j: (i, j, 0)),
                  pl.BlockSpec((None, 1, D_MODEL), lambda i, j: (layer, 0, 0)),
                  _resident((None, D_MODEL, U_COLS), lambda i, j: (layer, 0, 0)),
                  pl.BlockSpec((CONV_W, QKV_A), lambda i, j: (0, 0)),
                  pl.BlockSpec((nseq, SUBLANES, QKV_A), lambda i, j: (i, 0, 0))],
        out_specs=[pl.BlockSpec((nseq, tl, U_COLS), lambda i, j: (i, j, 0)),
                   pl.BlockSpec((nseq, SUBLANES, QKV_A), lambda i, j: (i, 0, 0))],
        out_shape=[jax.ShapeDtypeStruct((b, t, U_COLS), F32),
                   jax.ShapeDtypeStruct((b, SUBLANES, QKV_A), F32)],
        scratch_shapes=[pltpu.VMEM((nseq * QKV_A // LANES, tl + SUBLANES, LANES), F32)],
        compiler_params=_params(("parallel", "arbitrary")),
        name="inproj",
    )(x, g, w, conv_w, cb0)


def _gdn_kernel(u_ref, z_ref, ba_ref, prm_ref, s0_ref, oa_ref, sout_ref,
                dbuf, bbuf, val_scr, qk_scr, kdec_scr, attn_scr, s_scr, *, nb, tb):
    t = pl.program_id(1)
    nchunks = tb // CHUNK

    @pl.when(t == 0)
    def _():
        s_scr[...] = s0_ref[...].reshape(s_scr.shape)

    row = lax.broadcasted_iota(jnp.int32, (tb, tb), 0)
    col = lax.broadcasted_iota(jnp.int32, (tb, tb), 1)
    shift = CHUNK.bit_length() - 1
    same_chunk = jnp.right_shift(row, shift) == jnp.right_shift(col, shift)
    tri_bd = jnp.where(same_chunk & (col <= row), 1.0, 0.0)
    for bi in range(nb):
        ba = ba_ref[bi]
        x = ba + prm_ref[1:2, :]
        softplus = jnp.maximum(x, 0.0) + jnp.log1p(jnp.exp(-jnp.abs(x)))
        g = prm_ref[0:1, :] * softplus
        bbuf[bi] = jax.nn.sigmoid(ba)
        dbuf[bi] = _cumsum_rows(g, tri_bd)

    r_i = lax.broadcasted_iota(jnp.int32, (CHUNK, CHUNK), 0)
    c_i = lax.broadcasted_iota(jnp.int32, (CHUNK, CHUNK), 1)
    tril = c_i <= r_i
    strict = c_i < r_i
    eye = c_i == r_i
    eye_f = jnp.where(eye, 1.0, 0.0)
    gnorm = prm_ref[2:3, :]

    def pidx(bi, c, h):
        return (bi * nchunks + c) * H_A + h

    problems = [(bi, c, h) for bi in range(nb) for c in range(nchunks) for h in range(H_A)]
    lmask, qk, a_low = {}, {}, {}
    for bi in range(nb):
        for c in range(nchunks):
            rows = slice(c * CHUNK, (c + 1) * CHUNK)
            dch = dbuf[bi, rows, :]
            beta = bbuf[bi, rows, :]
            e_d = jnp.exp(dch)
            e_dec = jnp.exp(dch[CHUNK - 1:CHUNK, :] - dch)
            for h in range(H_A):
                p = pidx(bi, c, h)
                hcols = slice(h * DV_A, (h + 1) * DV_A)
                q = u_ref[bi, rows, h * DK_A:(h + 1) * DK_A]
                k = u_ref[bi, rows, MIX_A + h * DK_A:MIX_A + (h + 1) * DK_A]
                v = u_ref[bi, rows, 2 * MIX_A + h * DV_A:2 * MIX_A + (h + 1) * DV_A]
                dcol = dch[:, 4 + h:5 + h]
                bcol = beta[:, h:h + 1]
                edcol = e_d[:, 4 + h:5 + h]
                drow = jnp.sum(jnp.where(eye, dcol, 0.0), axis=0, keepdims=True)
                diff = dcol - drow
                lmask[bi, c, h] = jnp.where(tril, jnp.exp(jnp.where(tril, diff, 0.0)), 0.0)
                kb = k * bcol
                qk[bi, c, h] = _dot_nt(jnp.concatenate([q, kb], axis=0), k)
                val_scr[bi, rows, hcols] = v * bcol
                qk_scr[p, 0:CHUNK, :] = (q * edcol).astype(BF16)
                qk_scr[p, CHUNK:2 * CHUNK, :] = (kb * edcol).astype(BF16)
                kdec_scr[bi, rows, hcols] = (k * e_dec[:, 4 + h:5 + h]).astype(BF16)
    for pr in problems:
        attn_scr[pidx(*pr)] = (qk[pr][0:CHUNK] * lmask[pr]).astype(BF16)
        a_low[pr] = jnp.where(strict, qk[pr][CHUNK:2 * CHUNK] * lmask[pr], 0.0)
    pw = a_low
    inv = {pr: eye_f - a_low[pr] for pr in problems}
    for _ in range(5):
        pw = {pr: _dot(pw[pr], pw[pr]) for pr in problems}
        inv = {pr: inv[pr] + _dot(inv[pr], pw[pr]) for pr in problems}
    for bi, c, h in problems:
        p = pidx(bi, c, h)
        rows = slice(c * CHUNK, (c + 1) * CHUNK)
        hcols = slice(h * DV_A, (h + 1) * DV_A)
        rhs = jnp.concatenate([val_scr[bi, rows, hcols].astype(BF16), qk_scr[p, CHUNK:2 * CHUNK, :]], axis=1)
        sol = _dot(inv[bi, c, h], rhs)
        val_scr[bi, rows, hcols] = sol[:, :DV_A]
        qk_scr[p, CHUNK:2 * CHUNK, :] = sol[:, DV_A:].astype(BF16)

    chains = [(bi, h) for bi in range(nb) for h in range(H_A)]
    state = {ch: s_scr[ch[0] * H_A + ch[1]] for ch in chains}
    for c in range(nchunks):
        rows = slice(c * CHUNK, (c + 1) * CHUNK)
        qs = {(bi, h): _dot(qk_scr[pidx(bi, c, h)], state[bi, h]) for bi, h in chains}
        v_new = {(bi, h): val_scr[bi, rows, h * DV_A:(h + 1) * DV_A] - qs[bi, h][CHUNK:2 * CHUNK]
                 for bi, h in chains}
        o_in = {(bi, h): _dot(attn_scr[pidx(bi, c, h)], v_new[bi, h]) for bi, h in chains}
        ds = {(bi, h): _dot_tn(kdec_scr[bi, rows, h * DK_A:(h + 1) * DK_A], v_new[bi, h]) for bi, h in chains}
        for bi, h in chains:
            hcols = slice(h * DV_A, (h + 1) * DV_A)
            e_last = jnp.exp(dbuf[bi, (c + 1) * CHUNK - 1:(c + 1) * CHUNK, 4 + h:5 + h])
            o = qs[bi, h][0:CHUNK] + o_in[bi, h]
            state[bi, h] = state[bi, h] * e_last + ds[bi, h]
            oa_ref[bi, rows, hcols] = (_rms(o, gnorm) * _silu(z_ref[bi, rows, hcols])).astype(BF16)
    for bi, h in chains:
        s_scr[bi * H_A + h] = state[bi, h]

    @pl.when(t == pl.num_programs(1) - 1)
    def _():
        sout_ref[...] = s_scr[...].reshape(sout_ref.shape)


def _cumsum_rows(g, tri_bd):
    hi = g.astype(BF16)
    r1 = g - hi.astype(F32)
    mid = r1.astype(BF16)
    lo = (r1 - mid.astype(F32)).astype(BF16)
    tb = tri_bd.astype(BF16)
    out = jnp.dot(tb, hi, preferred_element_type=F32)
    out = out + jnp.dot(tb, mid, preferred_element_type=F32)
    return out + jnp.dot(tb, lo, preferred_element_type=F32)


def _gdn(u, prm, s0, tb):
    b, t, _ = u.shape
    nb = max(1, min(b, GDN_PROBLEMS // (tb // CHUNK * H_A)))
    assert b % nb == 0, (b, nb)
    nprob = nb * (tb // CHUNK) * H_A
    return pl.pallas_call(
        functools.partial(_gdn_kernel, nb=nb, tb=tb),
        grid=(b // nb, t // tb),
        in_specs=[pl.BlockSpec((nb, tb, QKV_A), lambda i, j: (i, j, 0)),
                  pl.BlockSpec((nb, tb, MIX_A), lambda i, j: (i, j, COL_Z)),
                  pl.BlockSpec((nb, tb, LANES), lambda i, j: (i, j, COL_BA)),
                  pl.BlockSpec((SUBLANES, LANES), lambda i, j: (0, 0)),
                  pl.BlockSpec((nb, H_A, DK_A, DV_A), lambda i, j: (i, 0, 0, 0))],
        out_specs=[pl.BlockSpec((nb, tb, MIX_A), lambda i, j: (i, j, 0)),
                   pl.BlockSpec((nb, H_A, DK_A, DV_A), lambda i, j: (i, 0, 0, 0))],
        out_shape=[jax.ShapeDtypeStruct((b, t, MIX_A), BF16),
                   jax.ShapeDtypeStruct((b, H_A, DK_A, DV_A), F32)],
        scratch_shapes=[pltpu.VMEM((nb, tb, LANES), F32), pltpu.VMEM((nb, tb, LANES), F32),
                        pltpu.VMEM((nb, tb, MIX_A), F32),
                        pltpu.VMEM((nprob, 2 * CHUNK, DK_A), BF16),
                        pltpu.VMEM((nb, tb, H_A * DK_A), BF16),
                        pltpu.VMEM((nprob, CHUNK, CHUNK), BF16),
                        pltpu.VMEM((nb * H_A, DK_A, DV_A), F32)],
        compiler_params=_params(("parallel", "arbitrary")),
        name="gdn",
    )(u, u, u, prm, s0)


def _band_kernel(q_ref, k_ref, v_ref, bias_ref, ck_ref, cv_ref, o_ref, kbuf, vbuf, *, nb, qb, t_len, n_invalid):
    t = pl.program_id(1)

    @pl.when(t == 0)
    def _():
        for bi in range(nb):
            kbuf[bi, 0:BAND, :] = ck_ref[bi].astype(BF16)
            vbuf[bi, 0:BAND, :] = cv_ref[bi].astype(BF16)
            kbuf[bi, BAND:BAND + t_len, :] = k_ref[bi].astype(BF16)
            vbuf[bi, BAND:BAND + t_len, :] = v_ref[bi].astype(BF16)

    start = pl.multiple_of(t * qb, qb)
    lane = lax.broadcasted_iota(jnp.int32, (1, MIX_B), 1)
    if n_invalid:
        kpos = lax.broadcasted_iota(jnp.int32, (qb, qb + BAND), 1) + start
        valid = kpos >= n_invalid
    for bi in range(nb):
        kw = kbuf[bi, pl.ds(start, qb + BAND), :]
        vw = vbuf[bi, pl.ds(start, qb + BAND), :]
        q = q_ref[bi] * DH_B ** -0.5
        acc = jnp.zeros((qb, MIX_B), F32)
        for h in range(H_B):
            mh = (lane >= h * DH_B) & (lane < (h + 1) * DH_B)
            s = _dot_nt(jnp.where(mh, q, 0.0), kw) + bias_ref[h]
            if n_invalid:
                s = jnp.where(valid, s, NEG_INF)
            p = jnp.exp(s - jnp.max(s, axis=-1, keepdims=True))
            o = _dot(p, vw) / jnp.sum(p, axis=-1, keepdims=True)
            acc = acc + jnp.where(mh, o, 0.0)
        o_ref[bi] = acc.astype(BF16)


def _band(u, bias, ck, cv, qb, n_invalid):
    b, t, _ = u.shape
    nb = max(1, min(b, MIX_ROWS // qb))
    assert b % nb == 0 and ck.shape[0] == b, (b, nb)
    return pl.pallas_call(
        functools.partial(_band_kernel, nb=nb, qb=qb, t_len=t, n_invalid=n_invalid),
        grid=(b // nb, t // qb),
        in_specs=[pl.BlockSpec((nb, qb, MIX_B), lambda i, j: (i, j, COL_B)),
                  pl.BlockSpec((nb, t, MIX_B), lambda i, j: (i, 0, COL_B + 1)),
                  pl.BlockSpec((nb, t, MIX_B), lambda i, j: (i, 0, COL_B + 2)),
                  pl.BlockSpec((H_B, qb, qb + BAND), lambda i, j: (0, 0, 0)),
                  pl.BlockSpec((nb, BAND, MIX_B), lambda i, j: (i, 0, 0)),
                  pl.BlockSpec((nb, BAND, MIX_B), lambda i, j: (i, 0, 0))],
        out_specs=pl.BlockSpec((nb, qb, MIX_B), lambda i, j: (i, j, 0)),
        out_shape=jax.ShapeDtypeStruct((b, t, MIX_B), BF16),
        scratch_shapes=[pltpu.VMEM((nb, BAND + t, MIX_B), BF16), pltpu.VMEM((nb, BAND + t, MIX_B), BF16)],
        compiler_params=_params(("parallel", "arbitrary")),
        name="band",
    )(u, u, u, bias, ck, cv)


def _band_bias_table(rel_bias, qb):
    rb = rel_bias.astype(F32)
    nh = rb.shape[0]
    w = qb + BAND
    p = w + qb - 1
    rel_lo, rel_hi = -(qb - 1) - BAND, qb - 1
    mid_hi = min(rel_hi, REL_CLIP)
    f = jnp.concatenate([jnp.broadcast_to(rb[:, :1], (nh, -REL_CLIP - rel_lo)),
                         rb[:, :mid_hi + REL_CLIP + 1],
                         jnp.broadcast_to(rb[:, -1:], (nh, rel_hi - mid_hi))], axis=1)
    rolled = jnp.tile(jnp.pad(f, ((0, 0), (0, 1))), (1, qb))[:, :qb * p].reshape(nh, qb, p)
    tab = rolled[:, :, qb - 1:qb - 1 + w]
    ci = np.arange(qb)[:, None] // CHUNK
    cj = np.arange(w)[None, :] // CHUNK
    return jnp.where((cj >= ci) & (cj <= ci + BAND_CHUNKS), tab, NEG_INF)


def _ret_kernel(q_ref, k_ref, v_ref, g_ref, rope_ref, pow_ref, dmask_ref, sq_ref, r0_ref, o_ref, rout_ref,
                r_scr, *, nb, tb):
    t = pl.program_id(1)

    @pl.when(t == 0)
    def _():
        r_scr[...] = r0_ref[...]

    def rope(x):
        return (x * rope_ref[0] + pltpu.roll(x, MIX_C - DK_C // 2, 1) * rope_ref[1]
                + pltpu.roll(x, DK_C // 2, 1) * rope_ref[2])

    lane = lax.broadcasted_iota(jnp.int32, (1, MIX_C), 1)
    avg = sq_ref[2]
    for bi in range(nb):
        q = rope(q_ref[bi])
        k = rope(k_ref[bi]) * DK_C ** -0.5
        v = v_ref[bi]
        r = r_scr[bi]
        acc = _dot(q * pow_ref[0], r)
        for h in range(H_C):
            mh = (lane >= h * DK_C) & (lane < (h + 1) * DK_C)
            s = _dot_nt(jnp.where(mh, q, 0.0), k) * dmask_ref[h]
            acc = acc + jnp.where(mh, _dot(s, v), 0.0)
        r_scr[bi] = r * sq_ref[0] + sq_ref[1] * _dot_tn(k * pow_ref[1], v)
        xc = acc - _dot_exact_rhs(acc, avg)
        var = _dot_exact_rhs(xc * xc, avg)
        o_ref[bi] = ((xc * lax.rsqrt(var + EPS)) * _silu(g_ref[bi])).astype(BF16)

    @pl.when(t == pl.num_programs(1) - 1)
    def _():
        rout_ref[...] = r_scr[...]


def _ret(u, rope_tab, pow_tab, dmask, sq_tab, r0, tb):
    b, t, _ = u.shape
    nb = max(1, min(b, MIX_ROWS // tb))
    assert b % nb == 0, (b, nb)
    ublk = lambda c: pl.BlockSpec((nb, tb, MIX_C), lambda i, j: (i, j, COL_B + c))
    return pl.pallas_call(
        functools.partial(_ret_kernel, nb=nb, tb=tb),
        grid=(b // nb, t // tb),
        in_specs=[ublk(3), ublk(4), ublk(5), ublk(6),
                  pl.BlockSpec((3, tb, MIX_C), lambda i, j: (0, j, 0)),
                  pl.BlockSpec((2, tb, MIX_C), lambda i, j: (0, 0, 0)),
                  pl.BlockSpec((H_C, tb, tb), lambda i, j: (0, 0, 0)),
                  pl.BlockSpec((3, MIX_C, MIX_C), lambda i, j: (0, 0, 0)),
                  pl.BlockSpec((nb, MIX_C, MIX_C), lambda i, j: (i, 0, 0))],
        out_specs=[pl.BlockSpec((nb, tb, MIX_C), lambda i, j: (i, j, 0)),
                   pl.BlockSpec((nb, MIX_C, MIX_C), lambda i, j: (i, 0, 0))],
        out_shape=[jax.ShapeDtypeStruct((b, t, MIX_C), BF16),
                   jax.ShapeDtypeStruct((b, MIX_C, MIX_C), F32)],
        scratch_shapes=[pltpu.VMEM((nb, MIX_C, MIX_C), F32)],
        compiler_params=_params(("parallel", "arbitrary")),
        name="ret",
    )(u, u, u, u, rope_tab, pow_tab, dmask, sq_tab, r0)


def _ret_tables(t_len, pos0, tb):
    half = DK_C // 2
    inv_freq = np.exp(-math.log(ROPE_BASE) * np.arange(half) / half)
    ang = (pos0 + np.arange(t_len))[:, None] * inv_freq[None, :]
    cos, sin = np.cos(ang), np.sin(ang)
    zero = np.zeros_like(sin)
    per_head = lambda a, b_: np.tile(np.concatenate([a, b_], axis=1), (1, H_C))
    rope_tab = np.stack([per_head(cos, cos), per_head(-sin, zero), per_head(zero, sin)])
    lg = np.log1p(-np.exp2(-5.0 - np.arange(H_C)))
    lg_lane = np.repeat(lg, DK_C)
    n = np.arange(tb, dtype=np.float64)
    pow_tab = np.stack([np.exp(lg_lane[None, :] * (n + 1.0)[:, None]),
                        np.exp(lg_lane[None, :] * (tb - 1.0 - n)[:, None])])
    diff = n[:, None] - n[None, :]
    causal = diff >= 0
    dmask = np.where(causal, np.exp(lg[:, None, None] * np.where(causal, diff, 0.0)), 0.0)
    head = np.arange(MIX_C) // DK_C
    same = (head[:, None] == head[None, :]).astype(np.float64)
    carry = np.broadcast_to(np.exp(lg_lane * tb)[:, None], (MIX_C, MIX_C))
    sq_tab = np.stack([carry, same, same / DV_C])
    return tuple(jnp.asarray(a, F32) for a in (rope_tab, pow_tab, dmask, sq_tab))


def _relayout_w_in(w_in):
    n_small = 2 * H_A
    a_end = QKV_A + MIX_A
    pad = jnp.zeros(w_in.shape[:2] + (LANES - n_small,), w_in.dtype)
    return jnp.concatenate([w_in[..., :a_end], w_in[..., a_end + n_small:],
                            w_in[..., a_end:a_end + n_small], pad], axis=-1).astype(BF16)


def _block_diag(r):
    b = r.shape[0]
    eye = jnp.eye(H_C, dtype=r.dtype)
    return jnp.einsum('bhkv,hg->bhkgv', r, eye).reshape(b, H_C * DK_C, H_C * DV_C)


def _diag_blocks(r):
    b = r.shape[0]
    r = r.reshape(b, H_C, DK_C, H_C, DV_C)
    return jnp.stack([r[:, h, :, h, :] for h in range(H_C)], axis=1)


def _run(x, pos0, caches, weights):
    (norm_ffn1, wg1, wu1, wd1, norm_mix, w_in, conv_w, a_log, dt_bias, gdn_norm, rel_bias, w_out,
     norm_ffn2, wg2, wu2, wd2, norm_final) = weights
    b, t, _ = x.shape
    n = b * t
    tb = min(MIX_TB, t)
    assert t % tb == 0 and tb % CHUNK == 0, (t, tb)
    rope_tab, pow_tab, dmask, sq_tab = _ret_tables(t, pos0, tb)
    gf = norm_final.reshape(1, D_MODEL)
    norm_ffn1, norm_mix, norm_ffn2 = (g.reshape(DEPTH, 1, D_MODEL) for g in (norm_ffn1, norm_mix, norm_ffn2))
    xf = x.reshape(n, D_MODEL)
    new = []
    for l in range(DEPTH):
        if caches is None:
            s_gdn0 = jnp.zeros((b, H_A, DK_A, DV_A), F32)
            cb0 = jnp.zeros((b, SUBLANES, QKV_A), F32)
            r0 = jnp.zeros((b, MIX_C, MIX_C), F32)
            ck = cv = jnp.zeros((b, BAND, MIX_B), F32)
            n_invalid = BAND
        else:
            state_gdn, state_conv, cache_k, cache_v, state_ret = caches
            s_gdn0 = state_gdn[l]
            cb0 = jnp.pad(state_conv[l], ((0, 0), (SUBLANES - (CONV_W - 1), 0), (0, 0)))
            r0 = _block_diag(state_ret[l])
            ck = cache_k[l].reshape(b, BAND, MIX_B)
            cv = cache_v[l].reshape(b, BAND, MIX_B)
            n_invalid = 0
        xf = _ffn(xf, norm_ffn1, wg1, wu1, wd1, l)
        u, conv_tail = _inproj(xf.reshape(b, t, D_MODEL), norm_mix, w_in, conv_w[l], cb0, l)
        decay_lanes = lambda v: jnp.pad(v.astype(F32), (H_A, LANES - 2 * H_A))
        prm = jnp.concatenate([decay_lanes(-jnp.exp(a_log[l].astype(F32)))[None],
                               decay_lanes(dt_bias[l])[None], gdn_norm[l].astype(F32)[None],
                               jnp.zeros((SUBLANES - 3, LANES), F32)], axis=0)
        oa, s_gdn = _gdn(u, prm, s_gdn0, tb)
        ob = _band(u, _band_bias_table(rel_bias[l], tb), ck, cv, tb, n_invalid)
        oc, r_new = _ret(u, rope_tab, pow_tab, dmask, sq_tab, r0, tb)
        xf = _mix_ffn(xf, oa.reshape(n, MIX_A), ob.reshape(n, MIX_B), oc.reshape(n, MIX_C), w_out,
                      norm_ffn2, wg2, wu2, wd2, gf, l, l == DEPTH - 1)
        keep = min(BAND, t)
        conv_new = conv_tail[:, SUBLANES - (CONV_W - 1):, :]
        kb_new = u[:, t - keep:, (COL_B + 1) * MIX_B:(COL_B + 2) * MIX_B].reshape(b, keep, H_B, DH_B)
        vb_new = u[:, t - keep:, (COL_B + 2) * MIX_B:(COL_B + 3) * MIX_B].reshape(b, keep, H_B, DH_B)
        new.append((s_gdn, conv_new, kb_new, vb_new, _diag_blocks(r_new)))
    g_, c_, k_, v_, r_ = zip(*new)
    return (xf.reshape(b, t, D_MODEL), jnp.stack(g_), jnp.stack(c_), jnp.stack(k_), jnp.stack(v_),
            jnp.stack(r_))


def kernel(x_prompt, x_sample, state_gdn, state_conv, cache_band_k, cache_band_v, state_ret, norm_ffn1, w_ffn1_gate, w_ffn1_up, w_ffn1_down, norm_mix, w_in, conv_w, a_log, dt_bias, gdn_norm, rel_bias, w_out, norm_ffn2, w_ffn2_gate, w_ffn2_up, w_ffn2_down, norm_final):
    past_len = 1024
    weights = (norm_ffn1, w_ffn1_gate.astype(BF16), w_ffn1_up.astype(BF16), w_ffn1_down.astype(BF16),
               norm_mix, _relayout_w_in(w_in), conv_w, a_log, dt_bias, gdn_norm, rel_bias,
               w_out.astype(BF16), norm_ffn2, w_ffn2_gate.astype(BF16), w_ffn2_up.astype(BF16),
               w_ffn2_down.astype(BF16), norm_final)
    p = _run(x_prompt, 0, None, weights)
    s = _run(x_sample, past_len, (state_gdn, state_conv, cache_band_k, cache_band_v, state_ret), weights)
    return (p[0], s[0]) + p[1:] + s[1:]
```

```python
import functools
import math

import jax
import jax.numpy as jnp
import numpy as np
from jax import lax
from jax.experimental import pallas as pl
from jax.experimental.pallas import tpu as pltpu

F32 = jnp.float32
BF16 = jnp.bfloat16

D_MODEL = 1024
DEPTH = 2
CHUNK = 64
H_A, DK_A, DV_A, CONV_W = 4, 128, 128, 4
H_B, DH_B, BAND_CHUNKS, REL_CLIP = 4, 64, 8, 128
H_C, DK_C, DV_C = 4, 64, 64
ROPE_BASE = 10000.0
D_FF = 2816
EPS = 1e-6
NEG_INF = -1e30
BAND = BAND_CHUNKS * CHUNK
QKV_A = 3 * H_A * DK_A
MIX_A = H_A * DV_A
MIX_B = H_B * DH_B
MIX_C = H_C * DV_C
LANES = 128
SUBLANES = 8
U_COLS = QKV_A + MIX_A + 3 * MIX_B + 4 * MIX_C + LANES
COL_Z = QKV_A // MIX_A
COL_B = (QKV_A + MIX_A) // MIX_B
COL_BA = (U_COLS - LANES) // LANES
VMEM_LIMIT = 56 * 1024 * 1024

FFN_TM = 1024
FFN_SUB = 512
FFN_TF = 256
PROJ_TM = 512
PROJ_SUB = 256
MIX_TB = 256
GDN_PROBLEMS = 32
MIX_ROWS = 512


def _params(sem):
    return pltpu.CompilerParams(dimension_semantics=sem, vmem_limit_bytes=VMEM_LIMIT)


def _rms(x, g):
    return (x * lax.rsqrt(jnp.mean(x * x, axis=-1, keepdims=True) + EPS)) * g


def _silu(x):
    return x * jax.nn.sigmoid(x)


def _dot(a, b):
    return jnp.dot(a.astype(BF16), b.astype(BF16), preferred_element_type=F32)


def _dot_nt(a, b):
    return lax.dot_general(a.astype(BF16), b.astype(BF16), (((1,), (1,)), ((), ())),
                           preferred_element_type=F32)


def _dot_tn(a, b):
    return lax.dot_general(a.astype(BF16), b.astype(BF16), (((0,), (0,)), ((), ())),
                           preferred_element_type=F32)


def _dot_exact_rhs(a, b):
    hi = a.astype(BF16)
    r1 = a - hi.astype(F32)
    mid = r1.astype(BF16)
    lo = (r1 - mid.astype(F32)).astype(BF16)
    bb = b.astype(BF16)
    out = jnp.dot(hi, bb, preferred_element_type=F32)
    out = out + jnp.dot(mid, bb, preferred_element_type=F32)
    return out + jnp.dot(lo, bb, preferred_element_type=F32)


def _half_swiglu(x, g, wg_ref, wu_ref, wd_ref):
    h = _rms(x, g).astype(BF16)
    acc = None
    for s in range(D_FF // FFN_TF):
        cols = slice(s * FFN_TF, (s + 1) * FFN_TF)
        gate = jnp.dot(h, wg_ref[:, cols], preferred_element_type=F32)
        up = jnp.dot(h, wu_ref[:, cols], preferred_element_type=F32)
        act = (_silu(gate) * up).astype(BF16)
        part = jnp.dot(act, wd_ref[cols, :], preferred_element_type=F32)
        acc = part if acc is None else acc + part
    return 0.5 * acc


def _ffn_kernel(x_ref, g_ref, wg_ref, wu_ref, wd_ref, o_ref, *, sub):
    for r in range(x_ref.shape[0] // sub):
        rows = slice(r * sub, (r + 1) * sub)
        x = x_ref[rows, :]
        o_ref[rows, :] = x + _half_swiglu(x, g_ref[...], wg_ref, wu_ref, wd_ref)


def _mix_ffn_kernel(x_ref, oa_ref, ob_ref, oc_ref, woa_ref, wob_ref, woc_ref, g_ref, wg_ref, wu_ref,
                    wd_ref, gf_ref, o_ref, *, sub, final_norm):
    for r in range(x_ref.shape[0] // sub):
        rows = slice(r * sub, (r + 1) * sub)
        mix = jnp.dot(oa_ref[rows, :], woa_ref[...], preferred_element_type=F32)
        mix = mix + jnp.dot(ob_ref[rows, :], wob_ref[...], preferred_element_type=F32)
        mix = mix + jnp.dot(oc_ref[rows, :], woc_ref[...], preferred_element_type=F32)
        x2 = x_ref[rows, :] + mix
        y = x2 + _half_swiglu(x2, g_ref[...], wg_ref, wu_ref, wd_ref)
        if final_norm:
            y = _rms(y, gf_ref[...])
        o_ref[rows, :] = y


def _resident(shape, index_map):
    return pl.BlockSpec(shape, index_map, pipeline_mode=pl.Buffered(1))


def _ffn_weight_specs(layer):
    return [
        _resident((None, D_MODEL, D_FF), lambda i: (layer, 0, 0)),
        _resident((None, D_MODEL, D_FF), lambda i: (layer, 0, 0)),
        _resident((None, D_FF, D_MODEL), lambda i: (layer, 0, 0)),
    ]


def _ffn(x, g, wg, wu, wd, layer):
    n = x.shape[0]
    tm = min(FFN_TM, n)
    sub = min(FFN_SUB, tm)
    assert n % tm == 0 and tm % sub == 0, (n, tm)
    tok = pl.BlockSpec((tm, D_MODEL), lambda i: (i, 0))
    return pl.pallas_call(
        functools.partial(_ffn_kernel, sub=sub),
        grid=(n // tm,),
        in_specs=[tok, pl.BlockSpec((None, 1, D_MODEL), lambda i: (layer, 0, 0))] + _ffn_weight_specs(layer),
        out_specs=tok,
        out_shape=jax.ShapeDtypeStruct((n, D_MODEL), F32),
        compiler_params=_params(("parallel",)),
        name="ffn",
    )(x, g, wg, wu, wd)


def _mix_ffn(x, oa, ob, oc, w_out, g, wg, wu, wd, g_final, layer, final_norm):
    n = x.shape[0]
    tm = min(FFN_TM, n)
    sub = min(FFN_SUB, tm)
    assert n % tm == 0 and tm % sub == 0, (n, tm)
    tok = pl.BlockSpec((tm, D_MODEL), lambda i: (i, 0))
    in_specs = [
        tok,
        pl.BlockSpec((tm, MIX_A), lambda i: (i, 0)),
        pl.BlockSpec((tm, MIX_B), lambda i: (i, 0)),
        pl.BlockSpec((tm, MIX_C), lambda i: (i, 0)),
        _resident((None, MIX_A, D_MODEL), lambda i: (layer, 0, 0)),
        _resident((None, MIX_B, D_MODEL), lambda i: (layer, MIX_A // MIX_B, 0)),
        _resident((None, MIX_C, D_MODEL), lambda i: (layer, (MIX_A + MIX_B) // MIX_C, 0)),
        pl.BlockSpec((None, 1, D_MODEL), lambda i: (layer, 0, 0)),
    ] + _ffn_weight_specs(layer) + [pl.BlockSpec((1, D_MODEL), lambda i: (0, 0))]
    return pl.pallas_call(
        functools.partial(_mix_ffn_kernel, sub=sub, final_norm=final_norm),
        grid=(n // tm,),
        in_specs=in_specs,
        out_specs=tok,
        out_shape=jax.ShapeDtypeStruct((n, D_MODEL), F32),
        compiler_params=_params(("parallel",)),
        name="mix_ffn",
    )(x, oa, ob, oc, w_out, w_out, w_out, g, wg, wu, wd, g_final)


def _inproj_kernel(x_ref, g_ref, w_ref, convw_ref, cb0_ref, o_ref, tail_ref, xbuf, *, nseq, tl, sub):
    j = pl.program_id(1)
    nslab = QKV_A // LANES

    @pl.when(j == 0)
    def _():
        for si in range(nseq):
            for sl in range(nslab):
                xbuf[si * nslab + sl, 0:SUBLANES, :] = cb0_ref[si, :, sl * LANES:(sl + 1) * LANES]

    if sub <= tl:
        pieces = [[(si, r0, sub)] for si in range(nseq) for r0 in range(0, tl, sub)]
    else:
        per = sub // tl
        pieces = [[(si, 0, tl) for si in range(g0, g0 + per)] for g0 in range(0, nseq, per)]
    for piece in pieces:
        xs = [x_ref[si, r0:r0 + ln, :] for si, r0, ln in piece]
        h = _rms(xs[0] if len(xs) == 1 else jnp.concatenate(xs, axis=0), g_ref[...]).astype(BF16)
        u = jnp.dot(h, w_ref[...], preferred_element_type=F32)
        lo = 0
        for si, r0, ln in piece:
            o_ref[si, r0:r0 + ln, QKV_A:] = u[lo:lo + ln, QKV_A:]
            for sl in range(nslab):
                xbuf[si * nslab + sl, SUBLANES + r0:SUBLANES + r0 + ln, :] = u[lo:lo + ln, sl * LANES:(sl + 1) * LANES]
            for c0 in range(r0, r0 + ln, CHUNK):
                w0 = c0 + SUBLANES - (CONV_W - 1)
                for sl in range(nslab):
                    cols = slice(sl * LANES, (sl + 1) * LANES)
                    idx = si * nslab + sl
                    y = xbuf[idx, w0:w0 + CHUNK, :] * convw_ref[0:1, cols]
                    for tap in range(1, CONV_W):
                        y = y + xbuf[idx, w0 + tap:w0 + tap + CHUNK, :] * convw_ref[tap:tap + 1, cols]
                    y = _silu(y)
                    if sl < 2 * H_A:
                        y = y * lax.rsqrt(jnp.sum(y * y, axis=-1, keepdims=True) + EPS)
                        if sl < H_A:
                            y = y * DK_A ** -0.5
                    o_ref[si, c0:c0 + CHUNK, cols] = y
            lo += ln
    for idx in range(nseq * nslab):
        si, sl = divmod(idx, nslab)
        tail_ref[si, :, sl * LANES:(sl + 1) * LANES] = xbuf[idx, tl:tl + SUBLANES, :]
        xbuf[idx, 0:SUBLANES, :] = xbuf[idx, tl:tl + SUBLANES, :]


def _inproj(x, g, w, conv_w, cb0, layer):
    b, t, _ = x.shape
    tl = min(PROJ_TM, t)
    nseq = max(1, min(b, PROJ_TM // tl))
    sub = min(PROJ_SUB, nseq * tl)
    assert b % nseq == 0 and t % tl == 0 and (tl % sub == 0 or sub % tl == 0) and tl % CHUNK == 0, (b, t)
    return pl.pallas_call(
        functools.partial(_inproj_kernel, nseq=nseq, tl=tl, sub=sub),
        grid=(b // nseq, t // tl),
        in_specs=[pl.BlockSpec((nseq, tl, D_MODEL), lambda i, j: (i, j, 0)),
                  pl.BlockSpec((None, 1, D_MODEL), lambda i, j: (layer, 0, 0)),
                  _resident((None, D_MODEL, U_COLS), lambda i, j: (layer, 0, 0)),
                  pl.BlockSpec((CONV_W, QKV_A), lambda i, j: (0, 0)),
                  pl.BlockSpec((nseq, SUBLANES, QKV_A), lambda i, j: (i, 0, 0))],
        out_specs=[pl.BlockSpec((nseq, tl, U_COLS), lambda i, j: (i, j, 0)),
                   pl.BlockSpec((nseq, SUBLANES, QKV_A), lambda i, j: (i, 0, 0))],
        out_shape=[jax.ShapeDtypeStruct((b, t, U_COLS), F32),
                   jax.ShapeDtypeStruct((b, SUBLANES, QKV_A), F32)],
        scratch_shapes=[pltpu.VMEM((nseq * QKV_A // LANES, tl + SUBLANES, LANES), F32)],
        compiler_params=_params(("parallel", "arbitrary")),
        name="inproj",
    )(x, g, w, conv_w, cb0)


def _gdn_kernel(u_ref, z_ref, ba_ref, prm_ref, s0_ref, oa_ref, sout_ref,
                dbuf, bbuf, val_scr, qk_scr, kdec_scr, attn_scr, s_scr, *, nb, tb):
    t = pl.program_id(1)
    nchunks = tb // CHUNK

    @pl.when(t == 0)
    def _():
        s_scr[...] = s0_ref[...].reshape(s_scr.shape)

    row = lax.broadcasted_iota(jnp.int32, (tb, tb), 0)
    col = lax.broadcasted_iota(jnp.int32, (tb, tb), 1)
    shift = CHUNK.bit_length() - 1
    same_chunk = jnp.right_shift(row, shift) == jnp.right_shift(col, shift)
    tri_bd = jnp.where(same_chunk & (col <= row), 1.0, 0.0)
    for bi in range(nb):
        ba = ba_ref[bi]
        x = ba + prm_ref[1:2, :]
        softplus = jnp.maximum(x, 0.0) + jnp.log1p(jnp.exp(-jnp.abs(x)))
        g = prm_ref[0:1, :] * softplus
        bbuf[bi] = jax.nn.sigmoid(ba)
        dbuf[bi] = _cumsum_rows(g, tri_bd)

    r_i = lax.broadcasted_iota(jnp.int32, (CHUNK, CHUNK), 0)
    c_i = lax.broadcasted_iota(jnp.int32, (CHUNK, CHUNK), 1)
    tril = c_i <= r_i
    strict = c_i < r_i
    eye = c_i == r_i
    eye_f = jnp.where(eye, 1.0, 0.0)
    gnorm = prm_ref[2:3, :]

    def pidx(bi, c, h):
        return (bi * nchunks + c) * H_A + h

    problems = [(bi, c, h) for bi in range(nb) for c in range(nchunks) for h in range(H_A)]
    lmask, qk, a_low = {}, {}, {}
    for bi in range(nb):
        for c in range(nchunks):
            rows = slice(c * CHUNK, (c + 1) * CHUNK)
            dch = dbuf[bi, rows, :]
            beta = bbuf[bi, rows, :]
            e_d = jnp.exp(dch)
            e_dec = jnp.exp(dch[CHUNK - 1:CHUNK, :] - dch)
            for h in range(H_A):
                p = pidx(bi, c, h)
                hcols = slice(h * DV_A, (h + 1) * DV_A)
                q = u_ref[bi, rows, h * DK_A:(h + 1) * DK_A]
                k = u_ref[bi, rows, MIX_A + h * DK_A:MIX_A + (h + 1) * DK_A]
                v = u_ref[bi, rows, 2 * MIX_A + h * DV_A:2 * MIX_A + (h + 1) * DV_A]
                dcol = dch[:, 4 + h:5 + h]
                bcol = beta[:, h:h + 1]
                edcol = e_d[:, 4 + h:5 + h]
                drow = jnp.sum(jnp.where(eye, dcol, 0.0), axis=0, keepdims=True)
                diff = dcol - drow
                lmask[bi, c, h] = jnp.where(tril, jnp.exp(jnp.where(tril, diff, 0.0)), 0.0)
                kb = k * bcol
                qk[bi, c, h] = _dot_nt(jnp.concatenate([q, kb], axis=0), k)
                val_scr[bi, rows, hcols] = v * bcol
                qk_scr[p, 0:CHUNK, :] = (q * edcol).astype(BF16)
                qk_scr[p, CHUNK:2 * CHUNK, :] = (kb * edcol).astype(BF16)
                kdec_scr[bi, rows, hcols] = (k * e_dec[:, 4 + h:5 + h]).astype(BF16)
    for pr in problems:
        attn_scr[pidx(*pr)] = (qk[pr][0:CHUNK] * lmask[pr]).astype(BF16)
        a_low[pr] = jnp.where(strict, qk[pr][CHUNK:2 * CHUNK] * lmask[pr], 0.0)
    pw = a_low
    inv = {pr: eye_f - a_low[pr] for pr in problems}
    for _ in range(5):
        pw = {pr: _dot(pw[pr], pw[pr]) for pr in problems}
        inv = {pr: inv[pr] + _dot(inv[pr], pw[pr]) for pr in problems}
    for bi, c, h in problems:
        p = pidx(bi, c, h)
        rows = slice(c * CHUNK, (c + 1) * CHUNK)
        hcols = slice(h * DV_A, (h + 1) * DV_A)
        rhs = jnp.concatenate([val_scr[bi, rows, hcols].astype(BF16), qk_scr[p, CHUNK:2 * CHUNK, :]], axis=1)
        sol = _dot(inv[bi, c, h], rhs)
        val_scr[bi, rows, hcols] = sol[:, :DV_A]
        qk_scr[p, CHUNK:2 * CHUNK, :] = sol[:, DV_A:].astype(BF16)

    chains = [(bi, h) for bi in range(nb) for h in range(H_A)]
    state = {ch: s_scr[ch[0] * H_A + ch[1]] for ch in chains}
    for c in range(nchunks):
        rows = slice(c * CHUNK, (c + 1) * CHUNK)
        qs = {(bi, h): _dot(qk_scr[pidx(bi, c, h)], state[bi, h]) for bi, h in chains}
        v_new = {(bi, h): val_scr[bi, rows, h * DV_A:(h + 1) * DV_A] - qs[bi, h][CHUNK:2 * CHUNK]
                 for bi, h in chains}
        o_in = {(bi, h): _dot(attn_scr[pidx(bi, c, h)], v_new[bi, h]) for bi, h in chains}
        ds = {(bi, h): _dot_tn(kdec_scr[bi, rows, h * DK_A:(h + 1) * DK_A], v_new[bi, h]) for bi, h in chains}
        for bi, h in chains:
            hcols = slice(h * DV_A, (h + 1) * DV_A)
            e_last = jnp.exp(dbuf[bi, (c + 1) * CHUNK - 1:(c + 1) * CHUNK, 4 + h:5 + h])
            o = qs[bi, h][0:CHUNK] + o_in[bi, h]
            state[bi, h] = state[bi, h] * e_last + ds[bi, h]
            oa_ref[bi, rows, hcols] = (_rms(o, gnorm) * _silu(z_ref[bi, rows, hcols])).astype(BF16)
    for bi, h in chains:
        s_scr[bi * H_A + h] = state[bi, h]

    @pl.when(t == pl.num_programs(1) - 1)
    def _():
        sout_ref[...] = s_scr[...].reshape(sout_ref.shape)


def _cumsum_rows(g, tri_bd):
    hi = g.astype(BF16)
    r1 = g - hi.astype(F32)
    mid = r1.astype(BF16)
    lo = (r1 - mid.astype(F32)).astype(BF16)
    tb = tri_bd.astype(BF16)
    out = jnp.dot(tb, hi, preferred_element_type=F32)
    out = out + jnp.dot(tb, mid, preferred_element_type=F32)
    return out + jnp.dot(tb, lo, preferred_element_type=F32)


def _gdn(u, prm, s0, tb):
    b, t, _ = u.shape
    nb = max(1, min(b, GDN_PROBLEMS // (tb // CHUNK * H_A)))
    assert b % nb == 0, (b, nb)
    nprob = nb * (tb // CHUNK) * H_A
    return pl.pallas_call(
        functools.partial(_gdn_kernel, nb=nb, tb=tb),
        grid=(b // nb, t // tb),
        in_specs=[pl.BlockSpec((nb, tb, QKV_A), lambda i, j: (i, j, 0)),
                  pl.BlockSpec((nb, tb, MIX_A), lambda i, j: (i, j, COL_Z)),
                  pl.BlockSpec((nb, tb, LANES), lambda i, j: (i, j, COL_BA)),
                  pl.BlockSpec((SUBLANES, LANES), lambda i, j: (0, 0)),
                  pl.BlockSpec((nb, H_A, DK_A, DV_A), lambda i, j: (i, 0, 0, 0))],
        out_specs=[pl.BlockSpec((nb, tb, MIX_A), lambda i, j: (i, j, 0)),
                   pl.BlockSpec((nb, H_A, DK_A, DV_A), lambda i, j: (i, 0, 0, 0))],
        out_shape=[jax.ShapeDtypeStruct((b, t, MIX_A), BF16),
                   jax.ShapeDtypeStruct((b, H_A, DK_A, DV_A), F32)],
        scratch_shapes=[pltpu.VMEM((nb, tb, LANES), F32), pltpu.VMEM((nb, tb, LANES), F32),
                        pltpu.VMEM((nb, tb, MIX_A), F32),
                        pltpu.VMEM((nprob, 2 * CHUNK, DK_A), BF16),
                        pltpu.VMEM((nb, tb, H_A * DK_A), BF16),
                        pltpu.VMEM((nprob, CHUNK, CHUNK), BF16),
                        pltpu.VMEM((nb * H_A, DK_A, DV_A), F32)],
        compiler_params=_params(("parallel", "arbitrary")),
        name="gdn",
    )(u, u, u, prm, s0)


def _band_kernel(q_ref, k_ref, v_ref, bias_ref, ck_ref, cv_ref, o_ref, knew_ref, vnew_ref, kbuf, vbuf,
                 *, nb, qb, t_len, n_invalid):
    t = pl.program_id(1)
    keep = knew_ref.shape[1]

    @pl.when(t == 0)
    def _():
        for bi in range(nb):
            kbuf[bi, 0:BAND, :] = ck_ref[bi].astype(BF16)
            vbuf[bi, 0:BAND, :] = cv_ref[bi].astype(BF16)
            kbuf[bi, BAND:BAND + t_len, :] = k_ref[bi].astype(BF16)
            vbuf[bi, BAND:BAND + t_len, :] = v_ref[bi].astype(BF16)
            knew_ref[bi] = k_ref[bi, t_len - keep:, :]
            vnew_ref[bi] = v_ref[bi, t_len - keep:, :]

    start = pl.multiple_of(t * qb, qb)
    lane = lax.broadcasted_iota(jnp.int32, (1, MIX_B), 1)
    if n_invalid:
        kpos = lax.broadcasted_iota(jnp.int32, (qb, qb + BAND), 1) + start
        valid = kpos >= n_invalid
    chains = [(bi, h) for bi in range(nb) for h in range(H_B)]
    head = [(lane >= h * DH_B) & (lane < (h + 1) * DH_B) for h in range(H_B)]
    q = [q_ref[bi] * DH_B ** -0.5 for bi in range(nb)]
    s = {(bi, h): _dot_nt(jnp.where(head[h], q[bi], 0.0), kbuf[bi, pl.ds(start, qb + BAND), :]) + bias_ref[h]
         for bi, h in chains}
    p, inv_l = {}, {}
    for ch in chains:
        sc = jnp.where(valid, s[ch], NEG_INF) if n_invalid else s[ch]
        p[ch] = jnp.exp(sc - jnp.max(sc, axis=-1, keepdims=True))
        inv_l[ch] = 1.0 / jnp.sum(p[ch], axis=-1, keepdims=True)
    o = {(bi, h): _dot(p[bi, h], vbuf[bi, pl.ds(start, qb + BAND), :]) for bi, h in chains}
    for bi in range(nb):
        acc = jnp.zeros((qb, MIX_B), F32)
        for h in range(H_B):
            acc = acc + jnp.where(head[h], o[bi, h] * inv_l[bi, h], 0.0)
        o_ref[bi] = acc.astype(BF16)


def _band(u, bias, ck, cv, qb, n_invalid):
    b, t, _ = u.shape
    nb = max(1, min(b, MIX_ROWS // qb))
    assert b % nb == 0 and ck.shape[0] == b, (b, nb)
    keep = min(BAND, t)
    new_spec = pl.BlockSpec((nb, keep, MIX_B), lambda i, j: (i, 0, 0))
    return pl.pallas_call(
        functools.partial(_band_kernel, nb=nb, qb=qb, t_len=t, n_invalid=n_invalid),
        grid=(b // nb, t // qb),
        in_specs=[pl.BlockSpec((nb, qb, MIX_B), lambda i, j: (i, j, COL_B)),
                  pl.BlockSpec((nb, t, MIX_B), lambda i, j: (i, 0, COL_B + 1)),
                  pl.BlockSpec((nb, t, MIX_B), lambda i, j: (i, 0, COL_B + 2)),
                  pl.BlockSpec((H_B, qb, qb + BAND), lambda i, j: (0, 0, 0)),
                  pl.BlockSpec((nb, BAND, MIX_B), lambda i, j: (i, 0, 0)),
                  pl.BlockSpec((nb, BAND, MIX_B), lambda i, j: (i, 0, 0))],
        out_specs=[pl.BlockSpec((nb, qb, MIX_B), lambda i, j: (i, j, 0)), new_spec, new_spec],
        out_shape=[jax.ShapeDtypeStruct((b, t, MIX_B), BF16),
                   jax.ShapeDtypeStruct((b, keep, MIX_B), F32), jax.ShapeDtypeStruct((b, keep, MIX_B), F32)],
        scratch_shapes=[pltpu.VMEM((nb, BAND + t, MIX_B), BF16), pltpu.VMEM((nb, BAND + t, MIX_B), BF16)],
        compiler_params=_params(("parallel", "arbitrary")),
        name="band",
    )(u, u, u, bias, ck, cv)


def _band_bias_table(rel_bias, qb):
    rb = rel_bias.astype(F32)
    nh = rb.shape[0]
    w = qb + BAND
    p = w + qb - 1
    rel_lo, rel_hi = -(qb - 1) - BAND, qb - 1
    mid_hi = min(rel_hi, REL_CLIP)
    f = jnp.concatenate([jnp.broadcast_to(rb[:, :1], (nh, -REL_CLIP - rel_lo)),
                         rb[:, :mid_hi + REL_CLIP + 1],
                         jnp.broadcast_to(rb[:, -1:], (nh, rel_hi - mid_hi))], axis=1)
    rolled = jnp.tile(jnp.pad(f, ((0, 0), (0, 1))), (1, qb))[:, :qb * p].reshape(nh, qb, p)
    tab = rolled[:, :, qb - 1:qb - 1 + w]
    ci = np.arange(qb)[:, None] // CHUNK
    cj = np.arange(w)[None, :] // CHUNK
    return jnp.where((cj >= ci) & (cj <= ci + BAND_CHUNKS), tab, NEG_INF)


def _ret_kernel(q_ref, k_ref, v_ref, g_ref, rope_ref, pow_ref, dmask_ref, sq_ref, r0_ref, o_ref, rout_ref,
                r_scr, *, nb, tb):
    t = pl.program_id(1)

    @pl.when(t == 0)
    def _():
        r_scr[...] = r0_ref[...]

    def rope(x):
        return (x * rope_ref[0] + pltpu.roll(x, MIX_C - DK_C // 2, 1) * rope_ref[1]
                + pltpu.roll(x, DK_C // 2, 1) * rope_ref[2])

    lane = lax.broadcasted_iota(jnp.int32, (1, MIX_C), 1)
    avg = sq_ref[2]
    seqs = range(nb)
    chains = [(bi, h) for bi in seqs for h in range(H_C)]
    head = [(lane >= h * DK_C) & (lane < (h + 1) * DK_C) for h in range(H_C)]
    q = [rope(q_ref[bi]) for bi in seqs]
    k = [rope(k_ref[bi]) * DK_C ** -0.5 for bi in seqs]
    v = [v_ref[bi].astype(BF16) for bi in seqs]
    r = [r_scr[bi] for bi in seqs]
    cross = [_dot(q[bi] * pow_ref[0], r[bi]) for bi in seqs]
    s = {(bi, h): _dot_nt(jnp.where(head[h], q[bi], 0.0), k[bi]) * dmask_ref[h] for bi, h in chains}
    kv = [_dot_tn(k[bi] * pow_ref[1], v[bi]) for bi in seqs]
    inner = {(bi, h): _dot(s[bi, h], v[bi]) for bi, h in chains}
    acc = []
    for bi in seqs:
        a = cross[bi]
        for h in range(H_C):
            a = a + jnp.where(head[h], inner[bi, h], 0.0)
        acc.append(a)
        r_scr[bi] = r[bi] * sq_ref[0] + sq_ref[1] * kv[bi]
    mean = [_dot_exact_rhs(acc[bi], avg) for bi in seqs]
    xc = [acc[bi] - mean[bi] for bi in seqs]
    var = [_dot_exact_rhs(xc[bi] * xc[bi], avg) for bi in seqs]
    for bi in seqs:
        o_ref[bi] = ((xc[bi] * lax.rsqrt(var[bi] + EPS)) * _silu(g_ref[bi])).astype(BF16)

    @pl.when(t == pl.num_programs(1) - 1)
    def _():
        rout_ref[...] = r_scr[...]


def _ret(u, rope_tab, pow_tab, dmask, sq_tab, r0, tb):
    b, t, _ = u.shape
    nb = max(1, min(b, MIX_ROWS // tb))
    assert b % nb == 0, (b, nb)
    ublk = lambda c: pl.BlockSpec((nb, tb, MIX_C), lambda i, j: (i, j, COL_B + c))
    return pl.pallas_call(
        functools.partial(_ret_kernel, nb=nb, tb=tb),
        grid=(b // nb, t // tb),
        in_specs=[ublk(3), ublk(4), ublk(5), ublk(6),
                  pl.BlockSpec((3, tb, MIX_C), lambda i, j: (0, j, 0)),
                  pl.BlockSpec((2, tb, MIX_C), lambda i, j: (0, 0, 0)),
                  pl.BlockSpec((H_C, tb, tb), lambda i, j: (0, 0, 0)),
                  pl.BlockSpec((3, MIX_C, MIX_C), lambda i, j: (0, 0, 0)),
                  pl.BlockSpec((nb, MIX_C, MIX_C), lambda i, j: (i, 0, 0))],
        out_specs=[pl.BlockSpec((nb, tb, MIX_C), lambda i, j: (i, j, 0)),
                   pl.BlockSpec((nb, MIX_C, MIX_C), lambda i, j: (i, 0, 0))],
        out_shape=[jax.ShapeDtypeStruct((b, t, MIX_C), BF16),
                   jax.ShapeDtypeStruct((b, MIX_C, MIX_C), F32)],
        scratch_shapes=[pltpu.VMEM((nb, MIX_C, MIX_C), F32)],
        compiler_params=_params(("parallel", "arbitrary")),
        name="ret",
    )(u, u, u, u, rope_tab, pow_tab, dmask, sq_tab, r0)


def _ret_tables(t_len, pos0, tb):
    half = DK_C // 2
    inv_freq = np.exp(-math.log(ROPE_BASE) * np.arange(half) / half)
    ang = (pos0 + np.arange(t_len))[:, None] * inv_freq[None, :]
    cos, sin = np.cos(ang), np.sin(ang)
    zero = np.zeros_like(sin)
    per_head = lambda a, b_: np.tile(np.concatenate([a, b_], axis=1), (1, H_C))
    rope_tab = np.stack([per_head(cos, cos), per_head(-sin, zero), per_head(zero, sin)])
    lg = np.log1p(-np.exp2(-5.0 - np.arange(H_C)))
    lg_lane = np.repeat(lg, DK_C)
    n = np.arange(tb, dtype=np.float64)
    pow_tab = np.stack([np.exp(lg_lane[None, :] * (n + 1.0)[:, None]),
                        np.exp(lg_lane[None, :] * (tb - 1.0 - n)[:, None])])
    diff = n[:, None] - n[None, :]
    causal = diff >= 0
    dmask = np.where(causal, np.exp(lg[:, None, None] * np.where(causal, diff, 0.0)), 0.0)
    head = np.arange(MIX_C) // DK_C
    same = (head[:, None] == head[None, :]).astype(np.float64)
    carry = np.broadcast_to(np.exp(lg_lane * tb)[:, None], (MIX_C, MIX_C))
    sq_tab = np.stack([carry, same, same / DV_C])
    return tuple(jnp.asarray(a, F32) for a in (rope_tab, pow_tab, dmask, sq_tab))


def _relayout_w_in(w_in):
    n_small = 2 * H_A
    a_end = QKV_A + MIX_A
    pad = jnp.zeros(w_in.shape[:2] + (LANES - n_small,), w_in.dtype)
    return jnp.concatenate([w_in[..., :a_end], w_in[..., a_end + n_small:],
                            w_in[..., a_end:a_end + n_small], pad], axis=-1).astype(BF16)


def _block_diag(r):
    b = r.shape[0]
    eye = jnp.eye(H_C, dtype=r.dtype)
    return jnp.einsum('bhkv,hg->bhkgv', r, eye).reshape(b, H_C * DK_C, H_C * DV_C)


def _diag_blocks(r):
    b = r.shape[0]
    r = r.reshape(b, H_C, DK_C, H_C, DV_C)
    return jnp.stack([r[:, h, :, h, :] for h in range(H_C)], axis=1)


def _run(x, pos0, caches, weights):
    (norm_ffn1, wg1, wu1, wd1, norm_mix, w_in, conv_w, a_log, dt_bias, gdn_norm, rel_bias, w_out,
     norm_ffn2, wg2, wu2, wd2, norm_final) = weights
    b, t, _ = x.shape
    n = b * t
    tb = min(MIX_TB, t)
    assert t % tb == 0 and tb % CHUNK == 0, (t, tb)
    rope_tab, pow_tab, dmask, sq_tab = _ret_tables(t, pos0, tb)
    gf = norm_final.reshape(1, D_MODEL)
    norm_ffn1, norm_mix, norm_ffn2 = (g.reshape(DEPTH, 1, D_MODEL) for g in (norm_ffn1, norm_mix, norm_ffn2))
    xf = x.reshape(n, D_MODEL)
    new = []
    for l in range(DEPTH):
        if caches is None:
            s_gdn0 = jnp.zeros((b, H_A, DK_A, DV_A), F32)
            cb0 = jnp.zeros((b, SUBLANES, QKV_A), F32)
            r0 = jnp.zeros((b, MIX_C, MIX_C), F32)
            ck = cv = jnp.zeros((b, BAND, MIX_B), F32)
            n_invalid = BAND
        else:
            state_gdn, state_conv, cache_k, cache_v, state_ret = caches
            s_gdn0 = state_gdn[l]
            cb0 = jnp.pad(state_conv[l], ((0, 0), (SUBLANES - (CONV_W - 1), 0), (0, 0)))
            r0 = _block_diag(state_ret[l])
            ck = cache_k[l].reshape(b, BAND, MIX_B)
            cv = cache_v[l].reshape(b, BAND, MIX_B)
            n_invalid = 0
        xf = _ffn(xf, norm_ffn1, wg1, wu1, wd1, l)
        u, conv_tail = _inproj(xf.reshape(b, t, D_MODEL), norm_mix, w_in, conv_w[l], cb0, l)
        decay_lanes = lambda v: jnp.pad(v.astype(F32), (H_A, LANES - 2 * H_A))
        prm = jnp.concatenate([decay_lanes(-jnp.exp(a_log[l].astype(F32)))[None],
                               decay_lanes(dt_bias[l])[None], gdn_norm[l].astype(F32)[None],
                               jnp.zeros((SUBLANES - 3, LANES), F32)], axis=0)
        oa, s_gdn = _gdn(u, prm, s_gdn0, tb)
        ob, k_tail, v_tail = _band(u, _band_bias_table(rel_bias[l], tb), ck, cv, tb, n_invalid)
        oc, r_new = _ret(u, rope_tab, pow_tab, dmask, sq_tab, r0, tb)
        xf = _mix_ffn(xf, oa.reshape(n, MIX_A), ob.reshape(n, MIX_B), oc.reshape(n, MIX_C), w_out,
                      norm_ffn2, wg2, wu2, wd2, gf, l, l == DEPTH - 1)
        conv_new = conv_tail[:, SUBLANES - (CONV_W - 1):, :]
        kb_new = k_tail.reshape(b, -1, H_B, DH_B)
        vb_new = v_tail.reshape(b, -1, H_B, DH_B)
        new.append((s_gdn, conv_new, kb_new, vb_new, _diag_blocks(r_new)))
    g_, c_, k_, v_, r_ = zip(*new)
    return (xf.reshape(b, t, D_MODEL), jnp.stack(g_), jnp.stack(c_), jnp.stack(k_), jnp.stack(v_),
            jnp.stack(r_))


def kernel(x_prompt, x_sample, state_gdn, state_conv, cache_band_k, cache_band_v, state_ret, norm_ffn1, w_ffn1_gate, w_ffn1_up, w_ffn1_down, norm_mix, w_in, conv_w, a_log, dt_bias, gdn_norm, rel_bias, w_out, norm_ffn2, w_ffn2_gate, w_ffn2_up, w_ffn2_down, norm_final):
    past_len = 1024
    weights = (norm_ffn1, w_ffn1_gate.astype(BF16), w_ffn1_up.astype(BF16), w_ffn1_down.astype(BF16),
               norm_mix, _relayout_w_in(w_in), conv_w, a_log, dt_bias, gdn_norm, rel_bias,
               w_out.astype(BF16), norm_ffn2, w_ffn2_gate.astype(BF16), w_ffn2_up.astype(BF16),
               w_ffn2_down.astype(BF16), norm_final)
    p = _run(x_prompt, 0, None, weights)
    s = _run(x_sample, past_len, (state_gdn, state_conv, cache_band_k, cache_band_v, state_ret), weights)
    return (p[0], s[0]) + p[1:] + s[1:]
```

```python
import functools
import math

import jax
import jax.numpy as jnp
import numpy as np
from jax import lax
from jax.experimental import pallas as pl
from jax.experimental.pallas import tpu as pltpu

F32 = jnp.float32
BF16 = jnp.bfloat16

D_MODEL = 1024
DEPTH = 2
CHUNK = 64
H_A, DK_A, DV_A, CONV_W = 4, 128, 128, 4
H_B, DH_B, BAND_CHUNKS, REL_CLIP = 4, 64, 8, 128
H_C, DK_C, DV_C = 4, 64, 64
ROPE_BASE = 10000.0
D_FF = 2816
EPS = 1e-6
NEG_INF = -1e30
BAND = BAND_CHUNKS * CHUNK
QKV_A = 3 * H_A * DK_A
MIX_A = H_A * DV_A
MIX_B = H_B * DH_B
MIX_C = H_C * DV_C
LANES = 128
SUBLANES = 8
A_COLS = QKV_A + MIX_A + LANES
BC_COLS = 3 * MIX_B + 4 * MIX_C
COL_Z = QKV_A // MIX_A
COL_BA = (QKV_A + MIX_A) // LANES
VMEM_LIMIT = 56 * 1024 * 1024

FFN_TM = 1024
FFN_SUB = 512
FFN_TF = 256
PROJ_TM = 512
PROJ_SUB = 256
MIX_TB = 256
GDN_PROBLEMS = 32
MIX_ROWS = 512
BAND_GROUP_ROWS = 256


def _params(sem):
    return pltpu.CompilerParams(dimension_semantics=sem, vmem_limit_bytes=VMEM_LIMIT)


def _rms(x, g):
    return (x * lax.rsqrt(jnp.mean(x * x, axis=-1, keepdims=True) + EPS)) * g


def _silu(x):
    return x * jax.nn.sigmoid(x)


def _dot(a, b):
    return jnp.dot(a.astype(BF16), b.astype(BF16), preferred_element_type=F32)


def _dot_nt(a, b):
    return lax.dot_general(a.astype(BF16), b.astype(BF16), (((1,), (1,)), ((), ())),
                           preferred_element_type=F32)


def _dot_tn(a, b):
    return lax.dot_general(a.astype(BF16), b.astype(BF16), (((0,), (0,)), ((), ())),
                           preferred_element_type=F32)


def _dot_exact_rhs(a, b):
    hi = a.astype(BF16)
    r1 = a - hi.astype(F32)
    mid = r1.astype(BF16)
    lo = (r1 - mid.astype(F32)).astype(BF16)
    bb = b.astype(BF16)
    out = jnp.dot(hi, bb, preferred_element_type=F32)
    out = out + jnp.dot(mid, bb, preferred_element_type=F32)
    return out + jnp.dot(lo, bb, preferred_element_type=F32)


def _half_swiglu(x, g, wg_ref, wu_ref, wd_ref):
    h = _rms(x, g).astype(BF16)
    acc = None
    for s in range(D_FF // FFN_TF):
        cols = slice(s * FFN_TF, (s + 1) * FFN_TF)
        gate = jnp.dot(h, wg_ref[:, cols], preferred_element_type=F32)
        up = jnp.dot(h, wu_ref[:, cols], preferred_element_type=F32)
        act = (_silu(gate) * up).astype(BF16)
        part = jnp.dot(act, wd_ref[cols, :], preferred_element_type=F32)
        acc = part if acc is None else acc + part
    return 0.5 * acc


def _ffn_kernel(x_ref, g_ref, wg_ref, wu_ref, wd_ref, o_ref, *, sub):
    for r in range(x_ref.shape[0] // sub):
        rows = slice(r * sub, (r + 1) * sub)
        x = x_ref[rows, :]
        o_ref[rows, :] = x + _half_swiglu(x, g_ref[...], wg_ref, wu_ref, wd_ref)


def _mix_ffn_kernel(x_ref, oa_ref, ob_ref, oc_ref, woa_ref, wob_ref, woc_ref, g_ref, wg_ref, wu_ref,
                    wd_ref, gf_ref, o_ref, *, sub, final_norm):
    for r in range(x_ref.shape[0] // sub):
        rows = slice(r * sub, (r + 1) * sub)
        mix = jnp.dot(oa_ref[rows, :], woa_ref[...], preferred_element_type=F32)
        mix = mix + jnp.dot(ob_ref[rows, :], wob_ref[...], preferred_element_type=F32)
        mix = mix + jnp.dot(oc_ref[rows, :], woc_ref[...], preferred_element_type=F32)
        x2 = x_ref[rows, :] + mix
        y = x2 + _half_swiglu(x2, g_ref[...], wg_ref, wu_ref, wd_ref)
        if final_norm:
            y = _rms(y, gf_ref[...])
        o_ref[rows, :] = y


def _resident(shape, index_map):
    return pl.BlockSpec(shape, index_map, pipeline_mode=pl.Buffered(1))


def _ffn_weight_specs(layer):
    return [
        _resident((None, D_MODEL, D_FF), lambda i: (layer, 0, 0)),
        _resident((None, D_MODEL, D_FF), lambda i: (layer, 0, 0)),
        _resident((None, D_FF, D_MODEL), lambda i: (layer, 0, 0)),
    ]


def _ffn(x, g, wg, wu, wd, layer):
    n = x.shape[0]
    tm = min(FFN_TM, n)
    sub = min(FFN_SUB, tm)
    assert n % tm == 0 and tm % sub == 0, (n, tm)
    tok = pl.BlockSpec((tm, D_MODEL), lambda i: (i, 0))
    return pl.pallas_call(
        functools.partial(_ffn_kernel, sub=sub),
        grid=(n // tm,),
        in_specs=[tok, pl.BlockSpec((None, 1, D_MODEL), lambda i: (layer, 0, 0))] + _ffn_weight_specs(layer),
        out_specs=tok,
        out_shape=jax.ShapeDtypeStruct((n, D_MODEL), F32),
        compiler_params=_params(("parallel",)),
        name="ffn",
    )(x, g, wg, wu, wd)


def _mix_ffn(x, oa, ob, oc, w_out, g, wg, wu, wd, g_final, layer, final_norm):
    n = x.shape[0]
    tm = min(FFN_TM, n)
    sub = min(FFN_SUB, tm)
    assert n % tm == 0 and tm % sub == 0, (n, tm)
    tok = pl.BlockSpec((tm, D_MODEL), lambda i: (i, 0))
    in_specs = [
        tok,
        pl.BlockSpec((tm, MIX_A), lambda i: (i, 0)),
        pl.BlockSpec((tm, MIX_B), lambda i: (i, 0)),
        pl.BlockSpec((tm, MIX_C), lambda i: (i, 0)),
        _resident((None, MIX_A, D_MODEL), lambda i: (layer, 0, 0)),
        _resident((None, MIX_B, D_MODEL), lambda i: (layer, MIX_A // MIX_B, 0)),
        _resident((None, MIX_C, D_MODEL), lambda i: (layer, (MIX_A + MIX_B) // MIX_C, 0)),
        pl.BlockSpec((None, 1, D_MODEL), lambda i: (layer, 0, 0)),
    ] + _ffn_weight_specs(layer) + [pl.BlockSpec((1, D_MODEL), lambda i: (0, 0))]
    return pl.pallas_call(
        functools.partial(_mix_ffn_kernel, sub=sub, final_norm=final_norm),
        grid=(n // tm,),
        in_specs=in_specs,
        out_specs=tok,
        out_shape=jax.ShapeDtypeStruct((n, D_MODEL), F32),
        compiler_params=_params(("parallel",)),
        name="mix_ffn",
    )(x, oa, ob, oc, w_out, w_out, w_out, g, wg, wu, wd, g_final)


def _inproj_kernel(x_ref, g_ref, wa_ref, wbc_ref, convw_ref, cb0_ref, o_ref, obc_ref, tail_ref, xbuf,
                   *, nseq, tl, sub):
    j = pl.program_id(1)
    nslab = QKV_A // LANES

    @pl.when(j == 0)
    def _():
        for si in range(nseq):
            for sl in range(nslab):
                xbuf[si * nslab + sl, 0:SUBLANES, :] = cb0_ref[si, :, sl * LANES:(sl + 1) * LANES]

    if sub <= tl:
        pieces = [[(si, r0, sub)] for si in range(nseq) for r0 in range(0, tl, sub)]
    else:
        per = sub // tl
        pieces = [[(si, 0, tl) for si in range(g0, g0 + per)] for g0 in range(0, nseq, per)]
    for piece in pieces:
        xs = [x_ref[si, r0:r0 + ln, :] for si, r0, ln in piece]
        h = _rms(xs[0] if len(xs) == 1 else jnp.concatenate(xs, axis=0), g_ref[...]).astype(BF16)
        u = jnp.dot(h, wa_ref[...], preferred_element_type=F32)
        u_bc = jnp.dot(h, wbc_ref[...], preferred_element_type=F32)
        lo = 0
        for si, r0, ln in piece:
            o_ref[si, r0:r0 + ln, QKV_A:] = u[lo:lo + ln, QKV_A:]
            obc_ref[si, r0:r0 + ln, :] = u_bc[lo:lo + ln, :]
            for sl in range(nslab):
                xbuf[si * nslab + sl, SUBLANES + r0:SUBLANES + r0 + ln, :] = u[lo:lo + ln, sl * LANES:(sl + 1) * LANES]
            for c0 in range(r0, r0 + ln, CHUNK):
                w0 = c0 + SUBLANES - (CONV_W - 1)
                for sl in range(nslab):
                    cols = slice(sl * LANES, (sl + 1) * LANES)
                    idx = si * nslab + sl
                    y = xbuf[idx, w0:w0 + CHUNK, :] * convw_ref[0:1, cols]
                    for tap in range(1, CONV_W):
                        y = y + xbuf[idx, w0 + tap:w0 + tap + CHUNK, :] * convw_ref[tap:tap + 1, cols]
                    y = _silu(y)
                    if sl < 2 * H_A:
                        y = y * lax.rsqrt(jnp.sum(y * y, axis=-1, keepdims=True) + EPS)
                        if sl < H_A:
                            y = y * DK_A ** -0.5
                    o_ref[si, c0:c0 + CHUNK, cols] = y
            lo += ln
    for idx in range(nseq * nslab):
        si, sl = divmod(idx, nslab)
        tail_ref[si, :, sl * LANES:(sl + 1) * LANES] = xbuf[idx, tl:tl + SUBLANES, :]
        xbuf[idx, 0:SUBLANES, :] = xbuf[idx, tl:tl + SUBLANES, :]


def _inproj(x, g, w_parts, conv_w, cb0, layer):
    b, t, _ = x.shape
    tl = min(PROJ_TM, t)
    nseq = max(1, min(b, PROJ_TM // tl))
    sub = min(PROJ_SUB, nseq * tl)
    assert b % nseq == 0 and t % tl == 0 and (tl % sub == 0 or sub % tl == 0) and tl % CHUNK == 0, (b, t)
    return pl.pallas_call(
        functools.partial(_inproj_kernel, nseq=nseq, tl=tl, sub=sub),
        grid=(b // nseq, t // tl),
        in_specs=[pl.BlockSpec((nseq, tl, D_MODEL), lambda i, j: (i, j, 0)),
                  pl.BlockSpec((None, 1, D_MODEL), lambda i, j: (layer, 0, 0)),
                  ] + [_resident((None,) + w.shape[1:], lambda i, j: (layer, 0, 0)) for w in w_parts] + [
                  pl.BlockSpec((CONV_W, QKV_A), lambda i, j: (0, 0)),
                  pl.BlockSpec((nseq, SUBLANES, QKV_A), lambda i, j: (i, 0, 0))],
        out_specs=[pl.BlockSpec((nseq, tl, A_COLS), lambda i, j: (i, j, 0)),
                   pl.BlockSpec((nseq, tl, BC_COLS), lambda i, j: (i, j, 0)),
                   pl.BlockSpec((nseq, SUBLANES, QKV_A), lambda i, j: (i, 0, 0))],
        out_shape=[jax.ShapeDtypeStruct((b, t, A_COLS), F32),
                   jax.ShapeDtypeStruct((b, t, BC_COLS), F32),
                   jax.ShapeDtypeStruct((b, SUBLANES, QKV_A), F32)],
        scratch_shapes=[pltpu.VMEM((nseq * QKV_A // LANES, tl + SUBLANES, LANES), F32)],
        compiler_params=_params(("parallel", "arbitrary")),
        name="inproj",
    )(x, g, *w_parts, conv_w, cb0)


def _gdn_kernel(u_ref, z_ref, ba_ref, prm_ref, s0_ref, oa_ref, sout_ref,
                dbuf, bbuf, val_scr, qk_scr, kdec_scr, attn_scr, s_scr, *, nb, tb):
    t = pl.program_id(1)
    nchunks = tb // CHUNK

    @pl.when(t == 0)
    def _():
        s_scr[...] = s0_ref[...].reshape(s_scr.shape)

    row = lax.broadcasted_iota(jnp.int32, (tb, tb), 0)
    col = lax.broadcasted_iota(jnp.int32, (tb, tb), 1)
    shift = CHUNK.bit_length() - 1
    same_chunk = jnp.right_shift(row, shift) == jnp.right_shift(col, shift)
    tri_bd = jnp.where(same_chunk & (col <= row), 1.0, 0.0)
    for bi in range(nb):
        ba = ba_ref[bi]
        x = ba + prm_ref[1:2, :]
        softplus = jnp.maximum(x, 0.0) + jnp.log1p(jnp.exp(-jnp.abs(x)))
        g = prm_ref[0:1, :] * softplus
        bbuf[bi] = jax.nn.sigmoid(ba)
        dbuf[bi] = _cumsum_rows(g, tri_bd)

    r_i = lax.broadcasted_iota(jnp.int32, (CHUNK, CHUNK), 0)
    c_i = lax.broadcasted_iota(jnp.int32, (CHUNK, CHUNK), 1)
    tril = c_i <= r_i
    strict = c_i < r_i
    eye = c_i == r_i
    eye_f = jnp.where(eye, 1.0, 0.0)
    gnorm = prm_ref[2:3, :]

    def pidx(bi, c, h):
        return (bi * nchunks + c) * H_A + h

    problems = [(bi, c, h) for bi in range(nb) for c in range(nchunks) for h in range(H_A)]
    lmask, qk, a_low = {}, {}, {}
    for bi in range(nb):
        for c in range(nchunks):
            rows = slice(c * CHUNK, (c + 1) * CHUNK)
            dch = dbuf[bi, rows, :]
            beta = bbuf[bi, rows, :]
            e_d = jnp.exp(dch)
            e_dec = jnp.exp(dch[CHUNK - 1:CHUNK, :] - dch)
            for h in range(H_A):
                p = pidx(bi, c, h)
                hcols = slice(h * DV_A, (h + 1) * DV_A)
                q = u_ref[bi, rows, h * DK_A:(h + 1) * DK_A]
                k = u_ref[bi, rows, MIX_A + h * DK_A:MIX_A + (h + 1) * DK_A]
                v = u_ref[bi, rows, 2 * MIX_A + h * DV_A:2 * MIX_A + (h + 1) * DV_A]
                dcol = dch[:, 4 + h:5 + h]
                bcol = beta[:, h:h + 1]
                edcol = e_d[:, 4 + h:5 + h]
                drow = jnp.sum(jnp.where(eye, dcol, 0.0), axis=0, keepdims=True)
                diff = dcol - drow
                lmask[bi, c, h] = jnp.where(tril, jnp.exp(jnp.where(tril, diff, 0.0)), 0.0)
                kb = k * bcol
                qk[bi, c, h] = _dot_nt(jnp.concatenate([q, kb], axis=0), k)
                val_scr[bi, rows, hcols] = v * bcol
                qk_scr[p, 0:CHUNK, :] = (q * edcol).astype(BF16)
                qk_scr[p, CHUNK:2 * CHUNK, :] = (kb * edcol).astype(BF16)
                kdec_scr[bi, rows, hcols] = (k * e_dec[:, 4 + h:5 + h]).astype(BF16)
    for pr in problems:
        attn_scr[pidx(*pr)] = (qk[pr][0:CHUNK] * lmask[pr]).astype(BF16)
        a_low[pr] = jnp.where(strict, qk[pr][CHUNK:2 * CHUNK] * lmask[pr], 0.0)
    pw = a_low
    inv = {pr: eye_f - a_low[pr] for pr in problems}
    for _ in range(5):
        pw = {pr: _dot(pw[pr], pw[pr]) for pr in problems}
        inv = {pr: inv[pr] + _dot(inv[pr], pw[pr]) for pr in problems}
    for bi, c, h in problems:
        p = pidx(bi, c, h)
        rows = slice(c * CHUNK, (c + 1) * CHUNK)
        hcols = slice(h * DV_A, (h + 1) * DV_A)
        rhs = jnp.concatenate([val_scr[bi, rows, hcols].astype(BF16), qk_scr[p, CHUNK:2 * CHUNK, :]], axis=1)
        sol = _dot(inv[bi, c, h], rhs)
        val_scr[bi, rows, hcols] = sol[:, :DV_A]
        qk_scr[p, CHUNK:2 * CHUNK, :] = sol[:, DV_A:].astype(BF16)

    chains = [(bi, h) for bi in range(nb) for h in range(H_A)]
    state = {ch: s_scr[ch[0] * H_A + ch[1]] for ch in chains}
    for c in range(nchunks):
        rows = slice(c * CHUNK, (c + 1) * CHUNK)
        qs = {(bi, h): _dot(qk_scr[pidx(bi, c, h)], state[bi, h]) for bi, h in chains}
        v_new = {(bi, h): val_scr[bi, rows, h * DV_A:(h + 1) * DV_A] - qs[bi, h][CHUNK:2 * CHUNK]
                 for bi, h in chains}
        o_in = {(bi, h): _dot(attn_scr[pidx(bi, c, h)], v_new[bi, h]) for bi, h in chains}
        ds = {(bi, h): _dot_tn(kdec_scr[bi, rows, h * DK_A:(h + 1) * DK_A], v_new[bi, h]) for bi, h in chains}
        for bi, h in chains:
            hcols = slice(h * DV_A, (h + 1) * DV_A)
            e_last = jnp.exp(dbuf[bi, (c + 1) * CHUNK - 1:(c + 1) * CHUNK, 4 + h:5 + h])
            o = qs[bi, h][0:CHUNK] + o_in[bi, h]
            state[bi, h] = state[bi, h] * e_last + ds[bi, h]
            oa_ref[bi, rows, hcols] = (_rms(o, gnorm) * _silu(z_ref[bi, rows, hcols])).astype(BF16)
    for bi, h in chains:
        s_scr[bi * H_A + h] = state[bi, h]

    @pl.when(t == pl.num_programs(1) - 1)
    def _():
        sout_ref[...] = s_scr[...].reshape(sout_ref.shape)


def _cumsum_rows(g, tri_bd):
    hi = g.astype(BF16)
    r1 = g - hi.astype(F32)
    mid = r1.astype(BF16)
    lo = (r1 - mid.astype(F32)).astype(BF16)
    tb = tri_bd.astype(BF16)
    out = jnp.dot(tb, hi, preferred_element_type=F32)
    out = out + jnp.dot(tb, mid, preferred_element_type=F32)
    return out + jnp.dot(tb, lo, preferred_element_type=F32)


def _gdn(u, prm, s0, tb):
    b, t, _ = u.shape
    nb = max(1, min(b, GDN_PROBLEMS // (tb // CHUNK * H_A)))
    assert b % nb == 0, (b, nb)
    nprob = nb * (tb // CHUNK) * H_A
    return pl.pallas_call(
        functools.partial(_gdn_kernel, nb=nb, tb=tb),
        grid=(b // nb, t // tb),
        in_specs=[pl.BlockSpec((nb, tb, QKV_A), lambda i, j: (i, j, 0)),
                  pl.BlockSpec((nb, tb, MIX_A), lambda i, j: (i, j, COL_Z)),
                  pl.BlockSpec((nb, tb, LANES), lambda i, j: (i, j, COL_BA)),
                  pl.BlockSpec((SUBLANES, LANES), lambda i, j: (0, 0)),
                  pl.BlockSpec((nb, H_A, DK_A, DV_A), lambda i, j: (i, 0, 0, 0))],
        out_specs=[pl.BlockSpec((nb, tb, MIX_A), lambda i, j: (i, j, 0)),
                   pl.BlockSpec((nb, H_A, DK_A, DV_A), lambda i, j: (i, 0, 0, 0))],
        out_shape=[jax.ShapeDtypeStruct((b, t, MIX_A), BF16),
                   jax.ShapeDtypeStruct((b, H_A, DK_A, DV_A), F32)],
        scratch_shapes=[pltpu.VMEM((nb, tb, LANES), F32), pltpu.VMEM((nb, tb, LANES), F32),
                        pltpu.VMEM((nb, tb, MIX_A), F32),
                        pltpu.VMEM((nprob, 2 * CHUNK, DK_A), BF16),
                        pltpu.VMEM((nb, tb, H_A * DK_A), BF16),
                        pltpu.VMEM((nprob, CHUNK, CHUNK), BF16),
                        pltpu.VMEM((nb * H_A, DK_A, DV_A), F32)],
        compiler_params=_params(("parallel", "arbitrary")),
        name="gdn",
    )(u, u, u, prm, s0)


def _band_kernel(q_ref, k_ref, v_ref, bias_ref, ck_ref, cv_ref, o_ref, knew_ref, vnew_ref, kbuf, vbuf,
                 *, nb, qb, t_len, n_invalid):
    t = pl.program_id(1)
    keep = knew_ref.shape[1]

    @pl.when(t == 0)
    def _():
        for bi in range(nb):
            kbuf[bi, 0:BAND, :] = ck_ref[bi].astype(BF16)
            vbuf[bi, 0:BAND, :] = cv_ref[bi].astype(BF16)
            kbuf[bi, BAND:BAND + t_len, :] = k_ref[bi].astype(BF16)
            vbuf[bi, BAND:BAND + t_len, :] = v_ref[bi].astype(BF16)
            knew_ref[bi] = k_ref[bi, t_len - keep:, :]
            vnew_ref[bi] = v_ref[bi, t_len - keep:, :]

    start = pl.multiple_of(t * qb, qb)
    lane = lax.broadcasted_iota(jnp.int32, (1, MIX_B), 1)
    if n_invalid:
        kpos = lax.broadcasted_iota(jnp.int32, (qb, qb + BAND), 1) + start
        valid = kpos >= n_invalid
    head = [(lane >= h * DH_B) & (lane < (h + 1) * DH_B) for h in range(H_B)]
    per_group = max(1, BAND_GROUP_ROWS // qb)
    for g0 in range(0, nb, per_group):
        seqs = range(g0, min(g0 + per_group, nb))
        chains = [(bi, h) for bi in seqs for h in range(H_B)]
        q = {bi: q_ref[bi] * DH_B ** -0.5 for bi in seqs}
        s = {(bi, h): _dot_nt(jnp.where(head[h], q[bi], 0.0), kbuf[bi, pl.ds(start, qb + BAND), :])
             + bias_ref[h] for bi, h in chains}
        p, inv_l = {}, {}
        for ch in chains:
            sc = jnp.where(valid, s[ch], NEG_INF) if n_invalid else s[ch]
            p[ch] = jnp.exp(sc - jnp.max(sc, axis=-1, keepdims=True))
            inv_l[ch] = 1.0 / jnp.sum(p[ch], axis=-1, keepdims=True)
        o = {(bi, h): _dot(p[bi, h], vbuf[bi, pl.ds(start, qb + BAND), :]) for bi, h in chains}
        for bi in seqs:
            acc = jnp.zeros((qb, MIX_B), F32)
            for h in range(H_B):
                acc = acc + jnp.where(head[h], o[bi, h] * inv_l[bi, h], 0.0)
            o_ref[bi] = acc.astype(BF16)


def _band(u, bias, ck, cv, qb, n_invalid):
    b, t, _ = u.shape
    nb = max(1, min(b, MIX_ROWS // qb))
    assert b % nb == 0 and ck.shape[0] == b, (b, nb)
    keep = min(BAND, t)
    new_spec = pl.BlockSpec((nb, keep, MIX_B), lambda i, j: (i, 0, 0))
    return pl.pallas_call(
        functools.partial(_band_kernel, nb=nb, qb=qb, t_len=t, n_invalid=n_invalid),
        grid=(b // nb, t // qb),
        in_specs=[pl.BlockSpec((nb, qb, MIX_B), lambda i, j: (i, j, 0)),
                  pl.BlockSpec((nb, t, MIX_B), lambda i, j: (i, 0, 1)),
                  pl.BlockSpec((nb, t, MIX_B), lambda i, j: (i, 0, 2)),
                  pl.BlockSpec((H_B, qb, qb + BAND), lambda i, j: (0, 0, 0)),
                  pl.BlockSpec((nb, BAND, MIX_B), lambda i, j: (i, 0, 0)),
                  pl.BlockSpec((nb, BAND, MIX_B), lambda i, j: (i, 0, 0))],
        out_specs=[pl.BlockSpec((nb, qb, MIX_B), lambda i, j: (i, j, 0)), new_spec, new_spec],
        out_shape=[jax.ShapeDtypeStruct((b, t, MIX_B), BF16),
                   jax.ShapeDtypeStruct((b, keep, MIX_B), F32), jax.ShapeDtypeStruct((b, keep, MIX_B), F32)],
        scratch_shapes=[pltpu.VMEM((nb, BAND + t, MIX_B), BF16), pltpu.VMEM((nb, BAND + t, MIX_B), BF16)],
        compiler_params=_params(("parallel", "arbitrary")),
        name="band",
    )(u, u, u, bias, ck, cv)


def _band_bias_table(rel_bias, qb):
    rb = rel_bias.astype(F32)
    nh = rb.shape[0]
    w = qb + BAND
    p = w + qb - 1
    rel_lo, rel_hi = -(qb - 1) - BAND, qb - 1
    mid_hi = min(rel_hi, REL_CLIP)
    f = jnp.concatenate([jnp.broadcast_to(rb[:, :1], (nh, -REL_CLIP - rel_lo)),
                         rb[:, :mid_hi + REL_CLIP + 1],
                         jnp.broadcast_to(rb[:, -1:], (nh, rel_hi - mid_hi))], axis=1)
    rolled = jnp.tile(jnp.pad(f, ((0, 0), (0, 1))), (1, qb))[:, :qb * p].reshape(nh, qb, p)
    tab = rolled[:, :, qb - 1:qb - 1 + w]
    ci = np.arange(qb)[:, None] // CHUNK
    cj = np.arange(w)[None, :] // CHUNK
    return jnp.where((cj >= ci) & (cj <= ci + BAND_CHUNKS), tab, NEG_INF)


def _ret_kernel(q_ref, k_ref, v_ref, g_ref, rope_ref, pow_ref, dmask_ref, sq_ref, r0_ref, o_ref, rout_ref,
                r_scr, *, nb, tb):
    t = pl.program_id(1)

    @pl.when(t == 0)
    def _():
        r_scr[...] = r0_ref[...]

    def rope(x):
        return (x * rope_ref[0] + pltpu.roll(x, MIX_C - DK_C // 2, 1) * rope_ref[1]
                + pltpu.roll(x, DK_C // 2, 1) * rope_ref[2])

    lane = lax.broadcasted_iota(jnp.int32, (1, MIX_C), 1)
    avg = sq_ref[2]
    seqs = range(nb)
    chains = [(bi, h) for bi in seqs for h in range(H_C)]
    head = [(lane >= h * DK_C) & (lane < (h + 1) * DK_C) for h in range(H_C)]
    q = [rope(q_ref[bi]) for bi in seqs]
    k = [rope(k_ref[bi]) * DK_C ** -0.5 for bi in seqs]
    v = [v_ref[bi].astype(BF16) for bi in seqs]
    r = [r_scr[bi] for bi in seqs]
    cross = [_dot(q[bi] * pow_ref[0], r[bi]) for bi in seqs]
    s = {(bi, h): _dot_nt(jnp.where(head[h], q[bi], 0.0), k[bi]) * dmask_ref[h] for bi, h in chains}
    kv = [_dot_tn(k[bi] * pow_ref[1], v[bi]) for bi in seqs]
    inner = {(bi, h): _dot(s[bi, h], v[bi]) for bi, h in chains}
    acc = []
    for bi in seqs:
        a = cross[bi]
        for h in range(H_C):
            a = a + jnp.where(head[h], inner[bi, h], 0.0)
        acc.append(a)
        r_scr[bi] = r[bi] * sq_ref[0] + sq_ref[1] * kv[bi]
    mean = [_dot_exact_rhs(acc[bi], avg) for bi in seqs]
    xc = [acc[bi] - mean[bi] for bi in seqs]
    var = [_dot_exact_rhs(xc[bi] * xc[bi], avg) for bi in seqs]
    for bi in seqs:
        o_ref[bi] = ((xc[bi] * lax.rsqrt(var[bi] + EPS)) * _silu(g_ref[bi])).astype(BF16)

    @pl.when(t == pl.num_programs(1) - 1)
    def _():
        rout_ref[...] = r_scr[...]


def _ret(u, rope_tab, pow_tab, dmask, sq_tab, r0, tb):
    b, t, _ = u.shape
    nb = max(1, min(b, MIX_ROWS // tb))
    assert b % nb == 0, (b, nb)
    ublk = lambda c: pl.BlockSpec((nb, tb, MIX_C), lambda i, j: (i, j, c))
    return pl.pallas_call(
        functools.partial(_ret_kernel, nb=nb, tb=tb),
        grid=(b // nb, t // tb),
        in_specs=[ublk(3), ublk(4), ublk(5), ublk(6),
                  pl.BlockSpec((3, tb, MIX_C), lambda i, j: (0, j, 0)),
                  pl.BlockSpec((2, tb, MIX_C), lambda i, j: (0, 0, 0)),
                  pl.BlockSpec((H_C, tb, tb), lambda i, j: (0, 0, 0)),
                  pl.BlockSpec((3, MIX_C, MIX_C), lambda i, j: (0, 0, 0)),
                  pl.BlockSpec((nb, MIX_C, MIX_C), lambda i, j: (i, 0, 0))],
        out_specs=[pl.BlockSpec((nb, tb, MIX_C), lambda i, j: (i, j, 0)),
                   pl.BlockSpec((nb, MIX_C, MIX_C), lambda i, j: (i, 0, 0))],
        out_shape=[jax.ShapeDtypeStruct((b, t, MIX_C), BF16),
                   jax.ShapeDtypeStruct((b, MIX_C, MIX_C), F32)],
        scratch_shapes=[pltpu.VMEM((nb, MIX_C, MIX_C), F32)],
        compiler_params=_params(("parallel", "arbitrary")),
        name="ret",
    )(u, u, u, u, rope_tab, pow_tab, dmask, sq_tab, r0)


def _ret_tables(t_len, pos0, tb):
    half = DK_C // 2
    inv_freq = np.exp(-math.log(ROPE_BASE) * np.arange(half) / half)
    ang = (pos0 + np.arange(t_len))[:, None] * inv_freq[None, :]
    cos, sin = np.cos(ang), np.sin(ang)
    zero = np.zeros_like(sin)
    per_head = lambda a, b_: np.tile(np.concatenate([a, b_], axis=1), (1, H_C))
    rope_tab = np.stack([per_head(cos, cos), per_head(-sin, zero), per_head(zero, sin)])
    lg = np.log1p(-np.exp2(-5.0 - np.arange(H_C)))
    lg_lane = np.repeat(lg, DK_C)
    n = np.arange(tb, dtype=np.float64)
    pow_tab = np.stack([np.exp(lg_lane[None, :] * (n + 1.0)[:, None]),
                        np.exp(lg_lane[None, :] * (tb - 1.0 - n)[:, None])])
    diff = n[:, None] - n[None, :]
    causal = diff >= 0
    dmask = np.where(causal, np.exp(lg[:, None, None] * np.where(causal, diff, 0.0)), 0.0)
    head = np.arange(MIX_C) // DK_C
    same = (head[:, None] == head[None, :]).astype(np.float64)
    carry = np.broadcast_to(np.exp(lg_lane * tb)[:, None], (MIX_C, MIX_C))
    sq_tab = np.stack([carry, same, same / DV_C])
    return tuple(jnp.asarray(a, F32) for a in (rope_tab, pow_tab, dmask, sq_tab))


def _split_w_in(w_in):
    a_end = QKV_A + MIX_A + 2 * H_A
    w_a = jnp.pad(w_in[..., :a_end], ((0, 0), (0, 0), (0, A_COLS - a_end)))
    return w_a.astype(BF16), w_in[..., a_end:].astype(BF16)


def _block_diag(r):
    b = r.shape[0]
    eye = jnp.eye(H_C, dtype=r.dtype)
    return jnp.einsum('bhkv,hg->bhkgv', r, eye).reshape(b, H_C * DK_C, H_C * DV_C)


def _diag_blocks(r):
    b = r.shape[0]
    r = r.reshape(b, H_C, DK_C, H_C, DV_C)
    return jnp.stack([r[:, h, :, h, :] for h in range(H_C)], axis=1)


def _run(x, pos0, caches, weights):
    (norm_ffn1, wg1, wu1, wd1, norm_mix, w_in, conv_w, a_log, dt_bias, gdn_norm, rel_bias, w_out,
     norm_ffn2, wg2, wu2, wd2, norm_final) = weights
    b, t, _ = x.shape
    n = b * t
    tb = min(MIX_TB, t)
    assert t % tb == 0 and tb % CHUNK == 0, (t, tb)
    rope_tab, pow_tab, dmask, sq_tab = _ret_tables(t, pos0, tb)
    gf = norm_final.reshape(1, D_MODEL)
    norm_ffn1, norm_mix, norm_ffn2 = (g.reshape(DEPTH, 1, D_MODEL) for g in (norm_ffn1, norm_mix, norm_ffn2))
    xf = x.reshape(n, D_MODEL)
    new = []
    for l in range(DEPTH):
        if caches is None:
            s_gdn0 = jnp.zeros((b, H_A, DK_A, DV_A), F32)
            cb0 = jnp.zeros((b, SUBLANES, QKV_A), F32)
            r0 = jnp.zeros((b, MIX_C, MIX_C), F32)
            ck = cv = jnp.zeros((b, BAND, MIX_B), F32)
            n_invalid = BAND
        else:
            state_gdn, state_conv, cache_k, cache_v, state_ret = caches
            s_gdn0 = state_gdn[l]
            cb0 = jnp.pad(state_conv[l], ((0, 0), (SUBLANES - (CONV_W - 1), 0), (0, 0)))
            r0 = _block_diag(state_ret[l])
            ck = cache_k[l].reshape(b, BAND, MIX_B)
            cv = cache_v[l].reshape(b, BAND, MIX_B)
            n_invalid = 0
        xf = _ffn(xf, norm_ffn1, wg1, wu1, wd1, l)
        u_a, u, conv_tail = _inproj(xf.reshape(b, t, D_MODEL), norm_mix, w_in, conv_w[l], cb0, l)
        decay_lanes = lambda v: jnp.pad(v.astype(F32), (H_A, LANES - 2 * H_A))
        prm = jnp.concatenate([decay_lanes(-jnp.exp(a_log[l].astype(F32)))[None],
                               decay_lanes(dt_bias[l])[None], gdn_norm[l].astype(F32)[None],
                               jnp.zeros((SUBLANES - 3, LANES), F32)], axis=0)
        oa, s_gdn = _gdn(u_a, prm, s_gdn0, tb)
        ob, k_tail, v_tail = _band(u, _band_bias_table(rel_bias[l], tb), ck, cv, tb, n_invalid)
        oc, r_new = _ret(u, rope_tab, pow_tab, dmask, sq_tab, r0, tb)
        xf = _mix_ffn(xf, oa.reshape(n, MIX_A), ob.reshape(n, MIX_B), oc.reshape(n, MIX_C), w_out,
                      norm_ffn2, wg2, wu2, wd2, gf, l, l == DEPTH - 1)
        conv_new = conv_tail[:, SUBLANES - (CONV_W - 1):, :]
        kb_new = k_tail.reshape(b, -1, H_B, DH_B)
        vb_new = v_tail.reshape(b, -1, H_B, DH_B)
        new.append((s_gdn, conv_new, kb_new, vb_new, _diag_blocks(r_new)))
    g_, c_, k_, v_, r_ = zip(*new)
    return (xf.reshape(b, t, D_MODEL), jnp.stack(g_), jnp.stack(c_), jnp.stack(k_), jnp.stack(v_),
            jnp.stack(r_))


def kernel(x_prompt, x_sample, state_gdn, state_conv, cache_band_k, cache_band_v, state_ret, norm_ffn1, w_ffn1_gate, w_ffn1_up, w_ffn1_down, norm_mix, w_in, conv_w, a_log, dt_bias, gdn_norm, rel_bias, w_out, norm_ffn2, w_ffn2_gate, w_ffn2_up, w_ffn2_down, norm_final):
    past_len = 1024
    weights = (norm_ffn1, w_ffn1_gate.astype(BF16), w_ffn1_up.astype(BF16), w_ffn1_down.astype(BF16),
               norm_mix, _split_w_in(w_in), conv_w, a_log, dt_bias, gdn_norm, rel_bias,
               w_out.astype(BF16), norm_ffn2, w_ffn2_gate.astype(BF16), w_ffn2_up.astype(BF16),
               w_ffn2_down.astype(BF16), norm_final)
    p = _run(x_prompt, 0, None, weights)
    s = _run(x_sample, past_len, (state_gdn, state_conv, cache_band_k, cache_band_v, state_ret), weights)
    return (p[0], s[0]) + p[1:] + s[1:]
```

```python
import functools
import math

import jax
import jax.numpy as jnp
import numpy as np
from jax import lax
from jax.experimental import pallas as pl
from jax.experimental.pallas import tpu as pltpu

F32 = jnp.float32
BF16 = jnp.bfloat16

D_MODEL = 1024
DEPTH = 2
CHUNK = 64
H_A, DK_A, DV_A, CONV_W = 4, 128, 128, 4
H_B, DH_B, BAND_CHUNKS, REL_CLIP = 4, 64, 8, 128
H_C, DK_C, DV_C = 4, 64, 64
ROPE_BASE = 10000.0
D_FF = 2816
EPS = 1e-6
NEG_INF = -1e30
BAND = BAND_CHUNKS * CHUNK
QKV_A = 3 * H_A * DK_A
MIX_A = H_A * DV_A
MIX_B = H_B * DH_B
MIX_C = H_C * DV_C
LANES = 128
SUBLANES = 8
A_COLS = QKV_A + MIX_A + LANES
VMEM_LIMIT = 56 * 1024 * 1024

FFN_TM = 1024
FFN_SUB = 512
FFN_TF = 256
PROJ_TM = 512
PROJ_SUB = 256
MIX_TB = 256
GDN_PROBLEMS = 32
MIX_ROWS = 512
BAND_GROUP_ROWS = 256


def _params(sem):
    return pltpu.CompilerParams(dimension_semantics=sem, vmem_limit_bytes=VMEM_LIMIT)


def _rms(x, g):
    return (x * lax.rsqrt(jnp.mean(x * x, axis=-1, keepdims=True) + EPS)) * g


def _silu(x):
    return x * jax.nn.sigmoid(x)


def _dot(a, b):
    return jnp.dot(a.astype(BF16), b.astype(BF16), preferred_element_type=F32)


def _dot_nt(a, b):
    return lax.dot_general(a.astype(BF16), b.astype(BF16), (((1,), (1,)), ((), ())),
                           preferred_element_type=F32)


def _dot_tn(a, b):
    return lax.dot_general(a.astype(BF16), b.astype(BF16), (((0,), (0,)), ((), ())),
                           preferred_element_type=F32)


def _dot_exact_rhs(a, b):
    hi = a.astype(BF16)
    r1 = a - hi.astype(F32)
    mid = r1.astype(BF16)
    lo = (r1 - mid.astype(F32)).astype(BF16)
    bb = b.astype(BF16)
    out = jnp.dot(hi, bb, preferred_element_type=F32)
    out = out + jnp.dot(mid, bb, preferred_element_type=F32)
    return out + jnp.dot(lo, bb, preferred_element_type=F32)


def _half_swiglu(x, g, wg_ref, wu_ref, wd_ref):
    h = _rms(x, g).astype(BF16)
    acc = None
    for s in range(D_FF // FFN_TF):
        cols = slice(s * FFN_TF, (s + 1) * FFN_TF)
        gate = jnp.dot(h, wg_ref[:, cols], preferred_element_type=F32)
        up = jnp.dot(h, wu_ref[:, cols], preferred_element_type=F32)
        act = (_silu(gate) * up).astype(BF16)
        part = jnp.dot(act, wd_ref[cols, :], preferred_element_type=F32)
        acc = part if acc is None else acc + part
    return 0.5 * acc


def _ffn_kernel(x_ref, g_ref, wg_ref, wu_ref, wd_ref, o_ref, *, sub):
    for r in range(x_ref.shape[0] // sub):
        rows = slice(r * sub, (r + 1) * sub)
        x = x_ref[rows, :]
        o_ref[rows, :] = x + _half_swiglu(x, g_ref[...], wg_ref, wu_ref, wd_ref)


def _mix_ffn_kernel(x_ref, oa_ref, ob_ref, oc_ref, woa_ref, wob_ref, woc_ref, g_ref, wg_ref, wu_ref,
                    wd_ref, gf_ref, o_ref, *, sub, final_norm):
    for r in range(x_ref.shape[0] // sub):
        rows = slice(r * sub, (r + 1) * sub)
        mix = jnp.dot(oa_ref[rows, :], woa_ref[...], preferred_element_type=F32)
        mix = mix + jnp.dot(ob_ref[rows, :], wob_ref[...], preferred_element_type=F32)
        mix = mix + jnp.dot(oc_ref[rows, :], woc_ref[...], preferred_element_type=F32)
        x2 = x_ref[rows, :] + mix
        y = x2 + _half_swiglu(x2, g_ref[...], wg_ref, wu_ref, wd_ref)
        if final_norm:
            y = _rms(y, gf_ref[...])
        o_ref[rows, :] = y


def _resident(shape, index_map):
    return pl.BlockSpec(shape, index_map, pipeline_mode=pl.Buffered(1))


def _ffn_weight_specs(layer):
    return [
        _resident((None, D_MODEL, D_FF), lambda i: (layer, 0, 0)),
        _resident((None, D_MODEL, D_FF), lambda i: (layer, 0, 0)),
        _resident((None, D_FF, D_MODEL), lambda i: (layer, 0, 0)),
    ]


def _ffn(x, g, wg, wu, wd, layer):
    n = x.shape[0]
    tm = min(FFN_TM, n)
    sub = min(FFN_SUB, tm)
    assert n % tm == 0 and tm % sub == 0, (n, tm)
    tok = pl.BlockSpec((tm, D_MODEL), lambda i: (i, 0))
    return pl.pallas_call(
        functools.partial(_ffn_kernel, sub=sub),
        grid=(n // tm,),
        in_specs=[tok, pl.BlockSpec((None, 1, D_MODEL), lambda i: (layer, 0, 0))] + _ffn_weight_specs(layer),
        out_specs=tok,
        out_shape=jax.ShapeDtypeStruct((n, D_MODEL), F32),
        compiler_params=_params(("parallel",)),
        name="ffn",
    )(x, g, wg, wu, wd)


def _mix_ffn(x, oa, ob, oc, w_out, g, wg, wu, wd, g_final, layer, final_norm):
    n = x.shape[0]
    tm = min(FFN_TM, n)
    sub = min(FFN_SUB, tm)
    assert n % tm == 0 and tm % sub == 0, (n, tm)
    tok = pl.BlockSpec((tm, D_MODEL), lambda i: (i, 0))
    in_specs = [
        tok,
        pl.BlockSpec((tm, MIX_A), lambda i: (i, 0)),
        pl.BlockSpec((tm, MIX_B), lambda i: (i, 0)),
        pl.BlockSpec((tm, MIX_C), lambda i: (i, 0)),
        _resident((None, MIX_A, D_MODEL), lambda i: (layer, 0, 0)),
        _resident((None, MIX_B, D_MODEL), lambda i: (layer, MIX_A // MIX_B, 0)),
        _resident((None, MIX_C, D_MODEL), lambda i: (layer, (MIX_A + MIX_B) // MIX_C, 0)),
        pl.BlockSpec((None, 1, D_MODEL), lambda i: (layer, 0, 0)),
    ] + _ffn_weight_specs(layer) + [pl.BlockSpec((1, D_MODEL), lambda i: (0, 0))]
    return pl.pallas_call(
        functools.partial(_mix_ffn_kernel, sub=sub, final_norm=final_norm),
        grid=(n // tm,),
        in_specs=in_specs,
        out_specs=tok,
        out_shape=jax.ShapeDtypeStruct((n, D_MODEL), F32),
        compiler_params=_params(("parallel",)),
        name="mix_ffn",
    )(x, oa, ob, oc, w_out, w_out, w_out, g, wg, wu, wd, g_final)


def _inproj_kernel(x_ref, g_ref, wa_ref, wbc_ref, convw_ref, cb0_ref,
                   qkv_ref, zba_ref, bc16_ref, bc32_ref, kvnew_ref, tail_ref, xbuf, *, nseq, tl, sub):
    j = pl.program_id(1)
    nslab = QKV_A // LANES

    @pl.when(j == 0)
    def _():
        for si in range(nseq):
            for sl in range(nslab):
                xbuf[si * nslab + sl, 0:SUBLANES, :] = cb0_ref[si, :, sl * LANES:(sl + 1) * LANES]

    if sub <= tl:
        pieces = [[(si, r0, sub)] for si in range(nseq) for r0 in range(0, tl, sub)]
    else:
        per = sub // tl
        pieces = [[(si, 0, tl) for si in range(g0, g0 + per)] for g0 in range(0, nseq, per)]
    for piece in pieces:
        xs = [x_ref[si, r0:r0 + ln, :] for si, r0, ln in piece]
        h = _rms(xs[0] if len(xs) == 1 else jnp.concatenate(xs, axis=0), g_ref[...]).astype(BF16)
        u = jnp.dot(h, wa_ref[...], preferred_element_type=F32)
        u_bc = jnp.dot(h, wbc_ref[...], preferred_element_type=F32)
        qc0, vc0, gc0 = 3 * MIX_B, 3 * MIX_B + 2 * MIX_C, 3 * MIX_B + 3 * MIX_C
        lo = 0
        for si, r0, ln in piece:
            rows, src = slice(r0, r0 + ln), slice(lo, lo + ln)
            zba_ref[si, rows, :] = u[src, QKV_A:]
            bc16_ref[si, rows, 0:qc0] = u_bc[src, 0:qc0].astype(BF16)
            bc16_ref[si, rows, qc0:] = u_bc[src, vc0:gc0].astype(BF16)
            bc32_ref[si, rows, 0:2 * MIX_C] = u_bc[src, qc0:vc0]
            bc32_ref[si, rows, 2 * MIX_C:] = u_bc[src, gc0:]
            kvnew_ref[si, rows, :] = u_bc[src, MIX_B:qc0]
            for sl in range(nslab):
                xbuf[si * nslab + sl, SUBLANES + r0:SUBLANES + r0 + ln, :] = u[lo:lo + ln, sl * LANES:(sl + 1) * LANES]
            for c0 in range(r0, r0 + ln, CHUNK):
                w0 = c0 + SUBLANES - (CONV_W - 1)
                for sl in range(nslab):
                    cols = slice(sl * LANES, (sl + 1) * LANES)
                    idx = si * nslab + sl
                    y = xbuf[idx, w0:w0 + CHUNK, :] * convw_ref[0:1, cols]
                    for tap in range(1, CONV_W):
                        y = y + xbuf[idx, w0 + tap:w0 + tap + CHUNK, :] * convw_ref[tap:tap + 1, cols]
                    y = _silu(y)
                    if sl < 2 * H_A:
                        y = y * lax.rsqrt(jnp.sum(y * y, axis=-1, keepdims=True) + EPS)
                        if sl < H_A:
                            y = y * DK_A ** -0.5
                    qkv_ref[si, c0:c0 + CHUNK, cols] = y.astype(BF16)
            lo += ln
    for idx in range(nseq * nslab):
        si, sl = divmod(idx, nslab)
        tail_ref[si, :, sl * LANES:(sl + 1) * LANES] = xbuf[idx, tl:tl + SUBLANES, :]
        xbuf[idx, 0:SUBLANES, :] = xbuf[idx, tl:tl + SUBLANES, :]


def _inproj(x, g, w_parts, conv_w, cb0, layer):
    b, t, _ = x.shape
    tl = min(PROJ_TM, t)
    nseq = max(1, min(b, PROJ_TM // tl))
    sub = min(PROJ_SUB, nseq * tl)
    assert b % nseq == 0 and t % tl == 0 and (tl % sub == 0 or sub % tl == 0) and tl % CHUNK == 0, (b, t)
    assert tl == min(BAND, t), (tl, t)
    seq_blk = lambda width: pl.BlockSpec((nseq, tl, width), lambda i, j: (i, j, 0))
    seq_shape = lambda width, dtype: jax.ShapeDtypeStruct((b, t, width), dtype)
    return pl.pallas_call(
        functools.partial(_inproj_kernel, nseq=nseq, tl=tl, sub=sub),
        grid=(b // nseq, t // tl),
        in_specs=[pl.BlockSpec((nseq, tl, D_MODEL), lambda i, j: (i, j, 0)),
                  pl.BlockSpec((None, 1, D_MODEL), lambda i, j: (layer, 0, 0)),
                  ] + [_resident((None,) + w.shape[1:], lambda i, j: (layer, 0, 0)) for w in w_parts] + [
                  pl.BlockSpec((CONV_W, QKV_A), lambda i, j: (0, 0)),
                  pl.BlockSpec((nseq, SUBLANES, QKV_A), lambda i, j: (i, 0, 0))],
        out_specs=[seq_blk(QKV_A), seq_blk(A_COLS - QKV_A), seq_blk(3 * MIX_B + MIX_C), seq_blk(3 * MIX_C),
                   pl.BlockSpec((nseq, tl, 2 * MIX_B), lambda i, j: (i, 0, 0)),
                   pl.BlockSpec((nseq, SUBLANES, QKV_A), lambda i, j: (i, 0, 0))],
        out_shape=[seq_shape(QKV_A, BF16), seq_shape(A_COLS - QKV_A, F32),
                   seq_shape(3 * MIX_B + MIX_C, BF16), seq_shape(3 * MIX_C, F32),
                   jax.ShapeDtypeStruct((b, tl, 2 * MIX_B), F32),
                   jax.ShapeDtypeStruct((b, SUBLANES, QKV_A), F32)],
        scratch_shapes=[pltpu.VMEM((nseq * QKV_A // LANES, tl + SUBLANES, LANES), F32)],
        compiler_params=_params(("parallel", "arbitrary")),
        name="inproj",
    )(x, g, *w_parts, conv_w, cb0)


def _gdn_kernel(u_ref, z_ref, ba_ref, prm_ref, s0_ref, oa_ref, sout_ref,
                dbuf, bbuf, val_scr, qk_scr, kdec_scr, attn_scr, s_scr, *, nb, tb):
    t = pl.program_id(1)
    nchunks = tb // CHUNK

    @pl.when(t == 0)
    def _():
        s_scr[...] = s0_ref[...].reshape(s_scr.shape)

    row = lax.broadcasted_iota(jnp.int32, (tb, tb), 0)
    col = lax.broadcasted_iota(jnp.int32, (tb, tb), 1)
    shift = CHUNK.bit_length() - 1
    same_chunk = jnp.right_shift(row, shift) == jnp.right_shift(col, shift)
    tri_bd = jnp.where(same_chunk & (col <= row), 1.0, 0.0)
    for bi in range(nb):
        ba = ba_ref[bi]
        x = ba + prm_ref[1:2, :]
        softplus = jnp.maximum(x, 0.0) + jnp.log1p(jnp.exp(-jnp.abs(x)))
        g = prm_ref[0:1, :] * softplus
        bbuf[bi] = jax.nn.sigmoid(ba)
        dbuf[bi] = _cumsum_rows(g, tri_bd)

    r_i = lax.broadcasted_iota(jnp.int32, (CHUNK, CHUNK), 0)
    c_i = lax.broadcasted_iota(jnp.int32, (CHUNK, CHUNK), 1)
    tril = c_i <= r_i
    strict = c_i < r_i
    eye = c_i == r_i
    eye_f = jnp.where(eye, 1.0, 0.0)
    gnorm = prm_ref[2:3, :]

    def pidx(bi, c, h):
        return (bi * nchunks + c) * H_A + h

    problems = [(bi, c, h) for bi in range(nb) for c in range(nchunks) for h in range(H_A)]
    lmask, qk, a_low = {}, {}, {}
    for bi in range(nb):
        for c in range(nchunks):
            rows = slice(c * CHUNK, (c + 1) * CHUNK)
            dch = dbuf[bi, rows, :]
            beta = bbuf[bi, rows, :]
            e_d = jnp.exp(dch)
            e_dec = jnp.exp(dch[CHUNK - 1:CHUNK, :] - dch)
            for h in range(H_A):
                p = pidx(bi, c, h)
                hcols = slice(h * DV_A, (h + 1) * DV_A)
                q = u_ref[bi, rows, h * DK_A:(h + 1) * DK_A].astype(F32)
                k = u_ref[bi, rows, MIX_A + h * DK_A:MIX_A + (h + 1) * DK_A].astype(F32)
                v = u_ref[bi, rows, 2 * MIX_A + h * DV_A:2 * MIX_A + (h + 1) * DV_A].astype(F32)
                dcol = dch[:, 4 + h:5 + h]
                bcol = beta[:, h:h + 1]
                edcol = e_d[:, 4 + h:5 + h]
                drow = jnp.sum(jnp.where(eye, dcol, 0.0), axis=0, keepdims=True)
                diff = dcol - drow
                lmask[bi, c, h] = jnp.where(tril, jnp.exp(jnp.where(tril, diff, 0.0)), 0.0)
                kb = k * bcol
                qk[bi, c, h] = _dot_nt(jnp.concatenate([q, kb], axis=0), k)
                val_scr[bi, rows, hcols] = v * bcol
                qk_scr[p, 0:CHUNK, :] = (q * edcol).astype(BF16)
                qk_scr[p, CHUNK:2 * CHUNK, :] = (kb * edcol).astype(BF16)
                kdec_scr[bi, rows, hcols] = (k * e_dec[:, 4 + h:5 + h]).astype(BF16)
    for pr in problems:
        attn_scr[pidx(*pr)] = (qk[pr][0:CHUNK] * lmask[pr]).astype(BF16)
        a_low[pr] = jnp.where(strict, qk[pr][CHUNK:2 * CHUNK] * lmask[pr], 0.0)
    pw = a_low
    inv = {pr: eye_f - a_low[pr] for pr in problems}
    for _ in range(5):
        pw = {pr: _dot(pw[pr], pw[pr]) for pr in problems}
        inv = {pr: inv[pr] + _dot(inv[pr], pw[pr]) for pr in problems}
    for bi, c, h in problems:
        p = pidx(bi, c, h)
        rows = slice(c * CHUNK, (c + 1) * CHUNK)
        hcols = slice(h * DV_A, (h + 1) * DV_A)
        rhs = jnp.concatenate([val_scr[bi, rows, hcols].astype(BF16), qk_scr[p, CHUNK:2 * CHUNK, :]], axis=1)
        sol = _dot(inv[bi, c, h], rhs)
        val_scr[bi, rows, hcols] = sol[:, :DV_A]
        qk_scr[p, CHUNK:2 * CHUNK, :] = sol[:, DV_A:].astype(BF16)

    chains = [(bi, h) for bi in range(nb) for h in range(H_A)]
    state = {ch: s_scr[ch[0] * H_A + ch[1]] for ch in chains}
    for c in range(nchunks):
        rows = slice(c * CHUNK, (c + 1) * CHUNK)
        qs = {(bi, h): _dot(qk_scr[pidx(bi, c, h)], state[bi, h]) for bi, h in chains}
        v_new = {(bi, h): val_scr[bi, rows, h * DV_A:(h + 1) * DV_A] - qs[bi, h][CHUNK:2 * CHUNK]
                 for bi, h in chains}
        o_in = {(bi, h): _dot(attn_scr[pidx(bi, c, h)], v_new[bi, h]) for bi, h in chains}
        ds = {(bi, h): _dot_tn(kdec_scr[bi, rows, h * DK_A:(h + 1) * DK_A], v_new[bi, h]) for bi, h in chains}
        for bi, h in chains:
            hcols = slice(h * DV_A, (h + 1) * DV_A)
            e_last = jnp.exp(dbuf[bi, (c + 1) * CHUNK - 1:(c + 1) * CHUNK, 4 + h:5 + h])
            o = qs[bi, h][0:CHUNK] + o_in[bi, h]
            state[bi, h] = state[bi, h] * e_last + ds[bi, h]
            oa_ref[bi, rows, hcols] = (_rms(o, gnorm) * _silu(z_ref[bi, rows, hcols])).astype(BF16)
    for bi, h in chains:
        s_scr[bi * H_A + h] = state[bi, h]

    @pl.when(t == pl.num_programs(1) - 1)
    def _():
        sout_ref[...] = s_scr[...].reshape(sout_ref.shape)


def _cumsum_rows(g, tri_bd):
    hi = g.astype(BF16)
    r1 = g - hi.astype(F32)
    mid = r1.astype(BF16)
    lo = (r1 - mid.astype(F32)).astype(BF16)
    tb = tri_bd.astype(BF16)
    out = jnp.dot(tb, hi, preferred_element_type=F32)
    out = out + jnp.dot(tb, mid, preferred_element_type=F32)
    return out + jnp.dot(tb, lo, preferred_element_type=F32)


def _gdn(u, zba, prm, s0, tb):
    b, t, _ = u.shape
    nb = max(1, min(b, GDN_PROBLEMS // (tb // CHUNK * H_A)))
    assert b % nb == 0, (b, nb)
    nprob = nb * (tb // CHUNK) * H_A
    return pl.pallas_call(
        functools.partial(_gdn_kernel, nb=nb, tb=tb),
        grid=(b // nb, t // tb),
        in_specs=[pl.BlockSpec((nb, tb, QKV_A), lambda i, j: (i, j, 0)),
                  pl.BlockSpec((nb, tb, MIX_A), lambda i, j: (i, j, 0)),
                  pl.BlockSpec((nb, tb, LANES), lambda i, j: (i, j, MIX_A // LANES)),
                  pl.BlockSpec((SUBLANES, LANES), lambda i, j: (0, 0)),
                  pl.BlockSpec((nb, H_A, DK_A, DV_A), lambda i, j: (i, 0, 0, 0))],
        out_specs=[pl.BlockSpec((nb, tb, MIX_A), lambda i, j: (i, j, 0)),
                   pl.BlockSpec((nb, H_A, DK_A, DV_A), lambda i, j: (i, 0, 0, 0))],
        out_shape=[jax.ShapeDtypeStruct((b, t, MIX_A), BF16),
                   jax.ShapeDtypeStruct((b, H_A, DK_A, DV_A), F32)],
        scratch_shapes=[pltpu.VMEM((nb, tb, LANES), F32), pltpu.VMEM((nb, tb, LANES), F32),
                        pltpu.VMEM((nb, tb, MIX_A), F32),
                        pltpu.VMEM((nprob, 2 * CHUNK, DK_A), BF16),
                        pltpu.VMEM((nb, tb, H_A * DK_A), BF16),
                        pltpu.VMEM((nprob, CHUNK, CHUNK), BF16),
                        pltpu.VMEM((nb * H_A, DK_A, DV_A), F32)],
        compiler_params=_params(("parallel", "arbitrary")),
        name="gdn",
    )(u, zba, zba, prm, s0)


def _band_kernel(q_ref, k_ref, v_ref, bias_ref, ck_ref, cv_ref, o_ref, kbuf, vbuf,
                 *, nb, qb, t_len, n_invalid):
    t = pl.program_id(1)

    @pl.when(t == 0)
    def _():
        for bi in range(nb):
            kbuf[bi, 0:BAND, :] = ck_ref[bi].astype(BF16)
            vbuf[bi, 0:BAND, :] = cv_ref[bi].astype(BF16)
            kbuf[bi, BAND:BAND + t_len, :] = k_ref[bi]
            vbuf[bi, BAND:BAND + t_len, :] = v_ref[bi]

    start = pl.multiple_of(t * qb, qb)
    lane = lax.broadcasted_iota(jnp.int32, (1, MIX_B), 1)
    if n_invalid:
        kpos = lax.broadcasted_iota(jnp.int32, (qb, qb + BAND), 1) + start
        valid = kpos >= n_invalid
    head = [(lane >= h * DH_B) & (lane < (h + 1) * DH_B) for h in range(H_B)]
    per_group = max(1, BAND_GROUP_ROWS // qb)
    head_sets = [range(H_B)] if per_group > 1 else [[h] for h in range(H_B)]
    groups = [(range(g0, min(g0 + per_group, nb)), hs) for g0 in range(0, nb, per_group) for hs in head_sets]
    acc = {bi: jnp.zeros((qb, MIX_B), F32) for bi in range(nb)}
    for seqs, heads in groups:
        chains = [(bi, h) for bi in seqs for h in heads]
        q = {bi: q_ref[bi] * DH_B ** -0.5 for bi in seqs}
        s = {(bi, h): _dot_nt(jnp.where(head[h], q[bi], 0.0), kbuf[bi, pl.ds(start, qb + BAND), :])
             + bias_ref[h] for bi, h in chains}
        p, inv_l = {}, {}
        for ch in chains:
            sc = jnp.where(valid, s[ch], NEG_INF) if n_invalid else s[ch]
            p[ch] = jnp.exp(sc - jnp.max(sc, axis=-1, keepdims=True))
            inv_l[ch] = 1.0 / jnp.sum(p[ch], axis=-1, keepdims=True)
        o = {(bi, h): _dot(p[bi, h], vbuf[bi, pl.ds(start, qb + BAND), :]) for bi, h in chains}
        for bi, h in chains:
            acc[bi] = acc[bi] + jnp.where(head[h], o[bi, h] * inv_l[bi, h], 0.0)
    for bi in range(nb):
        o_ref[bi] = acc[bi].astype(BF16)


def _band(u, bias, ck, cv, qb, n_invalid):
    b, t, _ = u.shape
    nb = max(1, min(b, MIX_ROWS // qb))
    assert b % nb == 0 and ck.shape[0] == b and u.dtype == BF16, (b, nb)
    return pl.pallas_call(
        functools.partial(_band_kernel, nb=nb, qb=qb, t_len=t, n_invalid=n_invalid),
        grid=(b // nb, t // qb),
        in_specs=[pl.BlockSpec((nb, qb, MIX_B), lambda i, j: (i, j, 0)),
                  pl.BlockSpec((nb, t, MIX_B), lambda i, j: (i, 0, 1)),
                  pl.BlockSpec((nb, t, MIX_B), lambda i, j: (i, 0, 2)),
                  pl.BlockSpec((H_B, qb, qb + BAND), lambda i, j: (0, 0, 0)),
                  pl.BlockSpec((nb, BAND, MIX_B), lambda i, j: (i, 0, 0)),
                  pl.BlockSpec((nb, BAND, MIX_B), lambda i, j: (i, 0, 0))],
        out_specs=pl.BlockSpec((nb, qb, MIX_B), lambda i, j: (i, j, 0)),
        out_shape=jax.ShapeDtypeStruct((b, t, MIX_B), BF16),
        scratch_shapes=[pltpu.VMEM((nb, BAND + t, MIX_B), BF16), pltpu.VMEM((nb, BAND + t, MIX_B), BF16)],
        compiler_params=_params(("parallel", "arbitrary")),
        name="band",
    )(u, u, u, bias, ck, cv)


def _band_bias_table(rel_bias, qb):
    rb = rel_bias.astype(F32)
    nh = rb.shape[0]
    w = qb + BAND
    p = w + qb - 1
    rel_lo, rel_hi = -(qb - 1) - BAND, qb - 1
    mid_hi = min(rel_hi, REL_CLIP)
    f = jnp.concatenate([jnp.broadcast_to(rb[:, :1], (nh, -REL_CLIP - rel_lo)),
                         rb[:, :mid_hi + REL_CLIP + 1],
                         jnp.broadcast_to(rb[:, -1:], (nh, rel_hi - mid_hi))], axis=1)
    rolled = jnp.tile(jnp.pad(f, ((0, 0), (0, 1))), (1, qb))[:, :qb * p].reshape(nh, qb, p)
    tab = rolled[:, :, qb - 1:qb - 1 + w]
    ci = np.arange(qb)[:, None] // CHUNK
    cj = np.arange(w)[None, :] // CHUNK
    return jnp.where((cj >= ci) & (cj <= ci + BAND_CHUNKS), tab, NEG_INF)


def _ret_kernel(q_ref, k_ref, v_ref, g_ref, rope_ref, pow_ref, dmask_ref, sq_ref, r0_ref, o_ref, rout_ref,
                r_scr, *, nb, tb):
    t = pl.program_id(1)

    @pl.when(t == 0)
    def _():
        r_scr[...] = r0_ref[...]

    def rope(x):
        return (x * rope_ref[0] + pltpu.roll(x, MIX_C - DK_C // 2, 1) * rope_ref[1]
                + pltpu.roll(x, DK_C // 2, 1) * rope_ref[2])

    lane = lax.broadcasted_iota(jnp.int32, (1, MIX_C), 1)
    avg = sq_ref[2]
    seqs = range(nb)
    chains = [(bi, h) for bi in seqs for h in range(H_C)]
    head = [(lane >= h * DK_C) & (lane < (h + 1) * DK_C) for h in range(H_C)]
    q = [rope(q_ref[bi]) for bi in seqs]
    k = [rope(k_ref[bi]) * DK_C ** -0.5 for bi in seqs]
    v = [v_ref[bi].astype(BF16) for bi in seqs]
    r = [r_scr[bi] for bi in seqs]
    cross = [_dot(q[bi] * pow_ref[0], r[bi]) for bi in seqs]
    s = {(bi, h): _dot_nt(jnp.where(head[h], q[bi], 0.0), k[bi]) * dmask_ref[h] for bi, h in chains}
    kv = [_dot_tn(k[bi] * pow_ref[1], v[bi]) for bi in seqs]
    inner = {(bi, h): _dot(s[bi, h], v[bi]) for bi, h in chains}
    acc = []
    for bi in seqs:
        a = cross[bi]
        for h in range(H_C):
            a = a + jnp.where(head[h], inner[bi, h], 0.0)
        acc.append(a)
        r_scr[bi] = r[bi] * sq_ref[0] + sq_ref[1] * kv[bi]
    mean = [_dot_exact_rhs(acc[bi], avg) for bi in seqs]
    xc = [acc[bi] - mean[bi] for bi in seqs]
    var = [_dot_exact_rhs(xc[bi] * xc[bi], avg) for bi in seqs]
    for bi in seqs:
        o_ref[bi] = ((xc[bi] * lax.rsqrt(var[bi] + EPS)) * _silu(g_ref[bi])).astype(BF16)

    @pl.when(t == pl.num_programs(1) - 1)
    def _():
        rout_ref[...] = r_scr[...]


def _ret(u32, u16, rope_tab, pow_tab, dmask, sq_tab, r0, tb):
    b, t, _ = u32.shape
    nb = max(1, min(b, MIX_ROWS // tb))
    assert b % nb == 0, (b, nb)
    ublk = lambda c: pl.BlockSpec((nb, tb, MIX_C), lambda i, j: (i, j, c))
    return pl.pallas_call(
        functools.partial(_ret_kernel, nb=nb, tb=tb),
        grid=(b // nb, t // tb),
        in_specs=[ublk(0), ublk(1), ublk(3 * MIX_B // MIX_C), ublk(2),
                  pl.BlockSpec((3, tb, MIX_C), lambda i, j: (0, j, 0)),
                  pl.BlockSpec((2, tb, MIX_C), lambda i, j: (0, 0, 0)),
                  pl.BlockSpec((H_C, tb, tb), lambda i, j: (0, 0, 0)),
                  pl.BlockSpec((3, MIX_C, MIX_C), lambda i, j: (0, 0, 0)),
                  pl.BlockSpec((nb, MIX_C, MIX_C), lambda i, j: (i, 0, 0))],
        out_specs=[pl.BlockSpec((nb, tb, MIX_C), lambda i, j: (i, j, 0)),
                   pl.BlockSpec((nb, MIX_C, MIX_C), lambda i, j: (i, 0, 0))],
        out_shape=[jax.ShapeDtypeStruct((b, t, MIX_C), BF16),
                   jax.ShapeDtypeStruct((b, MIX_C, MIX_C), F32)],
        scratch_shapes=[pltpu.VMEM((nb, MIX_C, MIX_C), F32)],
        compiler_params=_params(("parallel", "arbitrary")),
        name="ret",
    )(u32, u32, u16, u32, rope_tab, pow_tab, dmask, sq_tab, r0)


def _ret_tables(t_len, pos0, tb):
    half = DK_C // 2
    inv_freq = np.exp(-math.log(ROPE_BASE) * np.arange(half) / half)
    ang = (pos0 + np.arange(t_len))[:, None] * inv_freq[None, :]
    cos, sin = np.cos(ang), np.sin(ang)
    zero = np.zeros_like(sin)
    per_head = lambda a, b_: np.tile(np.concatenate([a, b_], axis=1), (1, H_C))
    rope_tab = np.stack([per_head(cos, cos), per_head(-sin, zero), per_head(zero, sin)])
    lg = np.log1p(-np.exp2(-5.0 - np.arange(H_C)))
    lg_lane = np.repeat(lg, DK_C)
    n = np.arange(tb, dtype=np.float64)
    pow_tab = np.stack([np.exp(lg_lane[None, :] * (n + 1.0)[:, None]),
                        np.exp(lg_lane[None, :] * (tb - 1.0 - n)[:, None])])
    diff = n[:, None] - n[None, :]
    causal = diff >= 0
    dmask = np.where(causal, np.exp(lg[:, None, None] * np.where(causal, diff, 0.0)), 0.0)
    head = np.arange(MIX_C) // DK_C
    same = (head[:, None] == head[None, :]).astype(np.float64)
    carry = np.broadcast_to(np.exp(lg_lane * tb)[:, None], (MIX_C, MIX_C))
    sq_tab = np.stack([carry, same, same / DV_C])
    return tuple(jnp.asarray(a, F32) for a in (rope_tab, pow_tab, dmask, sq_tab))


def _split_w_in(w_in):
    a_end = QKV_A + MIX_A + 2 * H_A
    w16 = w_in.astype(BF16)
    return jnp.pad(w16[..., :a_end], ((0, 0), (0, 0), (0, A_COLS - a_end))), w16[..., a_end:]


def _block_diag(r):
    b = r.shape[0]
    eye = jnp.eye(H_C, dtype=r.dtype)
    return jnp.einsum('bhkv,hg->bhkgv', r, eye).reshape(b, H_C * DK_C, H_C * DV_C)


def _diag_blocks(r):
    b = r.shape[0]
    r = r.reshape(b, H_C, DK_C, H_C, DV_C)
    return jnp.stack([r[:, h, :, h, :] for h in range(H_C)], axis=1)


def _run(x, pos0, caches, weights):
    (norm_ffn1, wg1, wu1, wd1, norm_mix, w_in, conv_w, a_log, dt_bias, gdn_norm, rel_bias, w_out,
     norm_ffn2, wg2, wu2, wd2, norm_final) = weights
    b, t, _ = x.shape
    n = b * t
    tb = min(MIX_TB, t)
    assert t % tb == 0 and tb % CHUNK == 0, (t, tb)
    rope_tab, pow_tab, dmask, sq_tab = _ret_tables(t, pos0, tb)
    gf = norm_final.reshape(1, D_MODEL)
    norm_ffn1, norm_mix, norm_ffn2 = (g.reshape(DEPTH, 1, D_MODEL) for g in (norm_ffn1, norm_mix, norm_ffn2))
    xf = x.reshape(n, D_MODEL)
    new = []
    for l in range(DEPTH):
        if caches is None:
            s_gdn0 = jnp.zeros((b, H_A, DK_A, DV_A), F32)
            cb0 = jnp.zeros((b, SUBLANES, QKV_A), F32)
            r0 = jnp.zeros((b, MIX_C, MIX_C), F32)
            ck = cv = jnp.zeros((b, BAND, MIX_B), F32)
            n_invalid = BAND
        else:
            state_gdn, state_conv, cache_k, cache_v, state_ret = caches
            s_gdn0 = state_gdn[l]
            cb0 = jnp.pad(state_conv[l], ((0, 0), (SUBLANES - (CONV_W - 1), 0), (0, 0)))
            r0 = _block_diag(state_ret[l])
            ck = cache_k[l].reshape(b, BAND, MIX_B)
            cv = cache_v[l].reshape(b, BAND, MIX_B)
            n_invalid = 0
        xf = _ffn(xf, norm_ffn1, wg1, wu1, wd1, l)
        qkv, zba, bc16, bc32, kv_new, conv_tail = _inproj(xf.reshape(b, t, D_MODEL), norm_mix, w_in,
                                                          conv_w[l], cb0, l)
        decay_lanes = lambda v: jnp.pad(v.astype(F32), (H_A, LANES - 2 * H_A))
        prm = jnp.concatenate([decay_lanes(-jnp.exp(a_log[l].astype(F32)))[None],
                               decay_lanes(dt_bias[l])[None], gdn_norm[l].astype(F32)[None],
                               jnp.zeros((SUBLANES - 3, LANES), F32)], axis=0)
        oa, s_gdn = _gdn(qkv, zba, prm, s_gdn0, tb)
        ob = _band(bc16, _band_bias_table(rel_bias[l], tb), ck, cv, tb, n_invalid)
        oc, r_new = _ret(bc32, bc16, rope_tab, pow_tab, dmask, sq_tab, r0, tb)
        xf = _mix_ffn(xf, oa.reshape(n, MIX_A), ob.reshape(n, MIX_B), oc.reshape(n, MIX_C), w_out,
                      norm_ffn2, wg2, wu2, wd2, gf, l, l == DEPTH - 1)
        conv_new = conv_tail[:, SUBLANES - (CONV_W - 1):, :]
        kb_new = kv_new[:, :, :MIX_B].reshape(b, -1, H_B, DH_B)
        vb_new = kv_new[:, :, MIX_B:].reshape(b, -1, H_B, DH_B)
        new.append((s_gdn, conv_new, kb_new, vb_new, _diag_blocks(r_new)))
    g_, c_, k_, v_, r_ = zip(*new)
    return (xf.reshape(b, t, D_MODEL), jnp.stack(g_), jnp.stack(c_), jnp.stack(k_), jnp.stack(v_),
            jnp.stack(r_))


def kernel(x_prompt, x_sample, state_gdn, state_conv, cache_band_k, cache_band_v, state_ret, norm_ffn1, w_ffn1_gate, w_ffn1_up, w_ffn1_down, norm_mix, w_in, conv_w, a_log, dt_bias, gdn_norm, rel_bias, w_out, norm_ffn2, w_ffn2_gate, w_ffn2_up, w_ffn2_down, norm_final):
    past_len = 1024
    weights = (norm_ffn1, w_ffn1_gate.astype(BF16), w_ffn1_up.astype(BF16), w_ffn1_down.astype(BF16),
               norm_mix, _split_w_in(w_in), conv_w, a_log, dt_bias, gdn_norm, rel_bias,
               w_out.astype(BF16), norm_ffn2, w_ffn2_gate.astype(BF16), w_ffn2_up.astype(BF16),
               w_ffn2_down.astype(BF16), norm_final)
    p = _run(x_prompt, 0, None, weights)
    s = _run(x_sample, past_len, (state_gdn, state_conv, cache_band_k, cache_band_v, state_ret), weights)
    return (p[0], s[0]) + p[1:] + s[1:]
```

```python
import functools
import math

import jax
import jax.numpy as jnp
import numpy as np
from jax import lax
from jax.experimental import pallas as pl
from jax.experimental.pallas import tpu as pltpu

F32 = jnp.float32
BF16 = jnp.bfloat16

D_MODEL = 1024
DEPTH = 2
CHUNK = 64
H_A, DK_A, DV_A, CONV_W = 4, 128, 128, 4
H_B, DH_B, BAND_CHUNKS, REL_CLIP = 4, 64, 8, 128
H_C, DK_C, DV_C = 4, 64, 64
ROPE_BASE = 10000.0
D_FF = 2816
EPS = 1e-6
NEG_INF = -1e30
BAND = BAND_CHUNKS * CHUNK
QKV_A = 3 * H_A * DK_A
MIX_A = H_A * DV_A
MIX_B = H_B * DH_B
MIX_C = H_C * DV_C
LANES = 128
SUBLANES = 8
A_COLS = QKV_A + MIX_A + LANES
VMEM_LIMIT = 56 * 1024 * 1024

FFN_TM = 1024
FFN_SUB = 512
FFN_TF = 256
PROJ_TM = 512
PROJ_SUB = 256
MIX_TB = 256
GDN_PROBLEMS = 64
MIX_ROWS = 1024
BAND_GROUP_ROWS = 256


def _params(sem):
    return pltpu.CompilerParams(dimension_semantics=sem, vmem_limit_bytes=VMEM_LIMIT)


def _rms(x, g):
    return (x * lax.rsqrt(jnp.mean(x * x, axis=-1, keepdims=True) + EPS)) * g


def _silu(x):
    return x * jax.nn.sigmoid(x)


def _dot(a, b):
    return jnp.dot(a.astype(BF16), b.astype(BF16), preferred_element_type=F32)


def _dot_nt(a, b):
    return lax.dot_general(a.astype(BF16), b.astype(BF16), (((1,), (1,)), ((), ())),
                           preferred_element_type=F32)


def _dot_tn(a, b):
    return lax.dot_general(a.astype(BF16), b.astype(BF16), (((0,), (0,)), ((), ())),
                           preferred_element_type=F32)


def _dot_exact_rhs(a, b):
    hi = a.astype(BF16)
    r1 = a - hi.astype(F32)
    mid = r1.astype(BF16)
    lo = (r1 - mid.astype(F32)).astype(BF16)
    bb = b.astype(BF16)
    out = jnp.dot(hi, bb, preferred_element_type=F32)
    out = out + jnp.dot(mid, bb, preferred_element_type=F32)
    return out + jnp.dot(lo, bb, preferred_element_type=F32)


def _half_swiglu(x, g, wg_ref, wu_ref, wd_ref):
    h = _rms(x, g).astype(BF16)
    acc = None
    for s in range(D_FF // FFN_TF):
        cols = slice(s * FFN_TF, (s + 1) * FFN_TF)
        gate = jnp.dot(h, wg_ref[:, cols], preferred_element_type=F32)
        up = jnp.dot(h, wu_ref[:, cols], preferred_element_type=F32)
        act = (_silu(gate) * up).astype(BF16)
        part = jnp.dot(act, wd_ref[cols, :], preferred_element_type=F32)
        acc = part if acc is None else acc + part
    return 0.5 * acc


def _ffn_kernel(x_ref, g_ref, wg_ref, wu_ref, wd_ref, o_ref, *, sub):
    for r in range(x_ref.shape[0] // sub):
        rows = slice(r * sub, (r + 1) * sub)
        x = x_ref[rows, :]
        o_ref[rows, :] = x + _half_swiglu(x, g_ref[...], wg_ref, wu_ref, wd_ref)


def _mix_ffn_kernel(x_ref, oa_ref, ob_ref, oc_ref, woa_ref, wob_ref, woc_ref, g_ref, wg_ref, wu_ref,
                    wd_ref, gf_ref, o_ref, *, sub, final_norm):
    for r in range(x_ref.shape[0] // sub):
        rows = slice(r * sub, (r + 1) * sub)
        mix = jnp.dot(oa_ref[rows, :], woa_ref[...], preferred_element_type=F32)
        mix = mix + jnp.dot(ob_ref[rows, :], wob_ref[...], preferred_element_type=F32)
        mix = mix + jnp.dot(oc_ref[rows, :], woc_ref[...], preferred_element_type=F32)
        x2 = x_ref[rows, :] + mix
        y = x2 + _half_swiglu(x2, g_ref[...], wg_ref, wu_ref, wd_ref)
        if final_norm:
            y = _rms(y, gf_ref[...])
        o_ref[rows, :] = y


def _resident(shape, index_map):
    return pl.BlockSpec(shape, index_map, pipeline_mode=pl.Buffered(1))


def _ffn_weight_specs(layer):
    return [
        _resident((None, D_MODEL, D_FF), lambda i: (layer, 0, 0)),
        _resident((None, D_MODEL, D_FF), lambda i: (layer, 0, 0)),
        _resident((None, D_FF, D_MODEL), lambda i: (layer, 0, 0)),
    ]


def _ffn(x, g, wg, wu, wd, layer):
    n = x.shape[0]
    tm = min(FFN_TM, n)
    sub = min(FFN_SUB, tm)
    assert n % tm == 0 and tm % sub == 0, (n, tm)
    tok = pl.BlockSpec((tm, D_MODEL), lambda i: (i, 0))
    return pl.pallas_call(
        functools.partial(_ffn_kernel, sub=sub),
        grid=(n // tm,),
        in_specs=[tok, pl.BlockSpec((None, 1, D_MODEL), lambda i: (layer, 0, 0))] + _ffn_weight_specs(layer),
        out_specs=tok,
        out_shape=jax.ShapeDtypeStruct((n, D_MODEL), F32),
        compiler_params=_params(("parallel",)),
        name="ffn",
    )(x, g, wg, wu, wd)


def _mix_ffn(x, oa, ob, oc, w_out, g, wg, wu, wd, g_final, layer, final_norm):
    n = x.shape[0]
    tm = min(FFN_TM, n)
    sub = min(FFN_SUB, tm)
    assert n % tm == 0 and tm % sub == 0, (n, tm)
    tok = pl.BlockSpec((tm, D_MODEL), lambda i: (i, 0))
    in_specs = [
        tok,
        pl.BlockSpec((tm, MIX_A), lambda i: (i, 0)),
        pl.BlockSpec((tm, MIX_B), lambda i: (i, 0)),
        pl.BlockSpec((tm, MIX_C), lambda i: (i, 0)),
        _resident((None, MIX_A, D_MODEL), lambda i: (layer, 0, 0)),
        _resident((None, MIX_B, D_MODEL), lambda i: (layer, MIX_A // MIX_B, 0)),
        _resident((None, MIX_C, D_MODEL), lambda i: (layer, (MIX_A + MIX_B) // MIX_C, 0)),
        pl.BlockSpec((None, 1, D_MODEL), lambda i: (layer, 0, 0)),
    ] + _ffn_weight_specs(layer) + [pl.BlockSpec((1, D_MODEL), lambda i: (0, 0))]
    return pl.pallas_call(
        functools.partial(_mix_ffn_kernel, sub=sub, final_norm=final_norm),
        grid=(n // tm,),
        in_specs=in_specs,
        out_specs=tok,
        out_shape=jax.ShapeDtypeStruct((n, D_MODEL), F32),
        compiler_params=_params(("parallel",)),
        name="mix_ffn",
    )(x, oa, ob, oc, w_out, w_out, w_out, g, wg, wu, wd, g_final)


def _inproj_kernel(x_ref, g_ref, wa_ref, wbc_ref, convw_ref, cb0_ref,
                   qkv_ref, zba_ref, bc16_ref, bc32_ref, kvnew_ref, tail_ref, xbuf, *, nseq, tl, sub):
    j = pl.program_id(1)
    nslab = QKV_A // LANES

    @pl.when(j == 0)
    def _():
        for si in range(nseq):
            for sl in range(nslab):
                xbuf[si * nslab + sl, 0:SUBLANES, :] = cb0_ref[si, :, sl * LANES:(sl + 1) * LANES]

    if sub <= tl:
        pieces = [[(si, r0, sub)] for si in range(nseq) for r0 in range(0, tl, sub)]
    else:
        per = sub // tl
        pieces = [[(si, 0, tl) for si in range(g0, g0 + per)] for g0 in range(0, nseq, per)]
    for piece in pieces:
        xs = [x_ref[si, r0:r0 + ln, :] for si, r0, ln in piece]
        h = _rms(xs[0] if len(xs) == 1 else jnp.concatenate(xs, axis=0), g_ref[...]).astype(BF16)
        u = jnp.dot(h, wa_ref[...], preferred_element_type=F32)
        u_bc = jnp.dot(h, wbc_ref[...], preferred_element_type=F32)
        qc0, vc0, gc0 = 3 * MIX_B, 3 * MIX_B + 2 * MIX_C, 3 * MIX_B + 3 * MIX_C
        lo = 0
        for si, r0, ln in piece:
            rows, src = slice(r0, r0 + ln), slice(lo, lo + ln)
            zba_ref[si, rows, :] = u[src, QKV_A:]
            bc16_ref[si, rows, 0:qc0] = u_bc[src, 0:qc0].astype(BF16)
            bc16_ref[si, rows, qc0:] = u_bc[src, vc0:gc0].astype(BF16)
            bc32_ref[si, rows, 0:2 * MIX_C] = u_bc[src, qc0:vc0]
            bc32_ref[si, rows, 2 * MIX_C:] = u_bc[src, gc0:]
            kvnew_ref[si, rows, :] = u_bc[src, MIX_B:qc0]
            for sl in range(nslab):
                xbuf[si * nslab + sl, SUBLANES + r0:SUBLANES + r0 + ln, :] = u[lo:lo + ln, sl * LANES:(sl + 1) * LANES]
            for c0 in range(r0, r0 + ln, CHUNK):
                w0 = c0 + SUBLANES - (CONV_W - 1)
                for sl in range(nslab):
                    cols = slice(sl * LANES, (sl + 1) * LANES)
                    idx = si * nslab + sl
                    y = xbuf[idx, w0:w0 + CHUNK, :] * convw_ref[0:1, cols]
                    for tap in range(1, CONV_W):
                        y = y + xbuf[idx, w0 + tap:w0 + tap + CHUNK, :] * convw_ref[tap:tap + 1, cols]
                    y = _silu(y)
                    if sl < 2 * H_A:
                        y = y * lax.rsqrt(jnp.sum(y * y, axis=-1, keepdims=True) + EPS)
                        if sl < H_A:
                            y = y * DK_A ** -0.5
                    qkv_ref[si, c0:c0 + CHUNK, cols] = y.astype(BF16)
            lo += ln
    for idx in range(nseq * nslab):
        si, sl = divmod(idx, nslab)
        tail_ref[si, :, sl * LANES:(sl + 1) * LANES] = xbuf[idx, tl:tl + SUBLANES, :]
        xbuf[idx, 0:SUBLANES, :] = xbuf[idx, tl:tl + SUBLANES, :]


def _inproj(x, g, w_parts, conv_w, cb0, layer):
    b, t, _ = x.shape
    tl = min(PROJ_TM, t)
    nseq = max(1, min(b, PROJ_TM // tl))
    sub = min(PROJ_SUB, nseq * tl)
    assert b % nseq == 0 and t % tl == 0 and (tl % sub == 0 or sub % tl == 0) and tl % CHUNK == 0, (b, t)
    assert tl == min(BAND, t), (tl, t)
    seq_blk = lambda width: pl.BlockSpec((nseq, tl, width), lambda i, j: (i, j, 0))
    seq_shape = lambda width, dtype: jax.ShapeDtypeStruct((b, t, width), dtype)
    return pl.pallas_call(
        functools.partial(_inproj_kernel, nseq=nseq, tl=tl, sub=sub),
        grid=(b // nseq, t // tl),
        in_specs=[pl.BlockSpec((nseq, tl, D_MODEL), lambda i, j: (i, j, 0)),
                  pl.BlockSpec((None, 1, D_MODEL), lambda i, j: (layer, 0, 0)),
                  ] + [_resident((None,) + w.shape[1:], lambda i, j: (layer, 0, 0)) for w in w_parts] + [
                  pl.BlockSpec((CONV_W, QKV_A), lambda i, j: (0, 0)),
                  pl.BlockSpec((nseq, SUBLANES, QKV_A), lambda i, j: (i, 0, 0))],
        out_specs=[seq_blk(QKV_A), seq_blk(A_COLS - QKV_A), seq_blk(3 * MIX_B + MIX_C), seq_blk(3 * MIX_C),
                   pl.BlockSpec((nseq, tl, 2 * MIX_B), lambda i, j: (i, 0, 0)),
                   pl.BlockSpec((nseq, SUBLANES, QKV_A), lambda i, j: (i, 0, 0))],
        out_shape=[seq_shape(QKV_A, BF16), seq_shape(A_COLS - QKV_A, F32),
                   seq_shape(3 * MIX_B + MIX_C, BF16), seq_shape(3 * MIX_C, F32),
                   jax.ShapeDtypeStruct((b, tl, 2 * MIX_B), F32),
                   jax.ShapeDtypeStruct((b, SUBLANES, QKV_A), F32)],
        scratch_shapes=[pltpu.VMEM((nseq * QKV_A // LANES, tl + SUBLANES, LANES), F32)],
        compiler_params=_params(("parallel", "arbitrary")),
        name="inproj",
    )(x, g, *w_parts, conv_w, cb0)


def _gdn_kernel(u_ref, z_ref, ba_ref, prm_ref, s0_ref, oa_ref, sout_ref,
                dbuf, bbuf, val_scr, qk_scr, kdec_scr, attn_scr, s_scr, *, nb, tb):
    t = pl.program_id(1)
    nchunks = tb // CHUNK

    @pl.when(t == 0)
    def _():
        s_scr[...] = s0_ref[...].reshape(s_scr.shape)

    row = lax.broadcasted_iota(jnp.int32, (tb, tb), 0)
    col = lax.broadcasted_iota(jnp.int32, (tb, tb), 1)
    shift = CHUNK.bit_length() - 1
    same_chunk = jnp.right_shift(row, shift) == jnp.right_shift(col, shift)
    tri_bd = jnp.where(same_chunk & (col <= row), 1.0, 0.0)
    for bi in range(nb):
        ba = ba_ref[bi]
        x = ba + prm_ref[1:2, :]
        softplus = jnp.maximum(x, 0.0) + jnp.log1p(jnp.exp(-jnp.abs(x)))
        g = prm_ref[0:1, :] * softplus
        bbuf[bi] = jax.nn.sigmoid(ba)
        dbuf[bi] = _cumsum_rows(g, tri_bd)

    r_i = lax.broadcasted_iota(jnp.int32, (CHUNK, CHUNK), 0)
    c_i = lax.broadcasted_iota(jnp.int32, (CHUNK, CHUNK), 1)
    tril = c_i <= r_i
    strict = c_i < r_i
    eye = c_i == r_i
    eye_f = jnp.where(eye, 1.0, 0.0)
    gnorm = prm_ref[2:3, :]

    def pidx(bi, c, h):
        return (bi * nchunks + c) * H_A + h

    problems = [(bi, c, h) for bi in range(nb) for c in range(nchunks) for h in range(H_A)]
    lmask, qk, a_low = {}, {}, {}
    for bi in range(nb):
        for c in range(nchunks):
            rows = slice(c * CHUNK, (c + 1) * CHUNK)
            dch = dbuf[bi, rows, :]
            beta = bbuf[bi, rows, :]
            e_d = jnp.exp(dch)
            e_dec = jnp.exp(dch[CHUNK - 1:CHUNK, :] - dch)
            for h in range(H_A):
                p = pidx(bi, c, h)
                hcols = slice(h * DV_A, (h + 1) * DV_A)
                q = u_ref[bi, rows, h * DK_A:(h + 1) * DK_A].astype(F32)
                k = u_ref[bi, rows, MIX_A + h * DK_A:MIX_A + (h + 1) * DK_A].astype(F32)
                v = u_ref[bi, rows, 2 * MIX_A + h * DV_A:2 * MIX_A + (h + 1) * DV_A].astype(F32)
                dcol = dch[:, 4 + h:5 + h]
                bcol = beta[:, h:h + 1]
                edcol = e_d[:, 4 + h:5 + h]
                drow = jnp.sum(jnp.where(eye, dcol, 0.0), axis=0, keepdims=True)
                diff = dcol - drow
                lmask[bi, c, h] = jnp.where(tril, jnp.exp(jnp.where(tril, diff, 0.0)), 0.0)
                kb = k * bcol
                qk[bi, c, h] = _dot_nt(jnp.concatenate([q, kb], axis=0), k)
                val_scr[bi, rows, hcols] = v * bcol
                qk_scr[p, 0:CHUNK, :] = (q * edcol).astype(BF16)
                qk_scr[p, CHUNK:2 * CHUNK, :] = (kb * edcol).astype(BF16)
                kdec_scr[bi, rows, hcols] = (k * e_dec[:, 4 + h:5 + h]).astype(BF16)
    for pr in problems:
        attn_scr[pidx(*pr)] = (qk[pr][0:CHUNK] * lmask[pr]).astype(BF16)
        a_low[pr] = jnp.where(strict, qk[pr][CHUNK:2 * CHUNK] * lmask[pr], 0.0)
    pw = a_low
    inv = {pr: eye_f - a_low[pr] for pr in problems}
    for _ in range(5):
        pw = {pr: _dot(pw[pr], pw[pr]) for pr in problems}
        inv = {pr: inv[pr] + _dot(inv[pr], pw[pr]) for pr in problems}
    for bi, c, h in problems:
        p = pidx(bi, c, h)
        rows = slice(c * CHUNK, (c + 1) * CHUNK)
        hcols = slice(h * DV_A, (h + 1) * DV_A)
        rhs = jnp.concatenate([val_scr[bi, rows, hcols].astype(BF16), qk_scr[p, CHUNK:2 * CHUNK, :]], axis=1)
        sol = _dot(inv[bi, c, h], rhs)
        val_scr[bi, rows, hcols] = sol[:, :DV_A]
        qk_scr[p, CHUNK:2 * CHUNK, :] = sol[:, DV_A:].astype(BF16)

    chains = [(bi, h) for bi in range(nb) for h in range(H_A)]
    state = {ch: s_scr[ch[0] * H_A + ch[1]] for ch in chains}
    for c in range(nchunks):
        rows = slice(c * CHUNK, (c + 1) * CHUNK)
        qs = {(bi, h): _dot(qk_scr[pidx(bi, c, h)], state[bi, h]) for bi, h in chains}
        v_new = {(bi, h): val_scr[bi, rows, h * DV_A:(h + 1) * DV_A] - qs[bi, h][CHUNK:2 * CHUNK]
                 for bi, h in chains}
        o_in = {(bi, h): _dot(attn_scr[pidx(bi, c, h)], v_new[bi, h]) for bi, h in chains}
        ds = {(bi, h): _dot_tn(kdec_scr[bi, rows, h * DK_A:(h + 1) * DK_A], v_new[bi, h]) for bi, h in chains}
        for bi, h in chains:
            hcols = slice(h * DV_A, (h + 1) * DV_A)
            e_last = jnp.exp(dbuf[bi, (c + 1) * CHUNK - 1:(c + 1) * CHUNK, 4 + h:5 + h])
            o = qs[bi, h][0:CHUNK] + o_in[bi, h]
            state[bi, h] = state[bi, h] * e_last + ds[bi, h]
            oa_ref[bi, rows, hcols] = (_rms(o, gnorm) * _silu(z_ref[bi, rows, hcols])).astype(BF16)
    for bi, h in chains:
        s_scr[bi * H_A + h] = state[bi, h]

    @pl.when(t == pl.num_programs(1) - 1)
    def _():
        sout_ref[...] = s_scr[...].reshape(sout_ref.shape)


def _cumsum_rows(g, tri_bd):
    hi = g.astype(BF16)
    r1 = g - hi.astype(F32)
    mid = r1.astype(BF16)
    lo = (r1 - mid.astype(F32)).astype(BF16)
    tb = tri_bd.astype(BF16)
    out = jnp.dot(tb, hi, preferred_element_type=F32)
    out = out + jnp.dot(tb, mid, preferred_element_type=F32)
    return out + jnp.dot(tb, lo, preferred_element_type=F32)


def _gdn(u, zba, prm, s0, tb):
    b, t, _ = u.shape
    nb = max(1, min(b, GDN_PROBLEMS // (tb // CHUNK * H_A)))
    assert b % nb == 0, (b, nb)
    nprob = nb * (tb // CHUNK) * H_A
    return pl.pallas_call(
        functools.partial(_gdn_kernel, nb=nb, tb=tb),
        grid=(b // nb, t // tb),
        in_specs=[pl.BlockSpec((nb, tb, QKV_A), lambda i, j: (i, j, 0)),
                  pl.BlockSpec((nb, tb, MIX_A), lambda i, j: (i, j, 0)),
                  pl.BlockSpec((nb, tb, LANES), lambda i, j: (i, j, MIX_A // LANES)),
                  pl.BlockSpec((SUBLANES, LANES), lambda i, j: (0, 0)),
                  pl.BlockSpec((nb, H_A, DK_A, DV_A), lambda i, j: (i, 0, 0, 0))],
        out_specs=[pl.BlockSpec((nb, tb, MIX_A), lambda i, j: (i, j, 0)),
                   pl.BlockSpec((nb, H_A, DK_A, DV_A), lambda i, j: (i, 0, 0, 0))],
        out_shape=[jax.ShapeDtypeStruct((b, t, MIX_A), BF16),
                   jax.ShapeDtypeStruct((b, H_A, DK_A, DV_A), F32)],
        scratch_shapes=[pltpu.VMEM((nb, tb, LANES), F32), pltpu.VMEM((nb, tb, LANES), F32),
                        pltpu.VMEM((nb, tb, MIX_A), F32),
                        pltpu.VMEM((nprob, 2 * CHUNK, DK_A), BF16),
                        pltpu.VMEM((nb, tb, H_A * DK_A), BF16),
                        pltpu.VMEM((nprob, CHUNK, CHUNK), BF16),
                        pltpu.VMEM((nb * H_A, DK_A, DV_A), F32)],
        compiler_params=_params(("parallel", "arbitrary")),
        name="gdn",
    )(u, zba, zba, prm, s0)


def _band_kernel(q_ref, k_ref, v_ref, bias_ref, ck_ref, cv_ref, o_ref, kbuf, vbuf,
                 *, nb, qb, t_len, n_invalid):
    t = pl.program_id(1)

    @pl.when(t == 0)
    def _():
        for bi in range(nb):
            kbuf[bi, 0:BAND, :] = ck_ref[bi].astype(BF16)
            vbuf[bi, 0:BAND, :] = cv_ref[bi].astype(BF16)
            kbuf[bi, BAND:BAND + t_len, :] = k_ref[bi]
            vbuf[bi, BAND:BAND + t_len, :] = v_ref[bi]

    start = pl.multiple_of(t * qb, qb)
    lane = lax.broadcasted_iota(jnp.int32, (1, MIX_B), 1)
    if n_invalid:
        kpos = lax.broadcasted_iota(jnp.int32, (qb, qb + BAND), 1) + start
        valid = kpos >= n_invalid
    head = [(lane >= h * DH_B) & (lane < (h + 1) * DH_B) for h in range(H_B)]
    per_group = max(1, BAND_GROUP_ROWS // qb)
    head_sets = [range(H_B)] if per_group > 1 else [[h] for h in range(H_B)]
    groups = [(range(g0, min(g0 + per_group, nb)), hs) for g0 in range(0, nb, per_group) for hs in head_sets]
    acc = {bi: jnp.zeros((qb, MIX_B), F32) for bi in range(nb)}
    for seqs, heads in groups:
        chains = [(bi, h) for bi in seqs for h in heads]
        q = {bi: q_ref[bi] * DH_B ** -0.5 for bi in seqs}
        s = {(bi, h): _dot_nt(jnp.where(head[h], q[bi], 0.0), kbuf[bi, pl.ds(start, qb + BAND), :])
             + bias_ref[h] for bi, h in chains}
        p, inv_l = {}, {}
        for ch in chains:
            sc = jnp.where(valid, s[ch], NEG_INF) if n_invalid else s[ch]
            p[ch] = jnp.exp(sc - jnp.max(sc, axis=-1, keepdims=True))
            inv_l[ch] = 1.0 / jnp.sum(p[ch], axis=-1, keepdims=True)
        o = {(bi, h): _dot(p[bi, h], vbuf[bi, pl.ds(start, qb + BAND), :]) for bi, h in chains}
        for bi, h in chains:
            acc[bi] = acc[bi] + jnp.where(head[h], o[bi, h] * inv_l[bi, h], 0.0)
    for bi in range(nb):
        o_ref[bi] = acc[bi].astype(BF16)


def _band(u, bias, ck, cv, qb, n_invalid):
    b, t, _ = u.shape
    nb = max(1, min(b, MIX_ROWS // qb))
    assert b % nb == 0 and ck.shape[0] == b and u.dtype == BF16, (b, nb)
    return pl.pallas_call(
        functools.partial(_band_kernel, nb=nb, qb=qb, t_len=t, n_invalid=n_invalid),
        grid=(b // nb, t // qb),
        in_specs=[pl.BlockSpec((nb, qb, MIX_B), lambda i, j: (i, j, 0)),
                  pl.BlockSpec((nb, t, MIX_B), lambda i, j: (i, 0, 1)),
                  pl.BlockSpec((nb, t, MIX_B), lambda i, j: (i, 0, 2)),
                  pl.BlockSpec((H_B, qb, qb + BAND), lambda i, j: (0, 0, 0)),
                  pl.BlockSpec((nb, BAND, MIX_B), lambda i, j: (i, 0, 0)),
                  pl.BlockSpec((nb, BAND, MIX_B), lambda i, j: (i, 0, 0))],
        out_specs=pl.BlockSpec((nb, qb, MIX_B), lambda i, j: (i, j, 0)),
        out_shape=jax.ShapeDtypeStruct((b, t, MIX_B), BF16),
        scratch_shapes=[pltpu.VMEM((nb, BAND + t, MIX_B), BF16), pltpu.VMEM((nb, BAND + t, MIX_B), BF16)],
        compiler_params=_params(("parallel", "arbitrary")),
        name="band",
    )(u, u, u, bias, ck, cv)


def _band_bias_table(rel_bias, qb):
    rb = rel_bias.astype(F32)
    nh = rb.shape[0]
    w = qb + BAND
    p = w + qb - 1
    rel_lo, rel_hi = -(qb - 1) - BAND, qb - 1
    mid_hi = min(rel_hi, REL_CLIP)
    f = jnp.concatenate([jnp.broadcast_to(rb[:, :1], (nh, -REL_CLIP - rel_lo)),
                         rb[:, :mid_hi + REL_CLIP + 1],
                         jnp.broadcast_to(rb[:, -1:], (nh, rel_hi - mid_hi))], axis=1)
    rolled = jnp.tile(jnp.pad(f, ((0, 0), (0, 1))), (1, qb))[:, :qb * p].reshape(nh, qb, p)
    tab = rolled[:, :, qb - 1:qb - 1 + w]
    ci = np.arange(qb)[:, None] // CHUNK
    cj = np.arange(w)[None, :] // CHUNK
    return jnp.where((cj >= ci) & (cj <= ci + BAND_CHUNKS), tab, NEG_INF)


def _ret_kernel(q_ref, k_ref, v_ref, g_ref, rope_ref, pow_ref, dmask_ref, sq_ref, r0_ref, o_ref, rout_ref,
                r_scr, *, nb, tb):
    t = pl.program_id(1)

    @pl.when(t == 0)
    def _():
        r_scr[...] = r0_ref[...]

    def rope(x):
        return (x * rope_ref[0] + pltpu.roll(x, MIX_C - DK_C // 2, 1) * rope_ref[1]
                + pltpu.roll(x, DK_C // 2, 1) * rope_ref[2])

    lane = lax.broadcasted_iota(jnp.int32, (1, MIX_C), 1)
    avg = sq_ref[2]
    seqs = range(nb)
    chains = [(bi, h) for bi in seqs for h in range(H_C)]
    head = [(lane >= h * DK_C) & (lane < (h + 1) * DK_C) for h in range(H_C)]
    q = [rope(q_ref[bi]) for bi in seqs]
    k = [rope(k_ref[bi]) * DK_C ** -0.5 for bi in seqs]
    v = [v_ref[bi].astype(BF16) for bi in seqs]
    r = [r_scr[bi] for bi in seqs]
    cross = [_dot(q[bi] * pow_ref[0], r[bi]) for bi in seqs]
    s = {(bi, h): _dot_nt(jnp.where(head[h], q[bi], 0.0), k[bi]) * dmask_ref[h] for bi, h in chains}
    kv = [_dot_tn(k[bi] * pow_ref[1], v[bi]) for bi in seqs]
    inner = {(bi, h): _dot(s[bi, h], v[bi]) for bi, h in chains}
    acc = []
    for bi in seqs:
        a = cross[bi]
        for h in range(H_C):
            a = a + jnp.where(head[h], inner[bi, h], 0.0)
        acc.append(a)
        r_scr[bi] = r[bi] * sq_ref[0] + sq_ref[1] * kv[bi]
    mean = [_dot_exact_rhs(acc[bi], avg) for bi in seqs]
    xc = [acc[bi] - mean[bi] for bi in seqs]
    var = [_dot_exact_rhs(xc[bi] * xc[bi], avg) for bi in seqs]
    for bi in seqs:
        o_ref[bi] = ((xc[bi] * lax.rsqrt(var[bi] + EPS)) * _silu(g_ref[bi])).astype(BF16)

    @pl.when(t == pl.num_programs(1) - 1)
    def _():
        rout_ref[...] = r_scr[...]


def _ret(u32, u16, rope_tab, pow_tab, dmask, sq_tab, r0, tb):
    b, t, _ = u32.shape
    nb = max(1, min(b, MIX_ROWS // tb))
    assert b % nb == 0, (b, nb)
    ublk = lambda c: pl.BlockSpec((nb, tb, MIX_C), lambda i, j: (i, j, c))
    return pl.pallas_call(
        functools.partial(_ret_kernel, nb=nb, tb=tb),
        grid=(b // nb, t // tb),
        in_specs=[ublk(0), ublk(1), ublk(3 * MIX_B // MIX_C), ublk(2),
                  pl.BlockSpec((3, tb, MIX_C), lambda i, j: (0, j, 0)),
                  pl.BlockSpec((2, tb, MIX_C), lambda i, j: (0, 0, 0)),
                  pl.BlockSpec((H_C, tb, tb), lambda i, j: (0, 0, 0)),
                  pl.BlockSpec((3, MIX_C, MIX_C), lambda i, j: (0, 0, 0)),
                  pl.BlockSpec((nb, MIX_C, MIX_C), lambda i, j: (i, 0, 0))],
        out_specs=[pl.BlockSpec((nb, tb, MIX_C), lambda i, j: (i, j, 0)),
                   pl.BlockSpec((nb, MIX_C, MIX_C), lambda i, j: (i, 0, 0))],
        out_shape=[jax.ShapeDtypeStruct((b, t, MIX_C), BF16),
                   jax.ShapeDtypeStruct((b, MIX_C, MIX_C), F32)],
        scratch_shapes=[pltpu.VMEM((nb, MIX_C, MIX_C), F32)],
        compiler_params=_params(("parallel", "arbitrary")),
        name="ret",
    )(u32, u32, u16, u32, rope_tab, pow_tab, dmask, sq_tab, r0)


def _ret_tables(t_len, pos0, tb):
    half = DK_C // 2
    inv_freq = np.exp(-math.log(ROPE_BASE) * np.arange(half) / half)
    ang = (pos0 + np.arange(t_len))[:, None] * inv_freq[None, :]
    cos, sin = np.cos(ang), np.sin(ang)
    zero = np.zeros_like(sin)
    per_head = lambda a, b_: np.tile(np.concatenate([a, b_], axis=1), (1, H_C))
    rope_tab = np.stack([per_head(cos, cos), per_head(-sin, zero), per_head(zero, sin)])
    lg = np.log1p(-np.exp2(-5.0 - np.arange(H_C)))
    lg_lane = np.repeat(lg, DK_C)
    n = np.arange(tb, dtype=np.float64)
    pow_tab = np.stack([np.exp(lg_lane[None, :] * (n + 1.0)[:, None]),
                        np.exp(lg_lane[None, :] * (tb - 1.0 - n)[:, None])])
    diff = n[:, None] - n[None, :]
    causal = diff >= 0
    dmask = np.where(causal, np.exp(lg[:, None, None] * np.where(causal, diff, 0.0)), 0.0)
    head = np.arange(MIX_C) // DK_C
    same = (head[:, None] == head[None, :]).astype(np.float64)
    carry = np.broadcast_to(np.exp(lg_lane * tb)[:, None], (MIX_C, MIX_C))
    sq_tab = np.stack([carry, same, same / DV_C])
    return tuple(jnp.asarray(a, F32) for a in (rope_tab, pow_tab, dmask, sq_tab))


def _split_w_in(w_in):
    a_end = QKV_A + MIX_A + 2 * H_A
    w16 = w_in.astype(BF16)
    return jnp.pad(w16[..., :a_end], ((0, 0), (0, 0), (0, A_COLS - a_end))), w16[..., a_end:]


def _block_diag(r):
    b = r.shape[0]
    eye = jnp.eye(H_C, dtype=r.dtype)
    return jnp.einsum('bhkv,hg->bhkgv', r, eye).reshape(b, H_C * DK_C, H_C * DV_C)


def _diag_blocks(r):
    b = r.shape[0]
    r = r.reshape(b, H_C, DK_C, H_C, DV_C)
    return jnp.stack([r[:, h, :, h, :] for h in range(H_C)], axis=1)


def _run(x, pos0, caches, weights):
    (norm_ffn1, wg1, wu1, wd1, norm_mix, w_in, conv_w, a_log, dt_bias, gdn_norm, rel_bias, w_out,
     norm_ffn2, wg2, wu2, wd2, norm_final) = weights
    b, t, _ = x.shape
    n = b * t
    tb = min(MIX_TB, t)
    assert t % tb == 0 and tb % CHUNK == 0, (t, tb)
    rope_tab, pow_tab, dmask, sq_tab = _ret_tables(t, pos0, tb)
    gf = norm_final.reshape(1, D_MODEL)
    norm_ffn1, norm_mix, norm_ffn2 = (g.reshape(DEPTH, 1, D_MODEL) for g in (norm_ffn1, norm_mix, norm_ffn2))
    xf = x.reshape(n, D_MODEL)
    new = []
    for l in range(DEPTH):
        if caches is None:
            s_gdn0 = jnp.zeros((b, H_A, DK_A, DV_A), F32)
            cb0 = jnp.zeros((b, SUBLANES, QKV_A), F32)
            r0 = jnp.zeros((b, MIX_C, MIX_C), F32)
            ck = cv = jnp.zeros((b, BAND, MIX_B), F32)
            n_invalid = BAND
        else:
            state_gdn, state_conv, cache_k, cache_v, state_ret = caches
            s_gdn0 = state_gdn[l]
            cb0 = jnp.pad(state_conv[l], ((0, 0), (SUBLANES - (CONV_W - 1), 0), (0, 0)))
            r0 = _block_diag(state_ret[l])
            ck = cache_k[l].reshape(b, BAND, MIX_B)
            cv = cache_v[l].reshape(b, BAND, MIX_B)
            n_invalid = 0
        xf = _ffn(xf, norm_ffn1, wg1, wu1, wd1, l)
        qkv, zba, bc16, bc32, kv_new, conv_tail = _inproj(xf.reshape(b, t, D_MODEL), norm_mix, w_in,
                                                          conv_w[l], cb0, l)
        decay_lanes = lambda v: jnp.pad(v.astype(F32), (H_A, LANES - 2 * H_A))
        prm = jnp.concatenate([decay_lanes(-jnp.exp(a_log[l].astype(F32)))[None],
                               decay_lanes(dt_bias[l])[None], gdn_norm[l].astype(F32)[None],
                               jnp.zeros((SUBLANES - 3, LANES), F32)], axis=0)
        oa, s_gdn = _gdn(qkv, zba, prm, s_gdn0, tb)
        ob = _band(bc16, _band_bias_table(rel_bias[l], tb), ck, cv, tb, n_invalid)
        oc, r_new = _ret(bc32, bc16, rope_tab, pow_tab, dmask, sq_tab, r0, tb)
        xf = _mix_ffn(xf, oa.reshape(n, MIX_A), ob.reshape(n, MIX_B), oc.reshape(n, MIX_C), w_out,
                      norm_ffn2, wg2, wu2, wd2, gf, l, l == DEPTH - 1)
        conv_new = conv_tail[:, SUBLANES - (CONV_W - 1):, :]
        kb_new = kv_new[:, :, :MIX_B].reshape(b, -1, H_B, DH_B)
        vb_new = kv_new[:, :, MIX_B:].reshape(b, -1, H_B, DH_B)
        new.append((s_gdn, conv_new, kb_new, vb_new, _diag_blocks(r_new)))
    g_, c_, k_, v_, r_ = zip(*new)
    return (xf.reshape(b, t, D_MODEL), jnp.stack(g_), jnp.stack(c_), jnp.stack(k_), jnp.stack(v_),
            jnp.stack(r_))


def kernel(x_prompt, x_sample, state_gdn, state_conv, cache_band_k, cache_band_v, state_ret, norm_ffn1, w_ffn1_gate, w_ffn1_up, w_ffn1_down, norm_mix, w_in, conv_w, a_log, dt_bias, gdn_norm, rel_bias, w_out, norm_ffn2, w_ffn2_gate, w_ffn2_up, w_ffn2_down, norm_final):
    past_len = 1024
    weights = (norm_ffn1, w_ffn1_gate.astype(BF16), w_ffn1_up.astype(BF16), w_ffn1_down.astype(BF16),
               norm_mix, _split_w_in(w_in), conv_w, a_log, dt_bias, gdn_norm, rel_bias,
               w_out.astype(BF16), norm_ffn2, w_ffn2_gate.astype(BF16), w_ffn2_up.astype(BF16),
               w_ffn2_down.astype(BF16), norm_final)
    p = _run(x_prompt, 0, None, weights)
    s = _run(x_sample, past_len, (state_gdn, state_conv, cache_band_k, cache_band_v, state_ret), weights)
    return (p[0], s[0]) + p[1:] + s[1:]
```

```python
import functools
import math

import jax
import jax.numpy as jnp
import numpy as np
from jax import lax
from jax.experimental import pallas as pl
from jax.experimental.pallas import tpu as pltpu

F32 = jnp.float32
BF16 = jnp.bfloat16

D_MODEL = 1024
DEPTH = 2
PAST_LEN = 1024
CHUNK = 64
H_A, DK_A, DV_A, CONV_W = 4, 128, 128, 4
H_B, DH_B, BAND_CHUNKS, REL_CLIP = 4, 64, 8, 128
H_C, DK_C, DV_C = 4, 64, 64
ROPE_BASE = 10000.0
D_FF = 2816
EPS = 1e-6
NEG_INF = -1e30
BAND = BAND_CHUNKS * CHUNK
QKV_A = 3 * H_A * DK_A
MIX_A = H_A * DV_A
MIX_B = H_B * DH_B
MIX_C = H_C * DV_C
LANES = 128
SUBLANES = 8
A_COLS = QKV_A + MIX_A + LANES
VMEM_LIMIT = 56 * 1024 * 1024

FFN_TM = 1024
FFN_SUB = 512
FFN_TF = 256
PROJ_TM = 512
PROJ_SUB = 256
MIX_TB = 256
GDN_PROBLEMS = 64
MIX_ROWS = 1024
BAND_GROUP_ROWS = 256


def _params(sem):
    return pltpu.CompilerParams(dimension_semantics=sem, vmem_limit_bytes=VMEM_LIMIT)


def _rms(x, g):
    return (x * lax.rsqrt(jnp.mean(x * x, axis=-1, keepdims=True) + EPS)) * g


def _silu(x):
    return x * jax.nn.sigmoid(x)


def _dot(a, b):
    return jnp.dot(a.astype(BF16), b.astype(BF16), preferred_element_type=F32)


def _dot_nt(a, b):
    return lax.dot_general(a.astype(BF16), b.astype(BF16), (((1,), (1,)), ((), ())),
                           preferred_element_type=F32)


def _dot_tn(a, b):
    return lax.dot_general(a.astype(BF16), b.astype(BF16), (((0,), (0,)), ((), ())),
                           preferred_element_type=F32)


def _dot_exact_rhs(a, b):
    hi = a.astype(BF16)
    r1 = a - hi.astype(F32)
    mid = r1.astype(BF16)
    lo = (r1 - mid.astype(F32)).astype(BF16)
    bb = b.astype(BF16)
    out = jnp.dot(hi, bb, preferred_element_type=F32)
    out = out + jnp.dot(mid, bb, preferred_element_type=F32)
    return out + jnp.dot(lo, bb, preferred_element_type=F32)


def _half_swiglu(x, g, wg_ref, wu_ref, wd_ref):
    h = _rms(x, g).astype(BF16)
    acc = None
    for s in range(D_FF // FFN_TF):
        cols = slice(s * FFN_TF, (s + 1) * FFN_TF)
        gate = jnp.dot(h, wg_ref[:, cols], preferred_element_type=F32)
        up = jnp.dot(h, wu_ref[:, cols], preferred_element_type=F32)
        act = (_silu(gate) * up).astype(BF16)
        part = jnp.dot(act, wd_ref[cols, :], preferred_element_type=F32)
        acc = part if acc is None else acc + part
    return 0.5 * acc


def _ffn_kernel(x_ref, g_ref, wg_ref, wu_ref, wd_ref, o_ref, *, sub):
    for r in range(x_ref.shape[0] // sub):
        rows = slice(r * sub, (r + 1) * sub)
        x = x_ref[rows, :]
        o_ref[rows, :] = x + _half_swiglu(x, g_ref[...], wg_ref, wu_ref, wd_ref)


def _mix_ffn_kernel(x_ref, oa_ref, ob_ref, oc_ref, woa_ref, wob_ref, woc_ref, g_ref, wg_ref, wu_ref,
                    wd_ref, gf_ref, o_ref, *, sub, final_norm):
    for r in range(x_ref.shape[0] // sub):
        rows = slice(r * sub, (r + 1) * sub)
        mix = jnp.dot(oa_ref[rows, :], woa_ref[...], preferred_element_type=F32)
        mix = mix + jnp.dot(ob_ref[rows, :], wob_ref[...], preferred_element_type=F32)
        mix = mix + jnp.dot(oc_ref[rows, :], woc_ref[...], preferred_element_type=F32)
        x2 = x_ref[rows, :] + mix
        y = x2 + _half_swiglu(x2, g_ref[...], wg_ref, wu_ref, wd_ref)
        if final_norm:
            y = _rms(y, gf_ref[...])
        o_ref[rows, :] = y


def _resident(shape, index_map):
    return pl.BlockSpec(shape, index_map, pipeline_mode=pl.Buffered(1))


def _ffn_weight_specs(layer):
    return [
        _resident((None, D_MODEL, D_FF), lambda i: (layer, 0, 0)),
        _resident((None, D_MODEL, D_FF), lambda i: (layer, 0, 0)),
        _resident((None, D_FF, D_MODEL), lambda i: (layer, 0, 0)),
    ]


def _ffn(x, g, wg, wu, wd, layer):
    n = x.shape[0]
    tm = min(FFN_TM, n)
    sub = min(FFN_SUB, tm)
    assert n % tm == 0 and tm % sub == 0, (n, tm)
    tok = pl.BlockSpec((tm, D_MODEL), lambda i: (i, 0))
    return pl.pallas_call(
        functools.partial(_ffn_kernel, sub=sub),
        grid=(n // tm,),
        in_specs=[tok, pl.BlockSpec((None, 1, D_MODEL), lambda i: (layer, 0, 0))] + _ffn_weight_specs(layer),
        out_specs=tok,
        out_shape=jax.ShapeDtypeStruct((n, D_MODEL), F32),
        compiler_params=_params(("parallel",)),
        name="ffn",
    )(x, g, wg, wu, wd)


def _mix_ffn(x, oa, ob, oc, w_out, g, wg, wu, wd, g_final, layer, final_norm):
    n = x.shape[0]
    tm = min(FFN_TM, n)
    sub = min(FFN_SUB, tm)
    assert n % tm == 0 and tm % sub == 0, (n, tm)
    tok = pl.BlockSpec((tm, D_MODEL), lambda i: (i, 0))
    in_specs = [
        tok,
        pl.BlockSpec((tm, MIX_A), lambda i: (i, 0)),
        pl.BlockSpec((tm, MIX_B), lambda i: (i, 0)),
        pl.BlockSpec((tm, MIX_C), lambda i: (i, 0)),
        _resident((None, MIX_A, D_MODEL), lambda i: (layer, 0, 0)),
        _resident((None, MIX_B, D_MODEL), lambda i: (layer, MIX_A // MIX_B, 0)),
        _resident((None, MIX_C, D_MODEL), lambda i: (layer, (MIX_A + MIX_B) // MIX_C, 0)),
        pl.BlockSpec((None, 1, D_MODEL), lambda i: (layer, 0, 0)),
    ] + _ffn_weight_specs(layer) + [pl.BlockSpec((1, D_MODEL), lambda i: (0, 0))]
    return pl.pallas_call(
        functools.partial(_mix_ffn_kernel, sub=sub, final_norm=final_norm),
        grid=(n // tm,),
        in_specs=in_specs,
        out_specs=tok,
        out_shape=jax.ShapeDtypeStruct((n, D_MODEL), F32),
        compiler_params=_params(("parallel",)),
        name="mix_ffn",
    )(x, oa, ob, oc, w_out, w_out, w_out, g, wg, wu, wd, g_final)


def _inproj_kernel(x_ref, g_ref, wa_ref, wbc_ref, convw_ref, cb0_ref,
                   qkv_ref, zba_ref, bc16_ref, bc32_ref, kvnew_ref, tail_ref, xbuf, *, nseq, tl, sub):
    j = pl.program_id(1)
    nslab = QKV_A // LANES

    @pl.when(j == 0)
    def _():
        for si in range(nseq):
            for sl in range(nslab):
                xbuf[si * nslab + sl, 0:SUBLANES, :] = cb0_ref[si, :, sl * LANES:(sl + 1) * LANES]

    if sub <= tl:
        pieces = [[(si, r0, sub)] for si in range(nseq) for r0 in range(0, tl, sub)]
    else:
        per = sub // tl
        pieces = [[(si, 0, tl) for si in range(g0, g0 + per)] for g0 in range(0, nseq, per)]
    for piece in pieces:
        xs = [x_ref[si, r0:r0 + ln, :] for si, r0, ln in piece]
        h = _rms(xs[0] if len(xs) == 1 else jnp.concatenate(xs, axis=0), g_ref[...]).astype(BF16)
        u = jnp.dot(h, wa_ref[...], preferred_element_type=F32)
        u_bc = jnp.dot(h, wbc_ref[...], preferred_element_type=F32)
        qc0, vc0, gc0 = 3 * MIX_B, 3 * MIX_B + 2 * MIX_C, 3 * MIX_B + 3 * MIX_C
        lo = 0
        for si, r0, ln in piece:
            rows, src = slice(r0, r0 + ln), slice(lo, lo + ln)
            zba_ref[si, rows, :] = u[src, QKV_A:]
            bc16_ref[si, rows, 0:qc0] = u_bc[src, 0:qc0].astype(BF16)
            bc16_ref[si, rows, qc0:] = u_bc[src, vc0:gc0].astype(BF16)
            bc32_ref[si, rows, 0:2 * MIX_C] = u_bc[src, qc0:vc0]
            bc32_ref[si, rows, 2 * MIX_C:] = u_bc[src, gc0:]
            kvnew_ref[si, rows, :] = u_bc[src, MIX_B:qc0]
            for sl in range(nslab):
                xbuf[si * nslab + sl, SUBLANES + r0:SUBLANES + r0 + ln, :] = u[lo:lo + ln, sl * LANES:(sl + 1) * LANES]
            for c0 in range(r0, r0 + ln, CHUNK):
                w0 = c0 + SUBLANES - (CONV_W - 1)
                for sl in range(nslab):
                    cols = slice(sl * LANES, (sl + 1) * LANES)
                    idx = si * nslab + sl
                    y = xbuf[idx, w0:w0 + CHUNK, :] * convw_ref[0:1, cols]
                    for tap in range(1, CONV_W):
                        y = y + xbuf[idx, w0 + tap:w0 + tap + CHUNK, :] * convw_ref[tap:tap + 1, cols]
                    y = _silu(y)
                    if sl < 2 * H_A:
                        y = y * lax.rsqrt(jnp.sum(y * y, axis=-1, keepdims=True) + EPS)
                        if sl < H_A:
                            y = y * DK_A ** -0.5
                    qkv_ref[si, c0:c0 + CHUNK, cols] = y.astype(BF16)
            lo += ln
    for idx in range(nseq * nslab):
        si, sl = divmod(idx, nslab)
        tail_ref[si, :, sl * LANES:(sl + 1) * LANES] = xbuf[idx, tl:tl + SUBLANES, :]
        xbuf[idx, 0:SUBLANES, :] = xbuf[idx, tl:tl + SUBLANES, :]


def _inproj(x, g, w_parts, conv_w, cb0, layer):
    b, t, _ = x.shape
    tl = min(PROJ_TM, t)
    nseq = max(1, min(b, PROJ_TM // tl))
    sub = min(PROJ_SUB, nseq * tl)
    assert b % nseq == 0 and t % tl == 0 and (tl % sub == 0 or sub % tl == 0) and tl % CHUNK == 0, (b, t)
    assert tl == min(BAND, t), (tl, t)
    seq_blk = lambda width: pl.BlockSpec((nseq, tl, width), lambda i, j: (i, j, 0))
    seq_shape = lambda width, dtype: jax.ShapeDtypeStruct((b, t, width), dtype)
    return pl.pallas_call(
        functools.partial(_inproj_kernel, nseq=nseq, tl=tl, sub=sub),
        grid=(b // nseq, t // tl),
        in_specs=[pl.BlockSpec((nseq, tl, D_MODEL), lambda i, j: (i, j, 0)),
                  pl.BlockSpec((None, 1, D_MODEL), lambda i, j: (layer, 0, 0)),
                  ] + [_resident((None,) + w.shape[1:], lambda i, j: (layer, 0, 0)) for w in w_parts] + [
                  pl.BlockSpec((CONV_W, QKV_A), lambda i, j: (0, 0)),
                  pl.BlockSpec((nseq, SUBLANES, QKV_A), lambda i, j: (i, 0, 0))],
        out_specs=[seq_blk(QKV_A), seq_blk(A_COLS - QKV_A), seq_blk(3 * MIX_B + MIX_C), seq_blk(3 * MIX_C),
                   pl.BlockSpec((nseq, tl, 2 * MIX_B), lambda i, j: (i, 0, 0)),
                   pl.BlockSpec((nseq, SUBLANES, QKV_A), lambda i, j: (i, 0, 0))],
        out_shape=[seq_shape(QKV_A, BF16), seq_shape(A_COLS - QKV_A, F32),
                   seq_shape(3 * MIX_B + MIX_C, BF16), seq_shape(3 * MIX_C, F32),
                   jax.ShapeDtypeStruct((b, tl, 2 * MIX_B), F32),
                   jax.ShapeDtypeStruct((b, SUBLANES, QKV_A), F32)],
        scratch_shapes=[pltpu.VMEM((nseq * QKV_A // LANES, tl + SUBLANES, LANES), F32)],
        compiler_params=_params(("parallel", "arbitrary")),
        name="inproj",
    )(x, g, *w_parts, conv_w, cb0)


def _gdn_kernel(u_ref, z_ref, ba_ref, prm_ref, s0_ref, oa_ref, sout_ref,
                dbuf, bbuf, val_scr, qk_scr, kdec_scr, attn_scr, s_scr, *, nb, tb):
    t = pl.program_id(1)
    nchunks = tb // CHUNK

    @pl.when(t == 0)
    def _():
        s_scr[...] = s0_ref[...].reshape(s_scr.shape)

    row = lax.broadcasted_iota(jnp.int32, (tb, tb), 0)
    col = lax.broadcasted_iota(jnp.int32, (tb, tb), 1)
    shift = CHUNK.bit_length() - 1
    same_chunk = jnp.right_shift(row, shift) == jnp.right_shift(col, shift)
    tri_bd = jnp.where(same_chunk & (col <= row), 1.0, 0.0)
    for bi in range(nb):
        ba = ba_ref[bi]
        x = ba + prm_ref[1:2, :]
        softplus = jnp.maximum(x, 0.0) + jnp.log1p(jnp.exp(-jnp.abs(x)))
        g = prm_ref[0:1, :] * softplus
        bbuf[bi] = jax.nn.sigmoid(ba)
        dbuf[bi] = _cumsum_rows(g, tri_bd)

    r_i = lax.broadcasted_iota(jnp.int32, (CHUNK, CHUNK), 0)
    c_i = lax.broadcasted_iota(jnp.int32, (CHUNK, CHUNK), 1)
    tril = c_i <= r_i
    strict = c_i < r_i
    eye = c_i == r_i
    eye_f = jnp.where(eye, 1.0, 0.0)
    gnorm = prm_ref[2:3, :]

    def pidx(bi, c, h):
        return (bi * nchunks + c) * H_A + h

    problems = [(bi, c, h) for bi in range(nb) for c in range(nchunks) for h in range(H_A)]
    lmask, qk, a_low = {}, {}, {}
    for bi in range(nb):
        for c in range(nchunks):
            rows = slice(c * CHUNK, (c + 1) * CHUNK)
            dch = dbuf[bi, rows, :]
            beta = bbuf[bi, rows, :]
            e_d = jnp.exp(dch)
            e_dec = jnp.exp(dch[CHUNK - 1:CHUNK, :] - dch)
            for h in range(H_A):
                p = pidx(bi, c, h)
                hcols = slice(h * DV_A, (h + 1) * DV_A)
                q = u_ref[bi, rows, h * DK_A:(h + 1) * DK_A].astype(F32)
                k = u_ref[bi, rows, MIX_A + h * DK_A:MIX_A + (h + 1) * DK_A].astype(F32)
                v = u_ref[bi, rows, 2 * MIX_A + h * DV_A:2 * MIX_A + (h + 1) * DV_A].astype(F32)
                dlane = slice(H_A + h, H_A + h + 1)
                dcol = dch[:, dlane]
                bcol = beta[:, h:h + 1]
                edcol = e_d[:, dlane]
                drow = jnp.sum(jnp.where(eye, dcol, 0.0), axis=0, keepdims=True)
                diff = dcol - drow
                lmask[bi, c, h] = jnp.where(tril, jnp.exp(jnp.where(tril, diff, 0.0)), 0.0)
                kb = k * bcol
                qk[bi, c, h] = _dot_nt(jnp.concatenate([q, kb], axis=0), k)
                val_scr[bi, rows, hcols] = v * bcol
                qk_scr[p, 0:CHUNK, :] = (q * edcol).astype(BF16)
                qk_scr[p, CHUNK:2 * CHUNK, :] = (kb * edcol).astype(BF16)
                kdec_scr[bi, rows, hcols] = (k * e_dec[:, dlane]).astype(BF16)
    for pr in problems:
        attn_scr[pidx(*pr)] = (qk[pr][0:CHUNK] * lmask[pr]).astype(BF16)
        a_low[pr] = jnp.where(strict, qk[pr][CHUNK:2 * CHUNK] * lmask[pr], 0.0)
    pw = a_low
    inv = {pr: eye_f - a_low[pr] for pr in problems}
    for _ in range(5):
        pw = {pr: _dot(pw[pr], pw[pr]) for pr in problems}
        inv = {pr: inv[pr] + _dot(inv[pr], pw[pr]) for pr in problems}
    for bi, c, h in problems:
        p = pidx(bi, c, h)
        rows = slice(c * CHUNK, (c + 1) * CHUNK)
        hcols = slice(h * DV_A, (h + 1) * DV_A)
        rhs = jnp.concatenate([val_scr[bi, rows, hcols].astype(BF16), qk_scr[p, CHUNK:2 * CHUNK, :]], axis=1)
        sol = _dot(inv[bi, c, h], rhs)
        val_scr[bi, rows, hcols] = sol[:, :DV_A]
        qk_scr[p, CHUNK:2 * CHUNK, :] = sol[:, DV_A:].astype(BF16)

    chains = [(bi, h) for bi in range(nb) for h in range(H_A)]
    state = {ch: s_scr[ch[0] * H_A + ch[1]] for ch in chains}
    for c in range(nchunks):
        rows = slice(c * CHUNK, (c + 1) * CHUNK)
        qs = {(bi, h): _dot(qk_scr[pidx(bi, c, h)], state[bi, h]) for bi, h in chains}
        v_new = {(bi, h): val_scr[bi, rows, h * DV_A:(h + 1) * DV_A] - qs[bi, h][CHUNK:2 * CHUNK]
                 for bi, h in chains}
        o_in = {(bi, h): _dot(attn_scr[pidx(bi, c, h)], v_new[bi, h]) for bi, h in chains}
        ds = {(bi, h): _dot_tn(kdec_scr[bi, rows, h * DK_A:(h + 1) * DK_A], v_new[bi, h]) for bi, h in chains}
        for bi, h in chains:
            hcols = slice(h * DV_A, (h + 1) * DV_A)
            e_last = jnp.exp(dbuf[bi, (c + 1) * CHUNK - 1:(c + 1) * CHUNK, H_A + h:H_A + h + 1])
            o = qs[bi, h][0:CHUNK] + o_in[bi, h]
            state[bi, h] = state[bi, h] * e_last + ds[bi, h]
            oa_ref[bi, rows, hcols] = (_rms(o, gnorm) * _silu(z_ref[bi, rows, hcols])).astype(BF16)
    for bi, h in chains:
        s_scr[bi * H_A + h] = state[bi, h]

    @pl.when(t == pl.num_programs(1) - 1)
    def _():
        sout_ref[...] = s_scr[...].reshape(sout_ref.shape)


def _cumsum_rows(g, tri_bd):
    hi = g.astype(BF16)
    r1 = g - hi.astype(F32)
    mid = r1.astype(BF16)
    lo = (r1 - mid.astype(F32)).astype(BF16)
    tb = tri_bd.astype(BF16)
    out = jnp.dot(tb, hi, preferred_element_type=F32)
    out = out + jnp.dot(tb, mid, preferred_element_type=F32)
    return out + jnp.dot(tb, lo, preferred_element_type=F32)


def _gdn(u, zba, prm, s0, tb):
    b, t, _ = u.shape
    nb = max(1, min(b, GDN_PROBLEMS // (tb // CHUNK * H_A)))
    assert b % nb == 0, (b, nb)
    nprob = nb * (tb // CHUNK) * H_A
    return pl.pallas_call(
        functools.partial(_gdn_kernel, nb=nb, tb=tb),
        grid=(b // nb, t // tb),
        in_specs=[pl.BlockSpec((nb, tb, QKV_A), lambda i, j: (i, j, 0)),
                  pl.BlockSpec((nb, tb, MIX_A), lambda i, j: (i, j, 0)),
                  pl.BlockSpec((nb, tb, LANES), lambda i, j: (i, j, MIX_A // LANES)),
                  pl.BlockSpec((SUBLANES, LANES), lambda i, j: (0, 0)),
                  pl.BlockSpec((nb, H_A, DK_A, DV_A), lambda i, j: (i, 0, 0, 0))],
        out_specs=[pl.BlockSpec((nb, tb, MIX_A), lambda i, j: (i, j, 0)),
                   pl.BlockSpec((nb, H_A, DK_A, DV_A), lambda i, j: (i, 0, 0, 0))],
        out_shape=[jax.ShapeDtypeStruct((b, t, MIX_A), BF16),
                   jax.ShapeDtypeStruct((b, H_A, DK_A, DV_A), F32)],
        scratch_shapes=[pltpu.VMEM((nb, tb, LANES), F32), pltpu.VMEM((nb, tb, LANES), F32),
                        pltpu.VMEM((nb, tb, MIX_A), F32),
                        pltpu.VMEM((nprob, 2 * CHUNK, DK_A), BF16),
                        pltpu.VMEM((nb, tb, H_A * DK_A), BF16),
                        pltpu.VMEM((nprob, CHUNK, CHUNK), BF16),
                        pltpu.VMEM((nb * H_A, DK_A, DV_A), F32)],
        compiler_params=_params(("parallel", "arbitrary")),
        name="gdn",
    )(u, zba, zba, prm, s0)


def _band_kernel(q_ref, k_ref, v_ref, bias_ref, ck_ref, cv_ref, o_ref, kbuf, vbuf,
                 *, nb, qb, t_len, n_invalid):
    t = pl.program_id(1)

    @pl.when(t == 0)
    def _():
        for bi in range(nb):
            kbuf[bi, 0:BAND, :] = ck_ref[bi].astype(BF16)
            vbuf[bi, 0:BAND, :] = cv_ref[bi].astype(BF16)
            kbuf[bi, BAND:BAND + t_len, :] = k_ref[bi]
            vbuf[bi, BAND:BAND + t_len, :] = v_ref[bi]

    start = pl.multiple_of(t * qb, qb)
    lane = lax.broadcasted_iota(jnp.int32, (1, MIX_B), 1)
    if n_invalid:
        kpos = lax.broadcasted_iota(jnp.int32, (qb, qb + BAND), 1) + start
        valid = kpos >= n_invalid
    head = [(lane >= h * DH_B) & (lane < (h + 1) * DH_B) for h in range(H_B)]
    per_group = max(1, BAND_GROUP_ROWS // qb)
    head_sets = [range(H_B)] if per_group > 1 else [[h] for h in range(H_B)]
    groups = [(range(g0, min(g0 + per_group, nb)), hs) for g0 in range(0, nb, per_group) for hs in head_sets]
    acc = {bi: jnp.zeros((qb, MIX_B), F32) for bi in range(nb)}
    for seqs, heads in groups:
        chains = [(bi, h) for bi in seqs for h in heads]
        q = {bi: q_ref[bi] * DH_B ** -0.5 for bi in seqs}
        s = {(bi, h): _dot_nt(jnp.where(head[h], q[bi], 0.0), kbuf[bi, pl.ds(start, qb + BAND), :])
             + bias_ref[h] for bi, h in chains}
        p, inv_l = {}, {}
        for ch in chains:
            sc = jnp.where(valid, s[ch], NEG_INF) if n_invalid else s[ch]
            p[ch] = jnp.exp(sc - jnp.max(sc, axis=-1, keepdims=True))
            inv_l[ch] = 1.0 / jnp.sum(p[ch], axis=-1, keepdims=True)
        o = {(bi, h): _dot(p[bi, h], vbuf[bi, pl.ds(start, qb + BAND), :]) for bi, h in chains}
        for bi, h in chains:
            acc[bi] = acc[bi] + jnp.where(head[h], o[bi, h] * inv_l[bi, h], 0.0)
    for bi in range(nb):
        o_ref[bi] = acc[bi].astype(BF16)


def _band(u, bias, ck, cv, qb, n_invalid):
    b, t, _ = u.shape
    nb = max(1, min(b, MIX_ROWS // qb))
    assert b % nb == 0 and ck.shape[0] == b and u.dtype == BF16, (b, nb)
    return pl.pallas_call(
        functools.partial(_band_kernel, nb=nb, qb=qb, t_len=t, n_invalid=n_invalid),
        grid=(b // nb, t // qb),
        in_specs=[pl.BlockSpec((nb, qb, MIX_B), lambda i, j: (i, j, 0)),
                  pl.BlockSpec((nb, t, MIX_B), lambda i, j: (i, 0, 1)),
                  pl.BlockSpec((nb, t, MIX_B), lambda i, j: (i, 0, 2)),
                  pl.BlockSpec((H_B, qb, qb + BAND), lambda i, j: (0, 0, 0)),
                  pl.BlockSpec((nb, BAND, MIX_B), lambda i, j: (i, 0, 0)),
                  pl.BlockSpec((nb, BAND, MIX_B), lambda i, j: (i, 0, 0))],
        out_specs=pl.BlockSpec((nb, qb, MIX_B), lambda i, j: (i, j, 0)),
        out_shape=jax.ShapeDtypeStruct((b, t, MIX_B), BF16),
        scratch_shapes=[pltpu.VMEM((nb, BAND + t, MIX_B), BF16), pltpu.VMEM((nb, BAND + t, MIX_B), BF16)],
        compiler_params=_params(("parallel", "arbitrary")),
        name="band",
    )(u, u, u, bias, ck, cv)


def _band_bias_table(rel_bias, qb):
    rb = rel_bias.astype(F32)
    nh = rb.shape[0]
    w = qb + BAND
    p = w + qb - 1
    rel_lo, rel_hi = -(qb - 1) - BAND, qb - 1
    mid_hi = min(rel_hi, REL_CLIP)
    f = jnp.concatenate([jnp.broadcast_to(rb[:, :1], (nh, -REL_CLIP - rel_lo)),
                         rb[:, :mid_hi + REL_CLIP + 1],
                         jnp.broadcast_to(rb[:, -1:], (nh, rel_hi - mid_hi))], axis=1)
    rolled = jnp.tile(jnp.pad(f, ((0, 0), (0, 1))), (1, qb))[:, :qb * p].reshape(nh, qb, p)
    tab = rolled[:, :, qb - 1:qb - 1 + w]
    ci = np.arange(qb)[:, None] // CHUNK
    cj = np.arange(w)[None, :] // CHUNK
    return jnp.where((cj >= ci) & (cj <= ci + BAND_CHUNKS), tab, NEG_INF)


def _ret_kernel(q_ref, k_ref, v_ref, g_ref, rope_ref, pow_ref, dmask_ref, sq_ref, r0_ref, o_ref, rout_ref,
                r_scr, *, nb, tb):
    t = pl.program_id(1)

    @pl.when(t == 0)
    def _():
        r_scr[...] = r0_ref[...]

    def rope(x):
        return (x * rope_ref[0] + pltpu.roll(x, MIX_C - DK_C // 2, 1) * rope_ref[1]
                + pltpu.roll(x, DK_C // 2, 1) * rope_ref[2])

    lane = lax.broadcasted_iota(jnp.int32, (1, MIX_C), 1)
    avg = sq_ref[2]
    seqs = range(nb)
    chains = [(bi, h) for bi in seqs for h in range(H_C)]
    head = [(lane >= h * DK_C) & (lane < (h + 1) * DK_C) for h in range(H_C)]
    q = [rope(q_ref[bi]) for bi in seqs]
    k = [rope(k_ref[bi]) * DK_C ** -0.5 for bi in seqs]
    v = [v_ref[bi].astype(BF16) for bi in seqs]
    r = [r_scr[bi] for bi in seqs]
    cross = [_dot(q[bi] * pow_ref[0], r[bi]) for bi in seqs]
    s = {(bi, h): _dot_nt(jnp.where(head[h], q[bi], 0.0), k[bi]) * dmask_ref[h] for bi, h in chains}
    kv = [_dot_tn(k[bi] * pow_ref[1], v[bi]) for bi in seqs]
    inner = {(bi, h): _dot(s[bi, h], v[bi]) for bi, h in chains}
    acc = []
    for bi in seqs:
        a = cross[bi]
        for h in range(H_C):
            a = a + jnp.where(head[h], inner[bi, h], 0.0)
        acc.append(a)
        r_scr[bi] = r[bi] * sq_ref[0] + sq_ref[1] * kv[bi]
    mean = [_dot_exact_rhs(acc[bi], avg) for bi in seqs]
    xc = [acc[bi] - mean[bi] for bi in seqs]
    var = [_dot_exact_rhs(xc[bi] * xc[bi], avg) for bi in seqs]
    for bi in seqs:
        o_ref[bi] = ((xc[bi] * lax.rsqrt(var[bi] + EPS)) * _silu(g_ref[bi])).astype(BF16)

    @pl.when(t == pl.num_programs(1) - 1)
    def _():
        rout_ref[...] = r_scr[...]


def _ret(u32, u16, rope_tab, pow_tab, dmask, sq_tab, r0, tb):
    b, t, _ = u32.shape
    nb = max(1, min(b, MIX_ROWS // tb))
    assert b % nb == 0, (b, nb)
    ublk = lambda c: pl.BlockSpec((nb, tb, MIX_C), lambda i, j: (i, j, c))
    return pl.pallas_call(
        functools.partial(_ret_kernel, nb=nb, tb=tb),
        grid=(b // nb, t // tb),
        in_specs=[ublk(0), ublk(1), ublk(3 * MIX_B // MIX_C), ublk(2),
                  pl.BlockSpec((3, tb, MIX_C), lambda i, j: (0, j, 0)),
                  pl.BlockSpec((2, tb, MIX_C), lambda i, j: (0, 0, 0)),
                  pl.BlockSpec((H_C, tb, tb), lambda i, j: (0, 0, 0)),
                  pl.BlockSpec((3, MIX_C, MIX_C), lambda i, j: (0, 0, 0)),
                  pl.BlockSpec((nb, MIX_C, MIX_C), lambda i, j: (i, 0, 0))],
        out_specs=[pl.BlockSpec((nb, tb, MIX_C), lambda i, j: (i, j, 0)),
                   pl.BlockSpec((nb, MIX_C, MIX_C), lambda i, j: (i, 0, 0))],
        out_shape=[jax.ShapeDtypeStruct((b, t, MIX_C), BF16),
                   jax.ShapeDtypeStruct((b, MIX_C, MIX_C), F32)],
        scratch_shapes=[pltpu.VMEM((nb, MIX_C, MIX_C), F32)],
        compiler_params=_params(("parallel", "arbitrary")),
        name="ret",
    )(u32, u32, u16, u32, rope_tab, pow_tab, dmask, sq_tab, r0)


def _ret_tables(t_len, pos0, tb):
    half = DK_C // 2
    inv_freq = np.exp(-math.log(ROPE_BASE) * np.arange(half) / half)
    ang = (pos0 + np.arange(t_len))[:, None] * inv_freq[None, :]
    cos, sin = np.cos(ang), np.sin(ang)
    zero = np.zeros_like(sin)
    per_head = lambda a, b_: np.tile(np.concatenate([a, b_], axis=1), (1, H_C))
    rope_tab = np.stack([per_head(cos, cos), per_head(-sin, zero), per_head(zero, sin)])
    lg = np.log1p(-np.exp2(-5.0 - np.arange(H_C)))
    lg_lane = np.repeat(lg, DK_C)
    n = np.arange(tb, dtype=np.float64)
    pow_tab = np.stack([np.exp(lg_lane[None, :] * (n + 1.0)[:, None]),
                        np.exp(lg_lane[None, :] * (tb - 1.0 - n)[:, None])])
    diff = n[:, None] - n[None, :]
    causal = diff >= 0
    dmask = np.where(causal, np.exp(lg[:, None, None] * np.where(causal, diff, 0.0)), 0.0)
    head = np.arange(MIX_C) // DK_C
    same = (head[:, None] == head[None, :]).astype(np.float64)
    carry = np.broadcast_to(np.exp(lg_lane * tb)[:, None], (MIX_C, MIX_C))
    sq_tab = np.stack([carry, same, same / DV_C])
    return tuple(jnp.asarray(a, F32) for a in (rope_tab, pow_tab, dmask, sq_tab))


def _split_w_in(w_in):
    a_end = QKV_A + MIX_A + 2 * H_A
    w16 = w_in.astype(BF16)
    return jnp.pad(w16[..., :a_end], ((0, 0), (0, 0), (0, A_COLS - a_end))), w16[..., a_end:]


def _block_diag(r):
    b = r.shape[0]
    eye = jnp.eye(H_C, dtype=r.dtype)
    return jnp.einsum('bhkv,hg->bhkgv', r, eye).reshape(b, H_C * DK_C, H_C * DV_C)


def _diag_blocks(r):
    b = r.shape[0]
    r = r.reshape(b, H_C, DK_C, H_C, DV_C)
    return jnp.stack([r[:, h, :, h, :] for h in range(H_C)], axis=1)


def _run(x, pos0, caches, weights):
    (norm_ffn1, wg1, wu1, wd1, norm_mix, w_in, conv_w, a_log, dt_bias, gdn_norm, rel_bias, w_out,
     norm_ffn2, wg2, wu2, wd2, norm_final) = weights
    b, t, _ = x.shape
    n = b * t
    tb = min(MIX_TB, t)
    assert t % tb == 0 and tb % CHUNK == 0, (t, tb)
    rope_tab, pow_tab, dmask, sq_tab = _ret_tables(t, pos0, tb)
    gf = norm_final.reshape(1, D_MODEL)
    norm_ffn1, norm_mix, norm_ffn2 = (g.reshape(DEPTH, 1, D_MODEL) for g in (norm_ffn1, norm_mix, norm_ffn2))
    xf = x.reshape(n, D_MODEL)
    new = []
    for l in range(DEPTH):
        if caches is None:
            s_gdn0 = jnp.zeros((b, H_A, DK_A, DV_A), F32)
            cb0 = jnp.zeros((b, SUBLANES, QKV_A), F32)
            r0 = jnp.zeros((b, MIX_C, MIX_C), F32)
            ck = cv = jnp.zeros((b, BAND, MIX_B), F32)
            n_invalid = BAND
        else:
            state_gdn, state_conv, cache_k, cache_v, state_ret = caches
            s_gdn0 = state_gdn[l]
            cb0 = jnp.pad(state_conv[l], ((0, 0), (SUBLANES - (CONV_W - 1), 0), (0, 0)))
            r0 = _block_diag(state_ret[l])
            ck = cache_k[l].reshape(b, BAND, MIX_B)
            cv = cache_v[l].reshape(b, BAND, MIX_B)
            n_invalid = 0
        xf = _ffn(xf, norm_ffn1, wg1, wu1, wd1, l)
        qkv, zba, bc16, bc32, kv_new, conv_tail = _inproj(xf.reshape(b, t, D_MODEL), norm_mix, w_in,
                                                          conv_w[l], cb0, l)
        decay_lanes = lambda v: jnp.pad(v.astype(F32), (H_A, LANES - 2 * H_A))
        prm = jnp.concatenate([decay_lanes(-jnp.exp(a_log[l].astype(F32)))[None],
                               decay_lanes(dt_bias[l])[None], gdn_norm[l].astype(F32)[None],
                               jnp.zeros((SUBLANES - 3, LANES), F32)], axis=0)
        oa, s_gdn = _gdn(qkv, zba, prm, s_gdn0, tb)
        ob = _band(bc16, _band_bias_table(rel_bias[l], tb), ck, cv, tb, n_invalid)
        oc, r_new = _ret(bc32, bc16, rope_tab, pow_tab, dmask, sq_tab, r0, tb)
        xf = _mix_ffn(xf, oa.reshape(n, MIX_A), ob.reshape(n, MIX_B), oc.reshape(n, MIX_C), w_out,
                      norm_ffn2, wg2, wu2, wd2, gf, l, l == DEPTH - 1)
        conv_new = conv_tail[:, SUBLANES - (CONV_W - 1):, :]
        kb_new = kv_new[:, :, :MIX_B].reshape(b, -1, H_B, DH_B)
        vb_new = kv_new[:, :, MIX_B:].reshape(b, -1, H_B, DH_B)
        new.append((s_gdn, conv_new, kb_new, vb_new, _diag_blocks(r_new)))
    g_, c_, k_, v_, r_ = zip(*new)
    return (xf.reshape(b, t, D_MODEL), jnp.stack(g_), jnp.stack(c_), jnp.stack(k_), jnp.stack(v_),
            jnp.stack(r_))


def kernel(x_prompt, x_sample, state_gdn, state_conv, cache_band_k, cache_band_v, state_ret, norm_ffn1, w_ffn1_gate, w_ffn1_up, w_ffn1_down, norm_mix, w_in, conv_w, a_log, dt_bias, gdn_norm, rel_bias, w_out, norm_ffn2, w_ffn2_gate, w_ffn2_up, w_ffn2_down, norm_final):
    weights = (norm_ffn1, w_ffn1_gate.astype(BF16), w_ffn1_up.astype(BF16), w_ffn1_down.astype(BF16),
               norm_mix, _split_w_in(w_in), conv_w, a_log, dt_bias, gdn_norm, rel_bias,
               w_out.astype(BF16), norm_ffn2, w_ffn2_gate.astype(BF16), w_ffn2_up.astype(BF16),
               w_ffn2_down.astype(BF16), norm_final)
    p = _run(x_prompt, 0, None, weights)
    s = _run(x_sample, PAST_LEN, (state_gdn, state_conv, cache_band_k, cache_band_v, state_ret), weights)
    return (p[0], s[0]) + p[1:] + s[1:]
```

```python
import functools
import math

import jax
import jax.numpy as jnp
import numpy as np
from jax import lax
from jax.experimental import pallas as pl
from jax.experimental.pallas import tpu as pltpu

F32 = jnp.float32
BF16 = jnp.bfloat16

D_MODEL = 1024
DEPTH = 2
PAST_LEN = 1024
CHUNK = 64
H_A, DK_A, DV_A, CONV_W = 4, 128, 128, 4
H_B, DH_B, BAND_CHUNKS, REL_CLIP = 4, 64, 8, 128
H_C, DK_C, DV_C = 4, 64, 64
ROPE_BASE = 10000.0
D_FF = 2816
EPS = 1e-6
NEG_INF = -1e30
BAND = BAND_CHUNKS * CHUNK
QKV_A = 3 * H_A * DK_A
MIX_A = H_A * DV_A
MIX_B = H_B * DH_B
MIX_C = H_C * DV_C
LANES = 128
SUBLANES = 8
A_COLS = QKV_A + MIX_A + LANES
VMEM_LIMIT = 56 * 1024 * 1024

FFN_TM = 1024
FFN_SUB = 512
FFN_TF = 256
PROJ_TM = 512
PROJ_SUB = 256
MIX_TB = 256
GDN_PROBLEMS = 64
MIX_ROWS = 1024
RET_ROWS = 2048
BAND_GROUP_ROWS = 256


def _params(sem):
    return pltpu.CompilerParams(dimension_semantics=sem, vmem_limit_bytes=VMEM_LIMIT)


def _rms(x, g):
    return (x * lax.rsqrt(jnp.mean(x * x, axis=-1, keepdims=True) + EPS)) * g


def _silu(x):
    return x * jax.nn.sigmoid(x)


def _dot(a, b):
    return jnp.dot(a.astype(BF16), b.astype(BF16), preferred_element_type=F32)


def _dot_nt(a, b):
    return lax.dot_general(a.astype(BF16), b.astype(BF16), (((1,), (1,)), ((), ())),
                           preferred_element_type=F32)


def _dot_tn(a, b):
    return lax.dot_general(a.astype(BF16), b.astype(BF16), (((0,), (0,)), ((), ())),
                           preferred_element_type=F32)


def _dot_exact_rhs(a, b):
    hi = a.astype(BF16)
    r1 = a - hi.astype(F32)
    mid = r1.astype(BF16)
    lo = (r1 - mid.astype(F32)).astype(BF16)
    bb = b.astype(BF16)
    out = jnp.dot(hi, bb, preferred_element_type=F32)
    out = out + jnp.dot(mid, bb, preferred_element_type=F32)
    return out + jnp.dot(lo, bb, preferred_element_type=F32)


def _half_swiglu(x, g, wg_ref, wu_ref, wd_ref):
    h = _rms(x, g).astype(BF16)
    acc = None
    for s in range(D_FF // FFN_TF):
        cols = slice(s * FFN_TF, (s + 1) * FFN_TF)
        gate = jnp.dot(h, wg_ref[:, cols], preferred_element_type=F32)
        up = jnp.dot(h, wu_ref[:, cols], preferred_element_type=F32)
        act = (_silu(gate) * up).astype(BF16)
        part = jnp.dot(act, wd_ref[cols, :], preferred_element_type=F32)
        acc = part if acc is None else acc + part
    return 0.5 * acc


def _ffn_kernel(x_ref, g_ref, wg_ref, wu_ref, wd_ref, o_ref, *, sub):
    for r in range(x_ref.shape[0] // sub):
        rows = slice(r * sub, (r + 1) * sub)
        x = x_ref[rows, :]
        o_ref[rows, :] = x + _half_swiglu(x, g_ref[...], wg_ref, wu_ref, wd_ref)


def _mix_ffn_kernel(x_ref, oa_ref, ob_ref, oc_ref, woa_ref, wob_ref, woc_ref, g_ref, wg_ref, wu_ref,
                    wd_ref, gf_ref, o_ref, *, sub, final_norm):
    for r in range(x_ref.shape[0] // sub):
        rows = slice(r * sub, (r + 1) * sub)
        mix = jnp.dot(oa_ref[rows, :], woa_ref[...], preferred_element_type=F32)
        mix = mix + jnp.dot(ob_ref[rows, :], wob_ref[...], preferred_element_type=F32)
        mix = mix + jnp.dot(oc_ref[rows, :], woc_ref[...], preferred_element_type=F32)
        x2 = x_ref[rows, :] + mix
        y = x2 + _half_swiglu(x2, g_ref[...], wg_ref, wu_ref, wd_ref)
        if final_norm:
            y = _rms(y, gf_ref[...])
        o_ref[rows, :] = y


def _resident(shape, index_map):
    return pl.BlockSpec(shape, index_map, pipeline_mode=pl.Buffered(1))


def _ffn_weight_specs(layer):
    return [
        _resident((None, D_MODEL, D_FF), lambda i: (layer, 0, 0)),
        _resident((None, D_MODEL, D_FF), lambda i: (layer, 0, 0)),
        _resident((None, D_FF, D_MODEL), lambda i: (layer, 0, 0)),
    ]


def _ffn(x, g, wg, wu, wd, layer):
    n = x.shape[0]
    tm = min(FFN_TM, n)
    sub = min(FFN_SUB, tm)
    assert n % tm == 0 and tm % sub == 0, (n, tm)
    tok = pl.BlockSpec((tm, D_MODEL), lambda i: (i, 0))
    return pl.pallas_call(
        functools.partial(_ffn_kernel, sub=sub),
        grid=(n // tm,),
        in_specs=[tok, pl.BlockSpec((None, 1, D_MODEL), lambda i: (layer, 0, 0))] + _ffn_weight_specs(layer),
        out_specs=tok,
        out_shape=jax.ShapeDtypeStruct((n, D_MODEL), F32),
        compiler_params=_params(("parallel",)),
        name="ffn",
    )(x, g, wg, wu, wd)


def _mix_ffn(x, oa, ob, oc, w_out, g, wg, wu, wd, g_final, layer, final_norm):
    n = x.shape[0]
    tm = min(FFN_TM, n)
    sub = min(FFN_SUB, tm)
    assert n % tm == 0 and tm % sub == 0, (n, tm)
    tok = pl.BlockSpec((tm, D_MODEL), lambda i: (i, 0))
    in_specs = [
        tok,
        pl.BlockSpec((tm, MIX_A), lambda i: (i, 0)),
        pl.BlockSpec((tm, MIX_B), lambda i: (i, 0)),
        pl.BlockSpec((tm, MIX_C), lambda i: (i, 0)),
        _resident((None, MIX_A, D_MODEL), lambda i: (layer, 0, 0)),
        _resident((None, MIX_B, D_MODEL), lambda i: (layer, MIX_A // MIX_B, 0)),
        _resident((None, MIX_C, D_MODEL), lambda i: (layer, (MIX_A + MIX_B) // MIX_C, 0)),
        pl.BlockSpec((None, 1, D_MODEL), lambda i: (layer, 0, 0)),
    ] + _ffn_weight_specs(layer) + [pl.BlockSpec((1, D_MODEL), lambda i: (0, 0))]
    return pl.pallas_call(
        functools.partial(_mix_ffn_kernel, sub=sub, final_norm=final_norm),
        grid=(n // tm,),
        in_specs=in_specs,
        out_specs=tok,
        out_shape=jax.ShapeDtypeStruct((n, D_MODEL), F32),
        compiler_params=_params(("parallel",)),
        name="mix_ffn",
    )(x, oa, ob, oc, w_out, w_out, w_out, g, wg, wu, wd, g_final)


def _inproj_kernel(x_ref, g_ref, wa_ref, wbc_ref, convw_ref, cb0_ref,
                   qkv_ref, zba_ref, bc16_ref, bc32_ref, kvnew_ref, tail_ref, xbuf, *, nseq, tl, sub):
    j = pl.program_id(1)
    nslab = QKV_A // LANES

    @pl.when(j == 0)
    def _():
        for si in range(nseq):
            for sl in range(nslab):
                xbuf[si * nslab + sl, 0:SUBLANES, :] = cb0_ref[si, :, sl * LANES:(sl + 1) * LANES]

    if sub <= tl:
        pieces = [[(si, r0, sub)] for si in range(nseq) for r0 in range(0, tl, sub)]
    else:
        per = sub // tl
        pieces = [[(si, 0, tl) for si in range(g0, g0 + per)] for g0 in range(0, nseq, per)]
    for piece in pieces:
        xs = [x_ref[si, r0:r0 + ln, :] for si, r0, ln in piece]
        h = _rms(xs[0] if len(xs) == 1 else jnp.concatenate(xs, axis=0), g_ref[...]).astype(BF16)
        u = jnp.dot(h, wa_ref[...], preferred_element_type=F32)
        u_bc = jnp.dot(h, wbc_ref[...], preferred_element_type=F32)
        qc0, vc0, gc0 = 3 * MIX_B, 3 * MIX_B + 2 * MIX_C, 3 * MIX_B + 3 * MIX_C
        lo = 0
        for si, r0, ln in piece:
            rows, src = slice(r0, r0 + ln), slice(lo, lo + ln)
            zba_ref[si, rows, :] = u[src, QKV_A:]
            bc16_ref[si, rows, 0:qc0] = u_bc[src, 0:qc0].astype(BF16)
            bc16_ref[si, rows, qc0:] = u_bc[src, vc0:gc0].astype(BF16)
            bc32_ref[si, rows, 0:2 * MIX_C] = u_bc[src, qc0:vc0]
            bc32_ref[si, rows, 2 * MIX_C:] = u_bc[src, gc0:]
            kvnew_ref[si, rows, :] = u_bc[src, MIX_B:qc0]
            for sl in range(nslab):
                xbuf[si * nslab + sl, SUBLANES + r0:SUBLANES + r0 + ln, :] = u[lo:lo + ln, sl * LANES:(sl + 1) * LANES]
            for c0 in range(r0, r0 + ln, CHUNK):
                w0 = c0 + SUBLANES - (CONV_W - 1)
                for sl in range(nslab):
                    cols = slice(sl * LANES, (sl + 1) * LANES)
                    idx = si * nslab + sl
                    y = xbuf[idx, w0:w0 + CHUNK, :] * convw_ref[0:1, cols]
                    for tap in range(1, CONV_W):
                        y = y + xbuf[idx, w0 + tap:w0 + tap + CHUNK, :] * convw_ref[tap:tap + 1, cols]
                    y = _silu(y)
                    if sl < 2 * H_A:
                        y = y * lax.rsqrt(jnp.sum(y * y, axis=-1, keepdims=True) + EPS)
                        if sl < H_A:
                            y = y * DK_A ** -0.5
                    qkv_ref[si, c0:c0 + CHUNK, cols] = y.astype(BF16)
            lo += ln
    for idx in range(nseq * nslab):
        si, sl = divmod(idx, nslab)
        tail_ref[si, :, sl * LANES:(sl + 1) * LANES] = xbuf[idx, tl:tl + SUBLANES, :]
        xbuf[idx, 0:SUBLANES, :] = xbuf[idx, tl:tl + SUBLANES, :]


def _inproj(x, g, w_parts, conv_w, cb0, layer):
    b, t, _ = x.shape
    tl = min(PROJ_TM, t)
    nseq = max(1, min(b, PROJ_TM // tl))
    sub = min(PROJ_SUB, nseq * tl)
    assert b % nseq == 0 and t % tl == 0 and (tl % sub == 0 or sub % tl == 0) and tl % CHUNK == 0, (b, t)
    assert tl == min(BAND, t), (tl, t)
    seq_blk = lambda width: pl.BlockSpec((nseq, tl, width), lambda i, j: (i, j, 0))
    seq_shape = lambda width, dtype: jax.ShapeDtypeStruct((b, t, width), dtype)
    return pl.pallas_call(
        functools.partial(_inproj_kernel, nseq=nseq, tl=tl, sub=sub),
        grid=(b // nseq, t // tl),
        in_specs=[pl.BlockSpec((nseq, tl, D_MODEL), lambda i, j: (i, j, 0)),
                  pl.BlockSpec((None, 1, D_MODEL), lambda i, j: (layer, 0, 0)),
                  ] + [_resident((None,) + w.shape[1:], lambda i, j: (layer, 0, 0)) for w in w_parts] + [
                  pl.BlockSpec((CONV_W, QKV_A), lambda i, j: (0, 0)),
                  pl.BlockSpec((nseq, SUBLANES, QKV_A), lambda i, j: (i, 0, 0))],
        out_specs=[seq_blk(QKV_A), seq_blk(A_COLS - QKV_A), seq_blk(3 * MIX_B + MIX_C), seq_blk(3 * MIX_C),
                   pl.BlockSpec((nseq, tl, 2 * MIX_B), lambda i, j: (i, 0, 0)),
                   pl.BlockSpec((nseq, SUBLANES, QKV_A), lambda i, j: (i, 0, 0))],
        out_shape=[seq_shape(QKV_A, BF16), seq_shape(A_COLS - QKV_A, F32),
                   seq_shape(3 * MIX_B + MIX_C, BF16), seq_shape(3 * MIX_C, F32),
                   jax.ShapeDtypeStruct((b, tl, 2 * MIX_B), F32),
                   jax.ShapeDtypeStruct((b, SUBLANES, QKV_A), F32)],
        scratch_shapes=[pltpu.VMEM((nseq * QKV_A // LANES, tl + SUBLANES, LANES), F32)],
        compiler_params=_params(("parallel", "arbitrary")),
        name="inproj",
    )(x, g, *w_parts, conv_w, cb0)


def _gdn_kernel(u_ref, z_ref, ba_ref, prm_ref, s0_ref, oa_ref, sout_ref,
                dbuf, bbuf, val_scr, qk_scr, kdec_scr, attn_scr, s_scr, *, nb, tb):
    t = pl.program_id(1)
    nchunks = tb // CHUNK

    @pl.when(t == 0)
    def _():
        s_scr[...] = s0_ref[...].reshape(s_scr.shape)

    row = lax.broadcasted_iota(jnp.int32, (tb, tb), 0)
    col = lax.broadcasted_iota(jnp.int32, (tb, tb), 1)
    shift = CHUNK.bit_length() - 1
    same_chunk = jnp.right_shift(row, shift) == jnp.right_shift(col, shift)
    tri_bd = jnp.where(same_chunk & (col <= row), 1.0, 0.0)
    for bi in range(nb):
        ba = ba_ref[bi]
        x = ba + prm_ref[1:2, :]
        softplus = jnp.maximum(x, 0.0) + jnp.log1p(jnp.exp(-jnp.abs(x)))
        g = prm_ref[0:1, :] * softplus
        bbuf[bi] = jax.nn.sigmoid(ba)
        dbuf[bi] = _cumsum_rows(g, tri_bd)

    r_i = lax.broadcasted_iota(jnp.int32, (CHUNK, CHUNK), 0)
    c_i = lax.broadcasted_iota(jnp.int32, (CHUNK, CHUNK), 1)
    tril = c_i <= r_i
    strict = c_i < r_i
    eye = c_i == r_i
    eye_f = jnp.where(eye, 1.0, 0.0)
    gnorm = prm_ref[2:3, :]

    def pidx(bi, c, h):
        return (bi * nchunks + c) * H_A + h

    problems = [(bi, c, h) for bi in range(nb) for c in range(nchunks) for h in range(H_A)]
    lmask, qk, a_low = {}, {}, {}
    for bi in range(nb):
        for c in range(nchunks):
            rows = slice(c * CHUNK, (c + 1) * CHUNK)
            dch = dbuf[bi, rows, :]
            beta = bbuf[bi, rows, :]
            e_d = jnp.exp(dch)
            e_dec = jnp.exp(dch[CHUNK - 1:CHUNK, :] - dch)
            for h in range(H_A):
                p = pidx(bi, c, h)
                hcols = slice(h * DV_A, (h + 1) * DV_A)
                q = u_ref[bi, rows, h * DK_A:(h + 1) * DK_A].astype(F32)
                k = u_ref[bi, rows, MIX_A + h * DK_A:MIX_A + (h + 1) * DK_A].astype(F32)
                v = u_ref[bi, rows, 2 * MIX_A + h * DV_A:2 * MIX_A + (h + 1) * DV_A].astype(F32)
                dlane = slice(H_A + h, H_A + h + 1)
                dcol = dch[:, dlane]
                bcol = beta[:, h:h + 1]
                edcol = e_d[:, dlane]
                drow = jnp.sum(jnp.where(eye, dcol, 0.0), axis=0, keepdims=True)
                diff = dcol - drow
                lmask[bi, c, h] = jnp.where(tril, jnp.exp(jnp.where(tril, diff, 0.0)), 0.0)
                kb = k * bcol
                qk[bi, c, h] = _dot_nt(jnp.concatenate([q, kb], axis=0), k)
                val_scr[bi, rows, hcols] = v * bcol
                qk_scr[p, 0:CHUNK, :] = (q * edcol).astype(BF16)
                qk_scr[p, CHUNK:2 * CHUNK, :] = (kb * edcol).astype(BF16)
                kdec_scr[bi, rows, hcols] = (k * e_dec[:, dlane]).astype(BF16)
    for pr in problems:
        attn_scr[pidx(*pr)] = (qk[pr][0:CHUNK] * lmask[pr]).astype(BF16)
        a_low[pr] = jnp.where(strict, qk[pr][CHUNK:2 * CHUNK] * lmask[pr], 0.0)
    pw = a_low
    inv = {pr: eye_f - a_low[pr] for pr in problems}
    for _ in range(5):
        pw = {pr: _dot(pw[pr], pw[pr]) for pr in problems}
        inv = {pr: inv[pr] + _dot(inv[pr], pw[pr]) for pr in problems}
    for bi, c, h in problems:
        p = pidx(bi, c, h)
        rows = slice(c * CHUNK, (c + 1) * CHUNK)
        hcols = slice(h * DV_A, (h + 1) * DV_A)
        rhs = jnp.concatenate([val_scr[bi, rows, hcols].astype(BF16), qk_scr[p, CHUNK:2 * CHUNK, :]], axis=1)
        sol = _dot(inv[bi, c, h], rhs)
        val_scr[bi, rows, hcols] = sol[:, :DV_A]
        qk_scr[p, CHUNK:2 * CHUNK, :] = sol[:, DV_A:].astype(BF16)

    chains = [(bi, h) for bi in range(nb) for h in range(H_A)]
    state = {ch: s_scr[ch[0] * H_A + ch[1]] for ch in chains}
    for c in range(nchunks):
        rows = slice(c * CHUNK, (c + 1) * CHUNK)
        qs = {(bi, h): _dot(qk_scr[pidx(bi, c, h)], state[bi, h]) for bi, h in chains}
        v_new = {(bi, h): val_scr[bi, rows, h * DV_A:(h + 1) * DV_A] - qs[bi, h][CHUNK:2 * CHUNK]
                 for bi, h in chains}
        o_in = {(bi, h): _dot(attn_scr[pidx(bi, c, h)], v_new[bi, h]) for bi, h in chains}
        ds = {(bi, h): _dot_tn(kdec_scr[bi, rows, h * DK_A:(h + 1) * DK_A], v_new[bi, h]) for bi, h in chains}
        for bi, h in chains:
            hcols = slice(h * DV_A, (h + 1) * DV_A)
            e_last = jnp.exp(dbuf[bi, (c + 1) * CHUNK - 1:(c + 1) * CHUNK, H_A + h:H_A + h + 1])
            o = qs[bi, h][0:CHUNK] + o_in[bi, h]
            state[bi, h] = state[bi, h] * e_last + ds[bi, h]
            oa_ref[bi, rows, hcols] = (_rms(o, gnorm) * _silu(z_ref[bi, rows, hcols])).astype(BF16)
    for bi, h in chains:
        s_scr[bi * H_A + h] = state[bi, h]

    @pl.when(t == pl.num_programs(1) - 1)
    def _():
        sout_ref[...] = s_scr[...].reshape(sout_ref.shape)


def _cumsum_rows(g, tri_bd):
    hi = g.astype(BF16)
    r1 = g - hi.astype(F32)
    mid = r1.astype(BF16)
    lo = (r1 - mid.astype(F32)).astype(BF16)
    tb = tri_bd.astype(BF16)
    out = jnp.dot(tb, hi, preferred_element_type=F32)
    out = out + jnp.dot(tb, mid, preferred_element_type=F32)
    return out + jnp.dot(tb, lo, preferred_element_type=F32)


def _gdn(u, zba, prm, s0, tb):
    b, t, _ = u.shape
    nb = max(1, min(b, GDN_PROBLEMS // (tb // CHUNK * H_A)))
    assert b % nb == 0, (b, nb)
    nprob = nb * (tb // CHUNK) * H_A
    return pl.pallas_call(
        functools.partial(_gdn_kernel, nb=nb, tb=tb),
        grid=(b // nb, t // tb),
        in_specs=[pl.BlockSpec((nb, tb, QKV_A), lambda i, j: (i, j, 0)),
                  pl.BlockSpec((nb, tb, MIX_A), lambda i, j: (i, j, 0)),
                  pl.BlockSpec((nb, tb, LANES), lambda i, j: (i, j, MIX_A // LANES)),
                  pl.BlockSpec((SUBLANES, LANES), lambda i, j: (0, 0)),
                  pl.BlockSpec((nb, H_A, DK_A, DV_A), lambda i, j: (i, 0, 0, 0))],
        out_specs=[pl.BlockSpec((nb, tb, MIX_A), lambda i, j: (i, j, 0)),
                   pl.BlockSpec((nb, H_A, DK_A, DV_A), lambda i, j: (i, 0, 0, 0))],
        out_shape=[jax.ShapeDtypeStruct((b, t, MIX_A), BF16),
                   jax.ShapeDtypeStruct((b, H_A, DK_A, DV_A), F32)],
        scratch_shapes=[pltpu.VMEM((nb, tb, LANES), F32), pltpu.VMEM((nb, tb, LANES), F32),
                        pltpu.VMEM((nb, tb, MIX_A), F32),
                        pltpu.VMEM((nprob, 2 * CHUNK, DK_A), BF16),
                        pltpu.VMEM((nb, tb, H_A * DK_A), BF16),
                        pltpu.VMEM((nprob, CHUNK, CHUNK), BF16),
                        pltpu.VMEM((nb * H_A, DK_A, DV_A), F32)],
        compiler_params=_params(("parallel", "arbitrary")),
        name="gdn",
    )(u, zba, zba, prm, s0)


def _band_kernel(q_ref, k_ref, v_ref, bias_ref, ck_ref, cv_ref, o_ref, kbuf, vbuf,
                 *, nb, qb, t_len, n_invalid):
    t = pl.program_id(1)

    @pl.when(t == 0)
    def _():
        for bi in range(nb):
            kbuf[bi, 0:BAND, :] = ck_ref[bi].astype(BF16)
            vbuf[bi, 0:BAND, :] = cv_ref[bi].astype(BF16)
            kbuf[bi, BAND:BAND + t_len, :] = k_ref[bi]
            vbuf[bi, BAND:BAND + t_len, :] = v_ref[bi]

    start = pl.multiple_of(t * qb, qb)
    lane = lax.broadcasted_iota(jnp.int32, (1, MIX_B), 1)
    if n_invalid:
        kpos = lax.broadcasted_iota(jnp.int32, (qb, qb + BAND), 1) + start
        valid = kpos >= n_invalid
    head = [(lane >= h * DH_B) & (lane < (h + 1) * DH_B) for h in range(H_B)]
    per_group = max(1, BAND_GROUP_ROWS // qb)
    head_sets = [range(H_B)] if per_group > 1 else [[h] for h in range(H_B)]
    groups = [(range(g0, min(g0 + per_group, nb)), hs) for g0 in range(0, nb, per_group) for hs in head_sets]
    acc = {bi: jnp.zeros((qb, MIX_B), F32) for bi in range(nb)}
    for seqs, heads in groups:
        chains = [(bi, h) for bi in seqs for h in heads]
        q = {bi: q_ref[bi] * DH_B ** -0.5 for bi in seqs}
        s = {(bi, h): _dot_nt(jnp.where(head[h], q[bi], 0.0), kbuf[bi, pl.ds(start, qb + BAND), :])
             + bias_ref[h] for bi, h in chains}
        p, inv_l = {}, {}
        for ch in chains:
            sc = jnp.where(valid, s[ch], NEG_INF) if n_invalid else s[ch]
            p[ch] = jnp.exp(sc - jnp.max(sc, axis=-1, keepdims=True))
            inv_l[ch] = 1.0 / jnp.sum(p[ch], axis=-1, keepdims=True)
        o = {(bi, h): _dot(p[bi, h], vbuf[bi, pl.ds(start, qb + BAND), :]) for bi, h in chains}
        for bi, h in chains:
            acc[bi] = acc[bi] + jnp.where(head[h], o[bi, h] * inv_l[bi, h], 0.0)
    for bi in range(nb):
        o_ref[bi] = acc[bi].astype(BF16)


def _band(u, bias, ck, cv, qb, n_invalid):
    b, t, _ = u.shape
    nb = max(1, min(b, MIX_ROWS // qb))
    assert b % nb == 0 and ck.shape[0] == b and u.dtype == BF16, (b, nb)
    return pl.pallas_call(
        functools.partial(_band_kernel, nb=nb, qb=qb, t_len=t, n_invalid=n_invalid),
        grid=(b // nb, t // qb),
        in_specs=[pl.BlockSpec((nb, qb, MIX_B), lambda i, j: (i, j, 0)),
                  pl.BlockSpec((nb, t, MIX_B), lambda i, j: (i, 0, 1)),
                  pl.BlockSpec((nb, t, MIX_B), lambda i, j: (i, 0, 2)),
                  pl.BlockSpec((H_B, qb, qb + BAND), lambda i, j: (0, 0, 0)),
                  pl.BlockSpec((nb, BAND, MIX_B), lambda i, j: (i, 0, 0)),
                  pl.BlockSpec((nb, BAND, MIX_B), lambda i, j: (i, 0, 0))],
        out_specs=pl.BlockSpec((nb, qb, MIX_B), lambda i, j: (i, j, 0)),
        out_shape=jax.ShapeDtypeStruct((b, t, MIX_B), BF16),
        scratch_shapes=[pltpu.VMEM((nb, BAND + t, MIX_B), BF16), pltpu.VMEM((nb, BAND + t, MIX_B), BF16)],
        compiler_params=_params(("parallel", "arbitrary")),
        name="band",
    )(u, u, u, bias, ck, cv)


def _band_bias_table(rel_bias, qb):
    rb = rel_bias.astype(F32)
    nh = rb.shape[0]
    w = qb + BAND
    p = w + qb - 1
    rel_lo, rel_hi = -(qb - 1) - BAND, qb - 1
    mid_hi = min(rel_hi, REL_CLIP)
    f = jnp.concatenate([jnp.broadcast_to(rb[:, :1], (nh, -REL_CLIP - rel_lo)),
                         rb[:, :mid_hi + REL_CLIP + 1],
                         jnp.broadcast_to(rb[:, -1:], (nh, rel_hi - mid_hi))], axis=1)
    rolled = jnp.tile(jnp.pad(f, ((0, 0), (0, 1))), (1, qb))[:, :qb * p].reshape(nh, qb, p)
    tab = rolled[:, :, qb - 1:qb - 1 + w]
    ci = np.arange(qb)[:, None] // CHUNK
    cj = np.arange(w)[None, :] // CHUNK
    return jnp.where((cj >= ci) & (cj <= ci + BAND_CHUNKS), tab, NEG_INF)


def _ret_kernel(q_ref, k_ref, v_ref, g_ref, rope_ref, pow_ref, dmask_ref, sq_ref, r0_ref, o_ref, rout_ref,
                r_scr, *, nb, tb):
    t = pl.program_id(1)

    @pl.when(t == 0)
    def _():
        r_scr[...] = r0_ref[...]

    def rope(x):
        return (x * rope_ref[0] + pltpu.roll(x, MIX_C - DK_C // 2, 1) * rope_ref[1]
                + pltpu.roll(x, DK_C // 2, 1) * rope_ref[2])

    lane = lax.broadcasted_iota(jnp.int32, (1, MIX_C), 1)
    avg = sq_ref[2]
    seqs = range(nb)
    chains = [(bi, h) for bi in seqs for h in range(H_C)]
    head = [(lane >= h * DK_C) & (lane < (h + 1) * DK_C) for h in range(H_C)]
    q = [rope(q_ref[bi]) for bi in seqs]
    k = [rope(k_ref[bi]) * DK_C ** -0.5 for bi in seqs]
    v = [v_ref[bi].astype(BF16) for bi in seqs]
    r = [r_scr[bi] for bi in seqs]
    cross = [_dot(q[bi] * pow_ref[0], r[bi]) for bi in seqs]
    s = {(bi, h): _dot_nt(jnp.where(head[h], q[bi], 0.0), k[bi]) * dmask_ref[h] for bi, h in chains}
    kv = [_dot_tn(k[bi] * pow_ref[1], v[bi]) for bi in seqs]
    inner = {(bi, h): _dot(s[bi, h], v[bi]) for bi, h in chains}
    acc = []
    for bi in seqs:
        a = cross[bi]
        for h in range(H_C):
            a = a + jnp.where(head[h], inner[bi, h], 0.0)
        acc.append(a)
        r_scr[bi] = r[bi] * sq_ref[0] + sq_ref[1] * kv[bi]
    mean = [_dot_exact_rhs(acc[bi], avg) for bi in seqs]
    xc = [acc[bi] - mean[bi] for bi in seqs]
    var = [_dot_exact_rhs(xc[bi] * xc[bi], avg) for bi in seqs]
    for bi in seqs:
        o_ref[bi] = ((xc[bi] * lax.rsqrt(var[bi] + EPS)) * _silu(g_ref[bi])).astype(BF16)

    @pl.when(t == pl.num_programs(1) - 1)
    def _():
        rout_ref[...] = r_scr[...]


def _ret(u32, u16, rope_tab, pow_tab, dmask, sq_tab, r0, tb):
    b, t, _ = u32.shape
    nb = max(1, min(b, RET_ROWS // tb))
    assert b % nb == 0, (b, nb)
    ublk = lambda c: pl.BlockSpec((nb, tb, MIX_C), lambda i, j: (i, j, c))
    return pl.pallas_call(
        functools.partial(_ret_kernel, nb=nb, tb=tb),
        grid=(b // nb, t // tb),
        in_specs=[ublk(0), ublk(1), ublk(3 * MIX_B // MIX_C), ublk(2),
                  pl.BlockSpec((3, tb, MIX_C), lambda i, j: (0, j, 0)),
                  pl.BlockSpec((2, tb, MIX_C), lambda i, j: (0, 0, 0)),
                  pl.BlockSpec((H_C, tb, tb), lambda i, j: (0, 0, 0)),
                  pl.BlockSpec((3, MIX_C, MIX_C), lambda i, j: (0, 0, 0)),
                  pl.BlockSpec((nb, MIX_C, MIX_C), lambda i, j: (i, 0, 0))],
        out_specs=[pl.BlockSpec((nb, tb, MIX_C), lambda i, j: (i, j, 0)),
                   pl.BlockSpec((nb, MIX_C, MIX_C), lambda i, j: (i, 0, 0))],
        out_shape=[jax.ShapeDtypeStruct((b, t, MIX_C), BF16),
                   jax.ShapeDtypeStruct((b, MIX_C, MIX_C), F32)],
        scratch_shapes=[pltpu.VMEM((nb, MIX_C, MIX_C), F32)],
        compiler_params=_params(("parallel", "arbitrary")),
        name="ret",
    )(u32, u32, u16, u32, rope_tab, pow_tab, dmask, sq_tab, r0)


def _ret_tables(t_len, pos0, tb):
    half = DK_C // 2
    inv_freq = np.exp(-math.log(ROPE_BASE) * np.arange(half) / half)
    ang = (pos0 + np.arange(t_len))[:, None] * inv_freq[None, :]
    cos, sin = np.cos(ang), np.sin(ang)
    zero = np.zeros_like(sin)
    per_head = lambda a, b_: np.tile(np.concatenate([a, b_], axis=1), (1, H_C))
    rope_tab = np.stack([per_head(cos, cos), per_head(-sin, zero), per_head(zero, sin)])
    lg = np.log1p(-np.exp2(-5.0 - np.arange(H_C)))
    lg_lane = np.repeat(lg, DK_C)
    n = np.arange(tb, dtype=np.float64)
    pow_tab = np.stack([np.exp(lg_lane[None, :] * (n + 1.0)[:, None]),
                        np.exp(lg_lane[None, :] * (tb - 1.0 - n)[:, None])])
    diff = n[:, None] - n[None, :]
    causal = diff >= 0
    dmask = np.where(causal, np.exp(lg[:, None, None] * np.where(causal, diff, 0.0)), 0.0)
    head = np.arange(MIX_C) // DK_C
    same = (head[:, None] == head[None, :]).astype(np.float64)
    carry = np.broadcast_to(np.exp(lg_lane * tb)[:, None], (MIX_C, MIX_C))
    sq_tab = np.stack([carry, same, same / DV_C])
    return tuple(jnp.asarray(a, F32) for a in (rope_tab, pow_tab, dmask, sq_tab))


def _split_w_in(w_in):
    a_end = QKV_A + MIX_A + 2 * H_A
    w16 = w_in.astype(BF16)
    return jnp.pad(w16[..., :a_end], ((0, 0), (0, 0), (0, A_COLS - a_end))), w16[..., a_end:]


def _block_diag(r):
    b = r.shape[0]
    eye = jnp.eye(H_C, dtype=r.dtype)
    return jnp.einsum('bhkv,hg->bhkgv', r, eye).reshape(b, H_C * DK_C, H_C * DV_C)


def _diag_blocks(r):
    b = r.shape[0]
    r = r.reshape(b, H_C, DK_C, H_C, DV_C)
    return jnp.stack([r[:, h, :, h, :] for h in range(H_C)], axis=1)


def _run(x, pos0, caches, weights):
    (norm_ffn1, wg1, wu1, wd1, norm_mix, w_in, conv_w, a_log, dt_bias, gdn_norm, rel_bias, w_out,
     norm_ffn2, wg2, wu2, wd2, norm_final) = weights
    b, t, _ = x.shape
    n = b * t
    tb = min(MIX_TB, t)
    assert t % tb == 0 and tb % CHUNK == 0, (t, tb)
    rope_tab, pow_tab, dmask, sq_tab = _ret_tables(t, pos0, tb)
    gf = norm_final.reshape(1, D_MODEL)
    norm_ffn1, norm_mix, norm_ffn2 = (g.reshape(DEPTH, 1, D_MODEL) for g in (norm_ffn1, norm_mix, norm_ffn2))
    xf = x.reshape(n, D_MODEL)
    new = []
    for l in range(DEPTH):
        if caches is None:
            s_gdn0 = jnp.zeros((b, H_A, DK_A, DV_A), F32)
            cb0 = jnp.zeros((b, SUBLANES, QKV_A), F32)
            r0 = jnp.zeros((b, MIX_C, MIX_C), F32)
            ck = cv = jnp.zeros((b, BAND, MIX_B), F32)
            n_invalid = BAND
        else:
            state_gdn, state_conv, cache_k, cache_v, state_ret = caches
            s_gdn0 = state_gdn[l]
            cb0 = jnp.pad(state_conv[l], ((0, 0), (SUBLANES - (CONV_W - 1), 0), (0, 0)))
            r0 = _block_diag(state_ret[l])
            ck = cache_k[l].reshape(b, BAND, MIX_B)
            cv = cache_v[l].reshape(b, BAND, MIX_B)
            n_invalid = 0
        xf = _ffn(xf, norm_ffn1, wg1, wu1, wd1, l)
        qkv, zba, bc16, bc32, kv_new, conv_tail = _inproj(xf.reshape(b, t, D_MODEL), norm_mix, w_in,
                                                          conv_w[l], cb0, l)
        decay_lanes = lambda v: jnp.pad(v.astype(F32), (H_A, LANES - 2 * H_A))
        prm = jnp.concatenate([decay_lanes(-jnp.exp(a_log[l].astype(F32)))[None],
                               decay_lanes(dt_bias[l])[None], gdn_norm[l].astype(F32)[None],
                               jnp.zeros((SUBLANES - 3, LANES), F32)], axis=0)
        oa, s_gdn = _gdn(qkv, zba, prm, s_gdn0, tb)
        ob = _band(bc16, _band_bias_table(rel_bias[l], tb), ck, cv, tb, n_invalid)
        oc, r_new = _ret(bc32, bc16, rope_tab, pow_tab, dmask, sq_tab, r0, tb)
        xf = _mix_ffn(xf, oa.reshape(n, MIX_A), ob.reshape(n, MIX_B), oc.reshape(n, MIX_C), w_out,
                      norm_ffn2, wg2, wu2, wd2, gf, l, l == DEPTH - 1)
        conv_new = conv_tail[:, SUBLANES - (CONV_W - 1):, :]
        kb_new = kv_new[:, :, :MIX_B].reshape(b, -1, H_B, DH_B)
        vb_new = kv_new[:, :, MIX_B:].reshape(b, -1, H_B, DH_B)
        new.append((s_gdn, conv_new, kb_new, vb_new, _diag_blocks(r_new)))
    g_, c_, k_, v_, r_ = zip(*new)
    return (xf.reshape(b, t, D_MODEL), jnp.stack(g_), jnp.stack(c_), jnp.stack(k_), jnp.stack(v_),
            jnp.stack(r_))


def kernel(x_prompt, x_sample, state_gdn, state_conv, cache_band_k, cache_band_v, state_ret, norm_ffn1, w_ffn1_gate, w_ffn1_up, w_ffn1_down, norm_mix, w_in, conv_w, a_log, dt_bias, gdn_norm, rel_bias, w_out, norm_ffn2, w_ffn2_gate, w_ffn2_up, w_ffn2_down, norm_final):
    weights = (norm_ffn1, w_ffn1_gate.astype(BF16), w_ffn1_up.astype(BF16), w_ffn1_down.astype(BF16),
               norm_mix, _split_w_in(w_in), conv_w, a_log, dt_bias, gdn_norm, rel_bias,
               w_out.astype(BF16), norm_ffn2, w_ffn2_gate.astype(BF16), w_ffn2_up.astype(BF16),
               w_ffn2_down.astype(BF16), norm_final)
    p = _run(x_prompt, 0, None, weights)
    s = _run(x_sample, PAST_LEN, (state_gdn, state_conv, cache_band_k, cache_band_v, state_ret), weights)
    return (p[0], s[0]) + p[1:] + s[1:]
```
